```python
import math
import jax, jax.numpy as jnp
from jax import lax
import numpy as np


D_MODEL = 1024
BATCH = 8
SEQ = 2048
DEPTH = 1

ATT_WIDTH = D_MODEL // 2
RWKV_WIDTH = D_MODEL - ATT_WIDTH
DA_HEAD_DIM = 64
DA_HEADS = ATT_WIDTH // (2 * DA_HEAD_DIM)
DA_V_DIM = 2 * DA_HEAD_DIM
Q_BLOCK = 128
RW_HEAD = 64
RW_HEADS = RWKV_WIDTH // RW_HEAD
DECAY_LORA = 64
ICLR_LORA = 64
GATE_LORA = 128
RWKV_COLS = 3 * RWKV_WIDTH + 2 * DECAY_LORA + 2 * ICLR_LORA + GATE_LORA
IN_COLS = 3 * ATT_WIDTH + RWKV_COLS
GN_EPS = 64e-5
PEER_HEADS = 8
PEER_NKEYS = 128
PEER_EXPERTS = PEER_NKEYS * PEER_NKEYS
PEER_TOPK = 16
PEER_QDIM = 256
PEER_SUBDIM = PEER_QDIM // 2
PEER_CHUNK = 128
NORM_EPS = 1e-6

kernel_name = "hymba_diffattn_rwkv7_peer_adaln"


def rms_norm(x, g, eps=NORM_EPS):
    xf = x.astype(jnp.float32)
    y = xf * lax.rsqrt(jnp.mean(xf * xf, axis=-1, keepdims=True) + eps)
    return (y * g.astype(jnp.float32)).astype(x.dtype)


def alibi_slopes(n):
    return jnp.asarray([2.0 ** (-8.0 * (h + 1) / n) for h in range(n)], jnp.float32)


def lambda_init_fn(layer_idx):
    return 0.8 - 0.6 * math.exp(-0.3 * layer_idx)


def diff_attention(q, k, v, qn_g, kn_g, lam, subln_g, lambda_init):
    B, S = q.shape[0], q.shape[1]
    q = rms_norm(q, qn_g)
    k = rms_norm(k, kn_g)
    lf = lam.astype(jnp.float32)
    lam_full = jnp.exp(jnp.sum(lf[0] * lf[1])) - jnp.exp(jnp.sum(lf[2] * lf[3])) + lambda_init
    slopes = alibi_slopes(DA_HEADS)
    scale = DA_HEAD_DIM ** -0.5
    kpos = jnp.arange(S)
    n_blk = S // Q_BLOCK
    qb = q.reshape(B, n_blk, Q_BLOCK, DA_HEADS, 2, DA_HEAD_DIM).transpose(1, 0, 2, 3, 4, 5)
    starts = jnp.arange(n_blk) * Q_BLOCK

    def block(args):
        qblk, start = args
        s = jnp.einsum('bqhcd,bkhcd->bhcqk', qblk, k, preferred_element_type=jnp.float32) * scale
        qpos = start + jnp.arange(Q_BLOCK)
        dist = jnp.abs(qpos[:, None] - kpos[None, :]).astype(jnp.float32)
        s = s - slopes[None, :, None, None, None] * dist[None, None, None]
        p = jax.nn.softmax(s, axis=-1)
        a = p[:, :, 0] - lam_full * p[:, :, 1]
        return jnp.einsum('bhqk,bkhv->bqhv', a.astype(v.dtype), v)

    o = lax.map(block, (qb, starts))
    o = o.transpose(1, 0, 2, 3, 4).reshape(B, S, DA_HEADS, DA_V_DIM)
    o = rms_norm(o, subln_g) * (1.0 - lambda_init)
    return o.reshape(B, S, DA_HEADS * DA_V_DIM)


def centred_shift(p, mu):
    prev = jnp.pad(p, ((0, 0), (1, 0), (0, 0)))[:, :-1]
    nxt = jnp.pad(p, ((0, 0), (0, 1), (0, 0)))[:, 1:]
    return p + mu[0] * (prev - p) + mu[1] * (nxt - p)


def rwkv7_bidir(p, mu, w0, w2, a0, a2, g2, k_k, k_a, r_k, ln_g, ln_b):
    B, S = p.shape[0], p.shape[1]
    out_dtype = p.dtype
    p = centred_shift(p.astype(jnp.float32), mu.astype(jnp.float32))
    W = RWKV_WIDTH
    o1 = 3 * W
    o2 = o1 + 2 * DECAY_LORA
    o3 = o2 + 2 * ICLR_LORA
    r, k, v = p[..., :W], p[..., W:2 * W], p[..., 2 * W:o1]
    wd = p[..., o1:o2].reshape(B, S, 2, DECAY_LORA)
    ad = p[..., o2:o3].reshape(B, S, 2, ICLR_LORA)
    gd = p[..., o3:]
    f = lambda t: t.astype(jnp.float32)
    w = -jax.nn.softplus(-(f(w0) + jnp.einsum('bsdr,drc->bsdc', jnp.tanh(wd), f(w2)))) - 0.5
    decay = jnp.exp(-jnp.exp(w))
    a = jax.nn.sigmoid(f(a0) + jnp.einsum('bsdr,drc->bsdc', ad, f(a2)))
    g = jax.nn.sigmoid(gd) @ f(g2)
    kk = (k * f(k_k)).reshape(B, S, RW_HEADS, RW_HEAD)
    kk = kk / jnp.maximum(jnp.sqrt(jnp.sum(kk * kk, -1, keepdims=True)), 1e-12)
    k_dir = k[:, :, None, :] * (1.0 + (a - 1.0) * f(k_a))
    hd = lambda t: t.reshape(t.shape[:-1] + (RW_HEADS, RW_HEAD))
    rh, kh, vh = hd(r), hd(k), hd(v)

    def shared_dirs(t):
        return jnp.stack([t, t[:, ::-1]], 0).transpose(2, 0, 1, 3, 4)

    def split_dirs(t):
        return jnp.stack([t[:, :, 0], t[:, ::-1, 1]], 0).transpose(2, 0, 1, 3, 4)

    xs = (shared_dirs(rh), split_dirs(hd(decay)), split_dirs(hd(k_dir)), shared_dirs(vh),
          shared_dirs(kk), split_dirs(hd(a)))

    def step(state, inp):
        r_t, dec_t, k_t, v_t, kk_t, a_t = inp
        sa = jnp.einsum('dbhvk,dbhk->dbhv', state, -kk_t)
        new = (state * dec_t[..., None, :] + sa[..., :, None] * (kk_t * a_t)[..., None, :]
               + v_t[..., :, None] * k_t[..., None, :])
        y_fwd = jnp.einsum('bhvk,bhk->bhv', new[0], r_t[0])
        y_bwd = jnp.einsum('bhvk,bhk->bhv', state[1], r_t[1])
        return new, jnp.stack([y_fwd, y_bwd], 0)

    s0 = jnp.zeros((2, B, RW_HEADS, RW_HEAD, RW_HEAD), jnp.float32)
    _, ys = lax.scan(step, s0, xs)
    y = (ys[:, 0] + ys[::-1, 1]).transpose(1, 0, 2, 3)
    mean = jnp.mean(y, -1, keepdims=True)
    var = jnp.mean(jnp.square(y - mean), -1, keepdims=True)
    yn = ((y - mean) * lax.rsqrt(var + GN_EPS)).reshape(B, S, W) * f(ln_g) + f(ln_b)
    bonus = (jnp.sum(rh * kh * f(r_k), -1, keepdims=True) * vh).reshape(B, S, W)
    return ((yn + bonus) * g).astype(out_dtype)


def peer_ffn(h, wq, keys, U, V):
    B, S, D = h.shape
    T = B * S
    hf = h.reshape(T, D)
    q = (hf @ wq).reshape(T, PEER_HEADS, 2, PEER_SUBDIM)
    s = jnp.einsum('thpd,hpkd->thpk', q, keys, preferred_element_type=jnp.float32)
    s1, i1 = lax.top_k(s[:, :, 0], PEER_TOPK)
    s2, i2 = lax.top_k(s[:, :, 1], PEER_TOPK)
    cand = (s1[..., :, None] + s2[..., None, :]).reshape(T, PEER_HEADS, PEER_TOPK * PEER_TOPK)
    sc, ci = lax.top_k(cand, PEER_TOPK)
    e1 = jnp.take_along_axis(i1, ci // PEER_TOPK, axis=-1)
    e2 = jnp.take_along_axis(i2, ci % PEER_TOPK, axis=-1)
    idx = e1 * PEER_NKEYS + e2
    gates = jax.nn.softmax(sc, axis=-1)
    n_chunks = T // PEER_CHUNK

    def chunk(args):
        xc, ic, gc = args
        act = jax.nn.gelu(jnp.einsum('chkd,cd->chk', U[ic], xc, preferred_element_type=jnp.float32),
                          approximate=False)
        return jnp.einsum('chk,chkd->cd', (gc * act).astype(V.dtype), V[ic])

    out = lax.map(chunk, (hf.reshape(n_chunks, PEER_CHUNK, D),
                          idx.reshape(n_chunks, PEER_CHUNK, PEER_HEADS, PEER_TOPK),
                          gates.reshape(n_chunks, PEER_CHUNK, PEER_HEADS, PEER_TOPK)))
    return out.reshape(B, S, D).astype(h.dtype)


def setup_inputs(seed: int = 0) -> dict:
    key = jax.random.key(seed)
    ks = jax.random.split(key, 32)
    f32 = jnp.float32
    L, D = DEPTH, D_MODEL
    nrm = lambda k, shape, s: jax.random.normal(k, shape, f32) * s
    gain = lambda k, shape: 1.0 + 0.05 * jax.random.normal(k, shape, f32)
    return {
        'x': nrm(ks[0], (BATCH, SEQ, D), 1.0),
        'c': nrm(ks[1], (BATCH, D), 1.0),
        'ada_w': nrm(ks[2], (L, D, 6 * D), D ** -0.5),
        'ada_b': nrm(ks[3], (L, 6 * D), 0.02),
        'norm1_g': gain(ks[4], (L, D)),
        'w_in': nrm(ks[5], (L, D, IN_COLS), D ** -0.5),
        'da_qnorm_g': gain(ks[6], (L, DA_HEAD_DIM)),
        'da_knorm_g': gain(ks[7], (L, DA_HEAD_DIM)),
        'da_lambda': nrm(ks[8], (L, 4, DA_HEAD_DIM), 0.1),
        'da_subln_g': gain(ks[9], (L, DA_V_DIM)),
        'rw_shift_mu': jax.random.uniform(ks[10], (L, 2, RWKV_COLS), f32, 0.0, 0.5),
        'rw_w0': jax.random.uniform(ks[11], (L, 2, RWKV_WIDTH), f32, -6.0, 1.0),
        'rw_w2': nrm(ks[12], (L, 2, DECAY_LORA, RWKV_WIDTH), 0.1),
        'rw_a0': nrm(ks[13], (L, 2, RWKV_WIDTH), 0.3),
        'rw_a2': nrm(ks[14], (L, 2, ICLR_LORA, RWKV_WIDTH), 0.1),
        'rw_g2': nrm(ks[15], (L, GATE_LORA, RWKV_WIDTH), GATE_LORA ** -0.5),
        'rw_k_k': 0.85 + 0.05 * jax.random.normal(ks[16], (L, RWKV_WIDTH), f32),
        'rw_k_a': gain(ks[17], (L, RWKV_WIDTH)),
        'rw_r_k': nrm(ks[18], (L, RW_HEADS, RW_HEAD), 0.1),
        'rw_ln_g': gain(ks[19], (L, RWKV_WIDTH)),
        'rw_ln_b': nrm(ks[20], (L, RWKV_WIDTH), 0.02),
        'w_out': nrm(ks[21], (L, D, D), D ** -0.5),
        'norm2_g': gain(ks[22], (L, D)),
        'peer_wq': nrm(ks[23], (L, D, PEER_HEADS * PEER_QDIM), D ** -0.5),
        'peer_keys': nrm(ks[24], (L, PEER_HEADS, 2, PEER_NKEYS, PEER_SUBDIM), PEER_SUBDIM ** -0.5),
        'peer_u': nrm(ks[25], (L, PEER_EXPERTS, D), D ** -0.5),
        'peer_v': nrm(ks[26], (L, PEER_EXPERTS, D), PEER_TOPK ** -0.5),
    }


def reference(x, c, ada_w, ada_b, norm1_g, w_in, da_qnorm_g, da_knorm_g, da_lambda, da_subln_g,
              rw_shift_mu, rw_w0, rw_w2, rw_a0, rw_a2, rw_g2, rw_k_k, rw_k_a, rw_r_k, rw_ln_g, rw_ln_b,
              w_out, norm2_g, peer_wq, peer_keys, peer_u, peer_v):
    B, S, D = x.shape
    for l in range(DEPTH):
        mod = (jax.nn.silu(c) @ ada_w[l] + ada_b[l]).reshape(B, 6, D)
        shift1, scale1, gate1, shift2, scale2, gate2 = [mod[:, i, None, :] for i in range(6)]
        h = rms_norm(x, norm1_g[l]) * (1.0 + scale1) + shift1
        p = h @ w_in[l]
        q = p[..., :ATT_WIDTH].reshape(B, S, DA_HEADS, 2, DA_HEAD_DIM)
        k = p[..., ATT_WIDTH:2 * ATT_WIDTH].reshape(B, S, DA_HEADS, 2, DA_HEAD_DIM)
        v = p[..., 2 * ATT_WIDTH:3 * ATT_WIDTH].reshape(B, S, DA_HEADS, DA_V_DIM)
        att = diff_attention(q, k, v, da_qnorm_g[l], da_knorm_g[l], da_lambda[l], da_subln_g[l],
                             lambda_init_fn(l))
        rw = rwkv7_bidir(p[..., 3 * ATT_WIDTH:], rw_shift_mu[l], rw_w0[l], rw_w2[l], rw_a0[l], rw_a2[l],
                         rw_g2[l], rw_k_k[l], rw_k_a[l], rw_r_k[l], rw_ln_g[l], rw_ln_b[l])
        x = x + gate1 * (jnp.concatenate([att, rw], axis=-1) @ w_out[l])
        h2 = rms_norm(x, norm2_g[l]) * (1.0 + scale2) + shift2
        x = x + gate2 * peer_ffn(h2, peer_wq[l], peer_keys[l], peer_u[l], peer_v[l])
    return x
```

```python
import functools
import math

import jax
import jax.numpy as jnp
from jax import lax
from jax.experimental import pallas as pl
from jax.experimental.pallas import tpu as pltpu

F32 = jnp.float32
BF16 = jnp.bfloat16
HIGHEST = lax.Precision.HIGHEST

LANES = 128
SUBLANES = 8
VMEM_LIMIT_BYTES = 56 * 1024 * 1024

DA_HEAD_DIM = 64
DA_HEADS = 4
RW_HEAD = 64
RW_HEADS = 8
RW_WIDTH = RW_HEADS * RW_HEAD
LORA_COLS = 128
RWKV_COLS = 3 * RW_WIDTH + 3 * LORA_COLS
ATT_COLS = 3 * 2 * DA_HEAD_DIM * DA_HEADS
GN_EPS = 64e-5
NORM_EPS = 1e-6
LAMBDA_INIT = 0.8 - 0.6 * math.exp(-0.3 * 0)
PEER_HEADS = 8
PEER_NKEYS = 128
PEER_TOPK = 16
RW_CHUNK = 64
NEG_INF = float("-inf")


def _params(*sem):
    return pltpu.CompilerParams(dimension_semantics=sem, vmem_limit_bytes=VMEM_LIMIT_BYTES)


def _dot(a, b, precision=None):
    return jnp.dot(a, b, preferred_element_type=F32, precision=precision)


def _dot_nt(a, b, precision=None):
    return lax.dot_general(a, b, (((1,), (1,)), ((), ())), preferred_element_type=F32, precision=precision)


def _dot_tn(a, b, precision=None):
    return lax.dot_general(a, b, (((0,), (0,)), ((), ())), preferred_element_type=F32, precision=precision)


def _group_matrix(n, group, value):
    shift = group.bit_length() - 1
    r = lax.broadcasted_iota(jnp.int32, (n, n), 0) >> shift
    c = lax.broadcasted_iota(jnp.int32, (n, n), 1) >> shift
    return jnp.where(r == c, value, 0.0).astype(F32)


def _ada_kernel(c_ref, w_ref, b_ref, o_ref):
    c = c_ref[...]
    s = c * jax.nn.sigmoid(c)
    o_ref[...] = _dot(s, w_ref[...], HIGHEST) + b_ref[...]


def _ada_mod(c, w, b):
    bsz, d = c.shape
    n = w.shape[1]
    tn = 1024
    return pl.pallas_call(
        _ada_kernel,
        grid=(n // tn,),
        in_specs=[pl.BlockSpec((bsz, d), lambda j: (0, 0)),
                  pl.BlockSpec((d, tn), lambda j: (0, j)),
                  pl.BlockSpec((1, tn), lambda j: (0, j))],
        out_specs=pl.BlockSpec((bsz, tn), lambda j: (0, j)),
        out_shape=jax.ShapeDtypeStruct((bsz, n), F32),
        compiler_params=_params("arbitrary"),
        name="ada_mod",
    )(c, w, b.reshape(1, n))


def _inproj_kernel(x_ref, mod_ref, g_ref, wa_ref, wr_ref, oa_ref, or_ref):
    x = x_ref[...]
    ms = jnp.mean(x * x, axis=-1, keepdims=True)
    y = x * lax.rsqrt(ms + NORM_EPS) * g_ref[...]
    m = mod_ref[0]
    h = (y * (1.0 + m[1:2]) + m[0:1]).astype(BF16)
    oa_ref[...] = _dot(h, wa_ref[...])
    or_ref[...] = _dot(h, wr_ref[...])


def _in_proj(x2, mod3, g, w_att, w_rw, seq, tm=256):
    t, d = x2.shape
    na, nr = w_att.shape[1], w_rw.shape[1]
    per_b = seq // tm
    return pl.pallas_call(
        _inproj_kernel,
        grid=(t // tm,),
        in_specs=[pl.BlockSpec((tm, d), lambda i: (i, 0)),
                  pl.BlockSpec((1, 6, d), lambda i: (i // per_b, 0, 0)),
                  pl.BlockSpec((1, d), lambda i: (0, 0)),
                  pl.BlockSpec((d, na), lambda i: (0, 0)),
                  pl.BlockSpec((d, nr), lambda i: (0, 0))],
        out_specs=[pl.BlockSpec((tm, na), lambda i: (i, 0)),
                   pl.BlockSpec((tm, nr), lambda i: (i, 0))],
        out_shape=[jax.ShapeDtypeStruct((t, na), F32), jax.ShapeDtypeStruct((t, nr), F32)],
        compiler_params=_params("arbitrary"),
        name="in_proj",
    )(x2, mod3, g.reshape(1, d), w_att, w_rw)


def _attn_kernel(tq, q_ref, k_ref, v_ref, qg_ref, kg_ref, lam_ref, slope_ref, sg_ref, o_ref, qs, ks, vs):
    seq = q_ref.shape[1]
    width = 2 * DA_HEAD_DIM
    avg = _group_matrix(width, DA_HEAD_DIM, 1.0 / DA_HEAD_DIM)

    def qk_norm(x, g):
        ms = _dot(x * x, avg)
        return x * lax.rsqrt(ms + NORM_EPS) * g

    qs[...] = (qk_norm(q_ref[0], qg_ref[...]) * (DA_HEAD_DIM ** -0.5)).astype(BF16)
    ks[...] = qk_norm(k_ref[0], kg_ref[...]).astype(BF16)
    vs[...] = v_ref[0].astype(BF16)

    lam = lam_ref[...]
    lam_full = (jnp.exp(jnp.sum(lam[0:1] * lam[1:2], axis=-1, keepdims=True))
                - jnp.exp(jnp.sum(lam[2:3] * lam[3:4], axis=-1, keepdims=True)) + LAMBDA_INIT)
    slope = slope_ref[0][:, 0:1]
    first = lax.broadcasted_iota(jnp.int32, (1, width), 1) < DA_HEAD_DIM
    cols = lax.broadcasted_iota(jnp.int32, (1, seq), 1)
    sg = sg_ref[...] * (1.0 - LAMBDA_INIT)

    def body(i, carry):
        r0 = pl.multiple_of(i * tq, tq)
        qt = qs[pl.ds(r0, tq), :]
        q0 = jnp.where(first, qt, jnp.zeros_like(qt))
        q1 = jnp.where(first, jnp.zeros_like(qt), qt)
        kk = ks[...]
        rows = r0 + lax.broadcasted_iota(jnp.int32, (tq, 1), 0)
        bias = slope * jnp.abs(rows - cols).astype(F32)
        s0 = _dot_nt(q0, kk) - bias
        s1 = _dot_nt(q1, kk) - bias
        p0 = jnp.exp(s0 - jnp.max(s0, axis=-1, keepdims=True))
        p1 = jnp.exp(s1 - jnp.max(s1, axis=-1, keepdims=True))
        w0 = 1.0 / jnp.sum(p0, axis=-1, keepdims=True)
        w1 = lam_full / jnp.sum(p1, axis=-1, keepdims=True)
        a = (p0 * w0 - p1 * w1).astype(BF16)
        o = _dot(a, vs[...])
        ms = jnp.mean(o * o, axis=-1, keepdims=True)
        o_ref[0, pl.ds(r0, tq), :] = o * lax.rsqrt(ms + NORM_EPS) * sg
        return carry

    lax.fori_loop(0, seq // tq, body, 0)


def _diff_attention(p_att3, qn_g, kn_g, lam, subln_g, tq=256):
    bsz, seq, _ = p_att3.shape
    width = 2 * DA_HEAD_DIM
    slopes = jnp.asarray([2.0 ** (-8.0 * (h + 1) / DA_HEADS) for h in range(DA_HEADS)], F32)
    slopes = jnp.broadcast_to(slopes[:, None, None], (DA_HEADS, 1, width))
    blk = lambda off: pl.BlockSpec((1, seq, width), lambda b, h: (b, 0, off + h))
    full = lambda shape: pl.BlockSpec(shape, lambda b, h: (0,) * len(shape))
    return pl.pallas_call(
        functools.partial(_attn_kernel, tq),
        grid=(bsz, DA_HEADS),
        in_specs=[blk(0), blk(DA_HEADS), blk(2 * DA_HEADS),
                  full((1, width)), full((1, width)), full((4, DA_HEAD_DIM)),
                  pl.BlockSpec((1, 1, width), lambda b, h: (h, 0, 0)),
                  full((1, width))],
        out_specs=pl.BlockSpec((1, seq, width), lambda b, h: (b, 0, h)),
        out_shape=jax.ShapeDtypeStruct((bsz, seq, DA_HEADS * width), F32),
        scratch_shapes=[pltpu.VMEM((seq, width), BF16)] * 3,
        compiler_params=_params("arbitrary", "arbitrary"),
        name="diff_attention",
    )(p_att3, p_att3, p_att3,
      jnp.tile(qn_g.reshape(1, DA_HEAD_DIM), (1, 2)), jnp.tile(kn_g.reshape(1, DA_HEAD_DIM), (1, 2)),
      lam, slopes, subln_g.reshape(1, width))


def _rwkv_kernel(direction, n_chunks, *refs):
    if direction == 0:
        (cur_ref, prev_ref, next_ref, mu_ref, w0_ref, w2_ref, a0_ref, a2_ref, g2_ref, kk_ref, ka_ref, hs_ref,
         out_ref, st_ref) = refs
    else:
        (cur_ref, prev_ref, next_ref, mu_ref, w0_ref, w2_ref, a0_ref, a2_ref, g2_ref, kk_ref, ka_ref, hs_ref,
         rk_ref, lng_ref, lnb_ref, y0_ref, out_ref, st_ref) = refs
    ch = RW_CHUNK
    w = RW_WIDTH
    step = pl.program_id(1)
    chunk = step if direction == 0 else n_chunks - 1 - step

    @pl.when(step == 0)
    def _():
        st_ref[...] = jnp.zeros_like(st_ref)

    pc = cur_ref[0]
    row = lax.broadcasted_iota(jnp.int32, (ch, 1), 0)
    pv = prev_ref[0][SUBLANES - 1:SUBLANES, :] * (chunk > 0).astype(F32)
    nx = next_ref[0][0:1, :] * (chunk < n_chunks - 1).astype(F32)
    prev = jnp.where(row == 0, pv, pltpu.roll(pc, 1, 0))
    nxt = jnp.where(row == ch - 1, nx, pltpu.roll(pc, ch - 1, 0))
    mu = mu_ref[...]
    ps = pc + mu[0:1] * (prev - pc) + mu[1:2] * (nxt - pc)
    r, k, v = ps[:, 0:w], ps[:, w:2 * w], ps[:, 2 * w:3 * w]
    wd = ps[:, 3 * w:3 * w + LORA_COLS]
    ad = ps[:, 3 * w + LORA_COLS:3 * w + 2 * LORA_COLS]
    gd = ps[:, 3 * w + 2 * LORA_COLS:3 * w + 3 * LORA_COLS]

    z = w0_ref[...] + _dot(jnp.tanh(wd), w2_ref[...])
    logdec = -jax.nn.sigmoid(z) * math.exp(-0.5)
    a = jax.nn.sigmoid(a0_ref[...] + _dot(ad, a2_ref[...]))
    head_sum = hs_ref[...]

    def per_head_sum(x):
        return _dot(x.astype(BF16), head_sum)

    kkr = k * kk_ref[...]
    kk = kkr / jnp.maximum(jnp.sqrt(per_head_sum(kkr * kkr)), 1e-12)
    kd = k * (1.0 + (a - 1.0) * ka_ref[...])

    ti = lax.broadcasted_iota(jnp.int32, (ch, ch), 0)
    tj = lax.broadcasted_iota(jnp.int32, (ch, ch), 1)
    before = (tj < ti) if direction == 0 else (tj > ti)
    upto = before | (tj == ti)
    cum = _dot(upto.astype(F32), logdec, HIGHEST)
    cum_ex = cum - logdec
    total = cum[ch - 1:ch] if direction == 0 else cum[0:1]
    w_in, w_ex, w_inv, w_tot = jnp.exp(cum), jnp.exp(cum_ex), jnp.exp(-cum), jnp.exp(total)
    a_bar = -kk * w_ex
    b_bar = kk * a * w_inv
    k_bar = kd * w_inv
    r_bar = r * (w_in if direction == 0 else w_ex)
    ymask = upto if direction == 0 else before

    lane = lax.broadcasted_iota(jnp.int32, (1, 2 * RW_HEAD), 1)
    pr = lax.broadcasted_iota(jnp.int32, (2 * RW_HEAD, 2 * RW_HEAD), 0) < RW_HEAD
    pcn = lax.broadcasted_iota(jnp.int32, (2 * RW_HEAD, 2 * RW_HEAD), 1) < RW_HEAD
    same_head = pr == pcn
    eye = (ti == tj).astype(F32)
    off_masks = []
    for lvl in range(int(math.log2(ch))):
        same_pair = (ti >> (lvl + 1)) == (tj >> (lvl + 1))
        later, earlier = (ti, tj) if direction == 0 else (tj, ti)
        off_masks.append(same_pair & (((later >> lvl) & 1) == 1) & (((earlier >> lvl) & 1) == 0))
    ys = []
    for pair in range(RW_HEADS // 2):
        sl = slice(pair * 2 * RW_HEAD, (pair + 1) * 2 * RW_HEAD)
        a_p, b_p, k_p, r_p, v_p = a_bar[:, sl], b_bar[:, sl], k_bar[:, sl], r_bar[:, sl], v[:, sl]
        s_p = st_ref[pair]
        sa = _dot_nt(a_p, s_p)
        y_p = _dot_nt(r_p, s_p)
        u_p = jnp.zeros_like(sa)
        for hh in range(2):
            mh = (lane < RW_HEAD) if hh == 0 else (lane >= RW_HEAD)
            a_h = jnp.where(mh, a_p, 0.0)
            r_h = jnp.where(mh, r_p, 0.0)
            v_h = jnp.where(mh, v_p, 0.0)
            low = _dot_nt(a_h, b_p)
            inv = eye + jnp.where(off_masks[0], low, 0.0)
            for off in off_masks[1:]:
                inv = inv + _dot(_dot(inv, jnp.where(off, low, 0.0)), inv)
            rhs = jnp.where(mh, sa, 0.0) + _dot(jnp.where(before, _dot_nt(a_h, k_p), 0.0), v_h)
            u_h = _dot(inv, rhs)
            u_p = u_p + u_h
            y_p = (y_p + _dot(jnp.where(ymask, _dot_nt(r_h, b_p), 0.0), u_h)
                   + _dot(jnp.where(ymask, _dot_nt(r_h, k_p), 0.0), v_h))
        s_new = s_p + jnp.where(same_head, _dot_tn(u_p, b_p) + _dot_tn(v_p, k_p), 0.0)
        st_ref[pair] = s_new * w_tot[:, sl]
        ys.append(y_p)
    y = jnp.concatenate(ys, axis=1)

    if direction == 0:
        out_ref[0] = y
    else:
        y = y + y0_ref[0]
        mean = per_head_sum(y) * (1.0 / RW_HEAD)
        yc = y - mean
        var = per_head_sum(yc * yc) * (1.0 / RW_HEAD)
        yn = yc * lax.rsqrt(var + GN_EPS) * lng_ref[...] + lnb_ref[...]
        bonus = per_head_sum(r * k * rk_ref[...]) * v
        g = _dot(jax.nn.sigmoid(gd), g2_ref[...])
        out_ref[0] = (yn + bonus) * g


def _pad_lora(w2, direction):
    keep = jnp.arange(2)[:, None, None] == direction
    return jnp.where(keep, w2, 0.0).reshape(-1, w2.shape[-1])


def _rwkv_direction(direction, p_rw3, y0, mu, w0, w2, a0, a2, g2, k_k, k_a, r_k, ln_g, ln_b):
    bsz, seq, cols = p_rw3.shape
    ch, w = RW_CHUNK, RW_WIDTH
    n_chunks = seq // ch
    sub_per_chunk = ch // SUBLANES
    n_sub = seq // SUBLANES
    cidx = (lambda c: c) if direction == 0 else (lambda c: n_chunks - 1 - c)
    full = lambda shape: pl.BlockSpec(shape, lambda b, c: (0,) * len(shape))
    row = lambda x: x.reshape(1, w)
    in_specs = [pl.BlockSpec((1, ch, cols), lambda b, c: (b, cidx(c), 0)),
                pl.BlockSpec((1, SUBLANES, cols),
                             lambda b, c: (b, jnp.maximum(cidx(c) * sub_per_chunk - 1, 0), 0)),
                pl.BlockSpec((1, SUBLANES, cols),
                             lambda b, c: (b, jnp.minimum((cidx(c) + 1) * sub_per_chunk, n_sub - 1), 0)),
                full((2, cols)), full((1, w)), full((LORA_COLS, w)), full((1, w)), full((LORA_COLS, w)),
                full((LORA_COLS, w)), full((1, w)), full((1, w)), full((w, w))]
    head_sum = (jnp.arange(w)[:, None] // RW_HEAD == jnp.arange(w)[None, :] // RW_HEAD).astype(BF16)
    args = [p_rw3, p_rw3, p_rw3, mu, row(w0[direction]), _pad_lora(w2, direction), row(a0[direction]),
            _pad_lora(a2, direction), g2, row(k_k), row(k_a), head_sum]
    if direction == 1:
        in_specs += [full((1, w)), full((1, w)), full((1, w)),
                     pl.BlockSpec((1, ch, w), lambda b, c: (b, cidx(c), 0))]
        args += [row(r_k), row(ln_g), row(ln_b), y0]
    return pl.pallas_call(
        functools.partial(_rwkv_kernel, direction, n_chunks),
        grid=(bsz, n_chunks),
        in_specs=in_specs,
        out_specs=pl.BlockSpec((1, ch, w), lambda b, c: (b, cidx(c), 0)),
        out_shape=jax.ShapeDtypeStruct((bsz, seq, w), F32),
        scratch_shapes=[pltpu.VMEM((RW_HEADS // 2, 2 * RW_HEAD, 2 * RW_HEAD), F32)],
        compiler_params=_params("arbitrary", "arbitrary"),
        name=f"rwkv7_dir{direction}",
    )(*args)


def _outproj_kernel(att_ref, rw_ref, x_ref, mod_ref, wa_ref, wr_ref, g_ref, x1_ref, h2_ref):
    acc = _dot(att_ref[...].astype(BF16), wa_ref[...]) + _dot(rw_ref[...].astype(BF16), wr_ref[...])
    m = mod_ref[0]
    x1 = x_ref[...] + m[2:3] * acc
    x1_ref[...] = x1
    ms = jnp.mean(x1 * x1, axis=-1, keepdims=True)
    y = x1 * lax.rsqrt(ms + NORM_EPS) * g_ref[...]
    h2_ref[...] = (y * (1.0 + m[4:5]) + m[3:4]).astype(BF16)


def _out_proj(att2, rw2, x2, mod3, w_a, w_r, g, seq, tm=256):
    t, d = x2.shape
    ka, kr = att2.shape[1], rw2.shape[1]
    per_b = seq // tm
    return pl.pallas_call(
        _outproj_kernel,
        grid=(t // tm,),
        in_specs=[pl.BlockSpec((tm, ka), lambda i: (i, 0)),
                  pl.BlockSpec((tm, kr), lambda i: (i, 0)),
                  pl.BlockSpec((tm, d), lambda i: (i, 0)),
                  pl.BlockSpec((1, 6, d), lambda i: (i // per_b, 0, 0)),
                  pl.BlockSpec((ka, d), lambda i: (0, 0)),
                  pl.BlockSpec((kr, d), lambda i: (0, 0)),
                  pl.BlockSpec((1, d), lambda i: (0, 0))],
        out_specs=[pl.BlockSpec((tm, d), lambda i: (i, 0)),
                   pl.BlockSpec((tm, d), lambda i: (i, 0))],
        out_shape=[jax.ShapeDtypeStruct((t, d), F32), jax.ShapeDtypeStruct((t, d), BF16)],
        compiler_params=_params("arbitrary"),
        name="out_proj",
    )(att2, rw2, x2, mod3, w_a, w_r, g.reshape(1, d))


def _topk_ranks(s, vals_ref):
    nk, tm = s.shape
    kiota = lax.broadcasted_iota(jnp.int32, (nk, tm), 0)

    def body(j, carry):
        cur, rank = carry
        m = jnp.max(cur, axis=0, keepdims=True)
        pick = jnp.min(jnp.where(cur == m, kiota, nk), axis=0, keepdims=True)
        hit = kiota == pick
        vals_ref[pl.ds(j, 1), :] = m
        return jnp.where(hit, NEG_INF, cur), jnp.where(hit, j, rank)

    _, rank = lax.fori_loop(0, PEER_TOPK, body, (s, jnp.full((nk, tm), PEER_TOPK, jnp.int32)))
    return rank


def _pair_candidates():
    pieces = []
    for lvl in range(4):
        cnt = PEER_TOPK // (lvl + 1)
        pieces.append(("row", lvl, lvl, cnt))
        if lvl + 1 < cnt:
            pieces.append(("col", lvl, lvl + 1, cnt))
    return pieces


def _route_kernel(h_ref, wq_ref, keys_ref, p1_ref, cnt_ref, p2_ref, rk2_ref, q_scr, v1_ref, v2_ref, sc_ref, ci_ref):
    tm = h_ref.shape[0]
    nk = PEER_NKEYS
    q = _dot(h_ref[...], wq_ref[...]).astype(BF16)
    for hp in range(2 * PEER_HEADS):
        q_scr[hp] = q[:, hp * nk:(hp + 1) * nk]

    pieces = _pair_candidates()
    big = 4 * PEER_TOPK * PEER_TOPK

    def head(h, carry):
        s1 = _dot_nt(keys_ref[2 * h], q_scr[2 * h])
        s2 = _dot_nt(keys_ref[2 * h + 1], q_scr[2 * h + 1])
        rank1 = _topk_ranks(s1, v1_ref)
        rank2 = _topk_ranks(s2, v2_ref)
        v1 = v1_ref[...]
        v2 = v2_ref[...]
        cands, cis = [], []
        for kind, fixed, lo, hi in pieces:
            rows = SUBLANES * ((hi + SUBLANES - 1) // SUBLANES)
            ridx = lax.broadcasted_iota(jnp.int32, (rows, 1), 0)
            valid = (ridx >= lo) & (ridx < hi)
            if kind == "row":
                vsum = v1[fixed:fixed + 1] + v2[0:rows]
                ci = fixed * PEER_TOPK + ridx
            else:
                vsum = v1[0:rows] + v2[fixed:fixed + 1]
                ci = ridx * PEER_TOPK + fixed
            cands.append(jnp.where(valid, vsum, NEG_INF))
            cis.append(jnp.broadcast_to(jnp.where(valid, ci, big), (rows, tm)))
        cand = jnp.concatenate(cands, axis=0)
        ci = jnp.concatenate(cis, axis=0)

        def pick(j, cur):
            m = jnp.max(cur, axis=0, keepdims=True)
            sel = jnp.min(jnp.where(cur == m, ci, big), axis=0, keepdims=True)
            sc_ref[pl.ds(j, 1), :] = m
            ci_ref[pl.ds(j, 1), :] = sel
            return jnp.where(ci == sel, NEG_INF, cur)

        lax.fori_loop(0, PEER_TOPK, pick, cand)
        sc = sc_ref[...]
        arank = ci_ref[...] >> (PEER_TOPK.bit_length() - 1)
        zsum = jnp.sum(jnp.exp(sc - sc[0:1]), axis=0, keepdims=True)
        cnt = jnp.zeros((nk, tm), F32)
        for a in range(PEER_TOPK):
            c_a = jnp.sum((arank == a).astype(F32), axis=0, keepdims=True)
            cnt = jnp.where(rank1 == a, c_a, cnt)
        p1_ref[h] = jnp.where(rank1 < PEER_TOPK, jnp.exp(s1 - v1[0:1]) / zsum, 0.0)
        cnt_ref[h] = cnt
        p2_ref[h] = jnp.where(rank2 < PEER_TOPK, jnp.exp(s2 - v2[0:1]), 0.0)
        rk2_ref[h] = rank2.astype(F32)
        return carry

    lax.fori_loop(0, PEER_HEADS, head, 0)


def _route(h2, wq, keys, tm=256):
    t, d = h2.shape
    nq = wq.shape[1]
    nk = PEER_NKEYS
    out = jax.ShapeDtypeStruct((PEER_HEADS, nk, t), F32)
    ospec = pl.BlockSpec((PEER_HEADS, nk, tm), lambda i: (0, 0, i))
    return pl.pallas_call(
        _route_kernel,
        grid=(t // tm,),
        in_specs=[pl.BlockSpec((tm, d), lambda i: (i, 0)),
                  pl.BlockSpec((d, nq), lambda i: (0, 0)),
                  pl.BlockSpec((2 * PEER_HEADS, nk, nk), lambda i: (0, 0, 0))],
        out_specs=[ospec] * 4,
        out_shape=[out] * 4,
        scratch_shapes=[pltpu.VMEM((2 * PEER_HEADS, tm, nk), BF16),
                        pltpu.VMEM((PEER_TOPK, tm), F32), pltpu.VMEM((PEER_TOPK, tm), F32),
                        pltpu.VMEM((PEER_TOPK, tm), F32), pltpu.VMEM((PEER_TOPK, tm), jnp.int32)],
        compiler_params=_params("arbitrary"),
        name="peer_route",
    )(h2, wq, keys)


def _expert_kernel(n_eblk, h_ref, u_ref, vt_ref, p1_ref, cnt_ref, p2_ref, rk2_ref, x1_ref, mod_ref,
                   o_ref, acc_ref, w_scr):
    j = pl.program_id(1)
    te = u_ref.shape[0]
    nk = PEER_NKEYS

    @pl.when(j == 0)
    def _():
        acc_ref[...] = jnp.zeros_like(acc_ref)

    act = _dot_nt(u_ref[...], h_ref[...])
    for sb in range(te // nk):
        e1 = j * (te // nk) + sb
        gate = None
        for h in range(PEER_HEADS):
            c = cnt_ref[h, pl.ds(e1, 1), :]
            p = p1_ref[h, pl.ds(e1, 1), :]
            term = jnp.where(rk2_ref[h] < c, p2_ref[h], 0.0) * p
            gate = term if gate is None else gate + term
        a = act[sb * nk:(sb + 1) * nk]
        gelu = 0.5 * a * (1.0 + lax.erf(a * (2.0 ** -0.5)))
        w_scr[sb * nk:(sb + 1) * nk, :] = (gate * gelu).astype(BF16)
    acc_ref[...] += _dot(vt_ref[...], w_scr[...])

    @pl.when(j == n_eblk - 1)
    def _():
        o_ref[...] = x1_ref[...] + mod_ref[0][5:6] * acc_ref[...].T


def _experts(h2, u, vt, p1, cnt, p2, rk2, x1, mod3, seq, tm=256, te=512):
    t, d = h2.shape
    ne = u.shape[0]
    nk = PEER_NKEYS
    n_eblk = ne // te
    per_b = seq // tm
    rspec = pl.BlockSpec((PEER_HEADS, nk, tm), lambda i, j: (0, 0, i))
    return pl.pallas_call(
        functools.partial(_expert_kernel, n_eblk),
        grid=(t // tm, n_eblk),
        in_specs=[pl.BlockSpec((tm, d), lambda i, j: (i, 0)),
                  pl.BlockSpec((te, d), lambda i, j: (j, 0)),
                  pl.BlockSpec((d, te), lambda i, j: (0, j)),
                  rspec, rspec, rspec, rspec,
                  pl.BlockSpec((tm, d), lambda i, j: (i, 0)),
                  pl.BlockSpec((1, 6, d), lambda i, j: (i // per_b, 0, 0))],
        out_specs=pl.BlockSpec((tm, d), lambda i, j: (i, 0)),
        out_shape=jax.ShapeDtypeStruct((t, d), F32),
        scratch_shapes=[pltpu.VMEM((d, tm), F32), pltpu.VMEM((te, tm), BF16)],
        compiler_params=_params("arbitrary", "arbitrary"),
        name="peer_experts",
    )(h2, u, vt, p1, cnt, p2, rk2, x1, mod3)


def kernel(x, c, ada_w, ada_b, norm1_g, w_in, da_qnorm_g, da_knorm_g, da_lambda, da_subln_g, rw_shift_mu, rw_w0,
           rw_w2, rw_a0, rw_a2, rw_g2, rw_k_k, rw_k_a, rw_r_k, rw_ln_g, rw_ln_b, w_out, norm2_g, peer_wq,
           peer_keys, peer_u, peer_v):
    bsz, seq, d = x.shape
    depth = ada_w.shape[0]
    t = bsz * seq
    for l in range(depth):
        mod3 = _ada_mod(c, ada_w[l], ada_b[l]).reshape(bsz, 6, d)
        x2 = x.reshape(t, d)
        w_l = w_in[l].astype(BF16)
        p_att, p_rw = _in_proj(x2, mod3, norm1_g[l], w_l[:, :ATT_COLS], w_l[:, ATT_COLS:], seq)
        att = _diff_attention(p_att.reshape(bsz, seq, ATT_COLS), da_qnorm_g[l], da_knorm_g[l], da_lambda[l],
                              da_subln_g[l])
        p_rw3 = p_rw.reshape(bsz, seq, RWKV_COLS)
        rw_args = (rw_shift_mu[l], rw_w0[l], rw_w2[l], rw_a0[l], rw_a2[l], rw_g2[l], rw_k_k[l], rw_k_a[l],
                   rw_r_k[l].reshape(-1), rw_ln_g[l], rw_ln_b[l])
        y0 = _rwkv_direction(0, p_rw3, None, *rw_args)
        rw = _rwkv_direction(1, p_rw3, y0, *rw_args)
        wo = w_out[l].astype(BF16)
        aw = att.shape[-1]
        x1, h2 = _out_proj(att.reshape(t, aw), rw.reshape(t, RW_WIDTH), x2, mod3, wo[:aw], wo[aw:], norm2_g[l], seq)
        keys = peer_keys[l].reshape(2 * PEER_HEADS, PEER_NKEYS, -1).astype(BF16)
        p1, cnt, p2, rk2 = _route(h2, peer_wq[l].astype(BF16), keys)
        x = _experts(h2, peer_u[l].astype(BF16), peer_v[l].T.astype(BF16), p1, cnt, p2, rk2, x1, mod3,
                     seq).reshape(bsz, seq, d)
    return x
```

```python
import functools
import math

import jax
import jax.numpy as jnp
from jax import lax
from jax.experimental import pallas as pl
from jax.experimental.pallas import tpu as pltpu

F32 = jnp.float32
BF16 = jnp.bfloat16
HIGHEST = lax.Precision.HIGHEST

LANES = 128
SUBLANES = 8
MXU_WIDTH = 256
VMEM_LIMIT_BYTES = 56 * 1024 * 1024

DA_HEAD_DIM = 64
DA_HEADS = 4
RW_HEAD = 64
RW_HEADS = 8
RW_PAIRS = RW_HEADS // 2
RW_WIDTH = RW_HEADS * RW_HEAD
LORA_COLS = 128
RWKV_COLS = 3 * RW_WIDTH + 3 * LORA_COLS
ATT_COLS = 3 * 2 * DA_HEAD_DIM * DA_HEADS
GN_EPS = 64e-5
NORM_EPS = 1e-6
LAMBDA_INIT = 0.8 - 0.6 * math.exp(-0.3 * 0)
PEER_HEADS = 8
PEER_NKEYS = 128
PEER_TOPK = 16
RW_CHUNK = 64
NEG_INF = float("-inf")


def _params(*sem):
    return pltpu.CompilerParams(dimension_semantics=sem, vmem_limit_bytes=VMEM_LIMIT_BYTES)


def _dot(a, b, precision=None):
    return jnp.dot(a, b, preferred_element_type=F32, precision=precision)


def _dot_nt(a, b, precision=None):
    return lax.dot_general(a, b, (((1,), (1,)), ((), ())), preferred_element_type=F32, precision=precision)


def _dot_tn(a, b, precision=None):
    return lax.dot_general(a, b, (((0,), (0,)), ((), ())), preferred_element_type=F32, precision=precision)


def _group_matrix(n, group, value):
    shift = group.bit_length() - 1
    r = lax.broadcasted_iota(jnp.int32, (n, n), 0) >> shift
    c = lax.broadcasted_iota(jnp.int32, (n, n), 1) >> shift
    return jnp.where(r == c, value, 0.0).astype(F32)


def _ada_kernel(c_ref, w_ref, b_ref, o_ref):
    c = c_ref[...]
    s = c * jax.nn.sigmoid(c)
    o_ref[...] = _dot(s, w_ref[...], HIGHEST) + b_ref[...]


def _ada_mod(c, w, b):
    bsz, d = c.shape
    n = w.shape[1]
    tn = 1024
    return pl.pallas_call(
        _ada_kernel,
        grid=(n // tn,),
        in_specs=[pl.BlockSpec((bsz, d), lambda j: (0, 0)),
                  pl.BlockSpec((d, tn), lambda j: (0, j)),
                  pl.BlockSpec((1, tn), lambda j: (0, j))],
        out_specs=pl.BlockSpec((bsz, tn), lambda j: (0, j)),
        out_shape=jax.ShapeDtypeStruct((bsz, n), F32),
        compiler_params=_params("arbitrary"),
        name="ada_mod",
    )(c, w, b.reshape(1, n))


def _inproj_kernel(x_ref, mod_ref, g_ref, wa_ref, wr_ref, oa_ref, or_ref):
    x = x_ref[...]
    ms = jnp.mean(x * x, axis=-1, keepdims=True)
    y = x * lax.rsqrt(ms + NORM_EPS) * g_ref[...]
    m = mod_ref[0]
    h = (y * (1.0 + m[1:2]) + m[0:1]).astype(BF16)
    oa_ref[...] = _dot(h, wa_ref[...])
    or_ref[...] = _dot(h, wr_ref[...])


def _in_proj(x2, mod3, g, w_att, w_rw, seq, tm=256):
    t, d = x2.shape
    na, nr = w_att.shape[1], w_rw.shape[1]
    per_b = seq // tm
    return pl.pallas_call(
        _inproj_kernel,
        grid=(t // tm,),
        in_specs=[pl.BlockSpec((tm, d), lambda i: (i, 0)),
                  pl.BlockSpec((1, 6, d), lambda i: (i // per_b, 0, 0)),
                  pl.BlockSpec((1, d), lambda i: (0, 0)),
                  pl.BlockSpec((d, na), lambda i: (0, 0)),
                  pl.BlockSpec((d, nr), lambda i: (0, 0))],
        out_specs=[pl.BlockSpec((tm, na), lambda i: (i, 0)),
                   pl.BlockSpec((tm, nr), lambda i: (i, 0))],
        out_shape=[jax.ShapeDtypeStruct((t, na), F32), jax.ShapeDtypeStruct((t, nr), F32)],
        compiler_params=_params("arbitrary"),
        name="in_proj",
    )(x2, mod3, g.reshape(1, d), w_att, w_rw)


def _attn_kernel(tq, q_ref, k_ref, v_ref, qg_ref, kg_ref, lam_ref, slope_ref, sg_ref, o_ref, qs, ks, vs):
    seq = q_ref.shape[1]
    width = 2 * DA_HEAD_DIM
    avg = _group_matrix(width, DA_HEAD_DIM, 1.0 / DA_HEAD_DIM)

    def qk_norm(x, g):
        ms = _dot(x * x, avg)
        return x * lax.rsqrt(ms + NORM_EPS) * g

    qs[...] = (qk_norm(q_ref[0], qg_ref[...]) * (DA_HEAD_DIM ** -0.5)).astype(BF16)
    ks[...] = qk_norm(k_ref[0], kg_ref[...]).astype(BF16)
    vs[...] = v_ref[0].astype(BF16)

    lam = lam_ref[...]
    lam_full = (jnp.exp(jnp.sum(lam[0:1] * lam[1:2], axis=-1, keepdims=True))
                - jnp.exp(jnp.sum(lam[2:3] * lam[3:4], axis=-1, keepdims=True)) + LAMBDA_INIT)
    slope = slope_ref[0][:, 0:1]
    first = lax.broadcasted_iota(jnp.int32, (1, width), 1) < DA_HEAD_DIM
    cols = lax.broadcasted_iota(jnp.int32, (1, seq), 1)
    sg = sg_ref[...] * (1.0 - LAMBDA_INIT)

    def body(i, carry):
        r0 = pl.multiple_of(i * tq, tq)
        qt = qs[pl.ds(r0, tq), :]
        q0 = jnp.where(first, qt, jnp.zeros_like(qt))
        q1 = jnp.where(first, jnp.zeros_like(qt), qt)
        kk = ks[...]
        rows = r0 + lax.broadcasted_iota(jnp.int32, (tq, 1), 0)
        bias = slope * jnp.abs(rows - cols).astype(F32)
        s0 = _dot_nt(q0, kk) - bias
        s1 = _dot_nt(q1, kk) - bias
        p0 = jnp.exp(s0 - jnp.max(s0, axis=-1, keepdims=True))
        p1 = jnp.exp(s1 - jnp.max(s1, axis=-1, keepdims=True))
        w0 = 1.0 / jnp.sum(p0, axis=-1, keepdims=True)
        w1 = lam_full / jnp.sum(p1, axis=-1, keepdims=True)
        a = (p0 * w0 - p1 * w1).astype(BF16)
        o = _dot(a, vs[...])
        ms = jnp.mean(o * o, axis=-1, keepdims=True)
        o_ref[0, pl.ds(r0, tq), :] = o * lax.rsqrt(ms + NORM_EPS) * sg
        return carry

    lax.fori_loop(0, seq // tq, body, 0)


def _diff_attention(p_att3, qn_g, kn_g, lam, subln_g, tq=256):
    bsz, seq, _ = p_att3.shape
    width = 2 * DA_HEAD_DIM
    slopes = jnp.asarray([2.0 ** (-8.0 * (h + 1) / DA_HEADS) for h in range(DA_HEADS)], F32)
    slopes = jnp.broadcast_to(slopes[:, None, None], (DA_HEADS, 1, width))
    blk = lambda off: pl.BlockSpec((1, seq, width), lambda b, h: (b, 0, off + h))
    full = lambda shape: pl.BlockSpec(shape, lambda b, h: (0,) * len(shape))
    return pl.pallas_call(
        functools.partial(_attn_kernel, tq),
        grid=(bsz, DA_HEADS),
        in_specs=[blk(0), blk(DA_HEADS), blk(2 * DA_HEADS),
                  full((1, width)), full((1, width)), full((4, DA_HEAD_DIM)),
                  pl.BlockSpec((1, 1, width), lambda b, h: (h, 0, 0)),
                  full((1, width))],
        out_specs=pl.BlockSpec((1, seq, width), lambda b, h: (b, 0, h)),
        out_shape=jax.ShapeDtypeStruct((bsz, seq, DA_HEADS * width), F32),
        scratch_shapes=[pltpu.VMEM((seq, width), BF16)] * 3,
        compiler_params=_params("arbitrary", "arbitrary"),
        name="diff_attention",
    )(p_att3, p_att3, p_att3,
      jnp.tile(qn_g.reshape(1, DA_HEAD_DIM), (1, 2)), jnp.tile(kn_g.reshape(1, DA_HEAD_DIM), (1, 2)),
      lam, slopes, subln_g.reshape(1, width))


def _rwkv_chunk_kernel(n_chunks, cur_ref, prev_ref, next_ref, mu_ref, w0_ref, w2_ref, a0_ref, a2_ref, g2_ref,
                       kk_ref, ka_ref, rk_ref, hs_ref, ra_ref, y0_ref, g_ref, h_ref, bonus_ref, gate_ref):
    ch = RW_CHUNK
    w = RW_WIDTH
    pw = 2 * RW_HEAD
    chunk = pl.program_id(1)

    pc = cur_ref[0]
    row = lax.broadcasted_iota(jnp.int32, (ch, 1), 0)
    pv = prev_ref[0][SUBLANES - 1:SUBLANES, :] * (chunk > 0).astype(F32)
    nx = next_ref[0][0:1, :] * (chunk < n_chunks - 1).astype(F32)
    prev = jnp.where(row == 0, pv, pltpu.roll(pc, 1, 0))
    nxt = jnp.where(row == ch - 1, nx, pltpu.roll(pc, ch - 1, 0))
    mu = mu_ref[...]
    ps = pc + mu[0:1] * (prev - pc) + mu[1:2] * (nxt - pc)
    r, k, v = ps[:, 0:w], ps[:, w:2 * w], ps[:, 2 * w:3 * w]
    wd = ps[:, 3 * w:3 * w + LORA_COLS]
    ad = ps[:, 3 * w + LORA_COLS:3 * w + 2 * LORA_COLS]
    gd = ps[:, 3 * w + 2 * LORA_COLS:3 * w + 3 * LORA_COLS]

    head_sum = hs_ref[...]

    def per_head_sum(x):
        return _dot(x.astype(BF16), head_sum)

    kkr = k * kk_ref[...]
    kk = kkr / jnp.maximum(jnp.sqrt(per_head_sum(kkr * kkr)), 1e-12)
    bonus_ref[0] = per_head_sum(r * k * rk_ref[...]) * v
    gate_ref[0] = _dot(jax.nn.sigmoid(gd), g2_ref[...])

    ti = lax.broadcasted_iota(jnp.int32, (ch, ch), 0)
    tj = lax.broadcasted_iota(jnp.int32, (ch, ch), 1)
    eye = (ti == tj).astype(F32)
    lane = lax.broadcasted_iota(jnp.int32, (1, pw), 1)
    head_lanes = (lane < RW_HEAD, lane >= RW_HEAD)
    pi = lax.broadcasted_iota(jnp.int32, (pw, pw), 0)
    pj = lax.broadcasted_iota(jnp.int32, (pw, pw), 1)
    same_head = (pi < RW_HEAD) == (pj < RW_HEAD)
    eye_pair = (pi == pj).astype(F32)
    v16 = v.astype(BF16)
    tanh_wd = jnp.tanh(wd)

    probs = []
    pairs = []
    for d in range(2):
        z = w0_ref[d:d + 1] + _dot(tanh_wd, w2_ref[d])
        logdec = -jax.nn.sigmoid(z) * math.exp(-0.5)
        a = jax.nn.sigmoid(a0_ref[d:d + 1] + _dot(ad, a2_ref[d]))
        kd = k * (1.0 + (a - 1.0) * ka_ref[...])
        before = (tj < ti) if d == 0 else (tj > ti)
        upto = before | (tj == ti)
        cum = _dot(upto.astype(F32), logdec, HIGHEST)
        total = cum[ch - 1:ch] if d == 0 else cum[0:1]
        w_in, w_ex, w_inv, w_tot = jnp.exp(cum), jnp.exp(cum - logdec), jnp.exp(-cum), jnp.exp(total)
        a_bar = -kk * w_ex
        b_bar = (kk * a * w_inv).astype(BF16)
        k_bar = (kd * w_inv).astype(BF16)
        r_bar = r * (w_in if d == 0 else w_ex)
        ymask = upto if d == 0 else before
        later, earlier = (ti, tj) if d == 0 else (tj, ti)
        off_masks = []
        for lvl in range(int(math.log2(ch))):
            same_pair = (ti >> (lvl + 1)) == (tj >> (lvl + 1))
            off_masks.append(same_pair & (((later >> lvl) & 1) == 1) & (((earlier >> lvl) & 1) == 0))
        for p in range(RW_PAIRS):
            sl = slice(p * pw, (p + 1) * pw)
            pairs.append(dict(d=d, p=p, b=b_bar[:, sl], k=k_bar[:, sl], v=v16[:, sl], r=r_bar[:, sl],
                              w_tot=w_tot[:, sl]))
            for hh in range(2):
                mh = head_lanes[hh]
                probs.append(dict(pair=len(pairs) - 1, before=before, ymask=ymask, off=off_masks,
                                  a=jnp.where(mh, a_bar[:, sl], 0.0).astype(BF16),
                                  r=jnp.where(mh, r_bar[:, sl], 0.0).astype(BF16),
                                  v=jnp.where(mh, v16[:, sl], jnp.zeros_like(v16[:, sl]))))

    for q in probs:
        pr = pairs[q["pair"]]
        q["ab"] = _dot_nt(q["a"], pr["b"])
        ak = _dot_nt(q["a"], pr["k"])
        q["rb"] = jnp.where(q["ymask"], _dot_nt(q["r"], pr["b"]), 0.0).astype(BF16)
        rk = _dot_nt(q["r"], pr["k"])
        q["ak"] = jnp.where(q["before"], ak, 0.0).astype(BF16)
        q["rk"] = jnp.where(q["ymask"], rk, 0.0).astype(BF16)
    for q in probs:
        q["akv"] = _dot(q["ak"], q["v"])
        q["rkv"] = _dot(q["rk"], q["v"])
        q["inv"] = eye + jnp.where(q["off"][0], q["ab"], 0.0)
    for lvl in range(1, int(math.log2(ch))):
        for q in probs:
            q["tmp"] = _dot(q["inv"].astype(BF16), jnp.where(q["off"][lvl], q["ab"], 0.0).astype(BF16))
        for q in probs:
            q["inv"] = q["inv"] + _dot(q["tmp"].astype(BF16), q["inv"].astype(BF16))
    for q in probs:
        rhs = jnp.concatenate([q["a"], q["akv"].astype(BF16)], axis=1)
        q["x"] = _dot(q["inv"].astype(BF16), rhs)
    for q in probs:
        q["z"] = _dot(q["rb"], q["x"].astype(BF16))
    for i, pr in enumerate(pairs):
        q0, q1 = probs[2 * i], probs[2 * i + 1]
        x = q0["x"] + q1["x"]
        z = q0["z"] + q1["z"]
        ra = pr["r"] + z[:, :pw]
        y0 = z[:, pw:] + q0["rkv"] + q1["rkv"]
        xb = _dot_tn(x.astype(BF16), pr["b"])
        vk = _dot_tn(pr["v"], pr["k"])
        g = (eye_pair + jnp.where(same_head, xb[:pw], 0.0)) * pr["w_tot"]
        h = jnp.where(same_head, xb[pw:] + vk, 0.0) * pr["w_tot"]
        d, p = pr["d"], pr["p"]
        sl = slice(p * pw, (p + 1) * pw)
        ra_ref[0, d, :, sl] = ra.astype(BF16)
        y0_ref[0, d, :, sl] = y0
        g_ref[0, d, 0, :, sl] = g.astype(BF16)
        h_ref[0, d, 0, :, sl] = h


def _pad_lora(w2):
    keep = jnp.arange(2)[:, None, None, None] == jnp.arange(2)[None, :, None, None]
    return jnp.where(keep, w2[None], 0.0).reshape(2, -1, w2.shape[-1])


def _rwkv_chunks(p_rw3, mu, w0, w2, a0, a2, g2, k_k, k_a, r_k):
    bsz, seq, cols = p_rw3.shape
    ch, w, pw = RW_CHUNK, RW_WIDTH, 2 * RW_HEAD
    n_chunks = seq // ch
    sub_per_chunk = ch // SUBLANES
    n_sub = seq // SUBLANES
    full = lambda shape: pl.BlockSpec(shape, lambda b, c: (0,) * len(shape))
    row = lambda x: x.reshape(1, w)
    head_sum = (jnp.arange(w)[:, None] // RW_HEAD == jnp.arange(w)[None, :] // RW_HEAD).astype(BF16)
    tok = lambda dt: jax.ShapeDtypeStruct((bsz, 2, seq, w), dt)
    mat = lambda dt: jax.ShapeDtypeStruct((bsz, 2, n_chunks, pw, w), dt)
    tok_spec = pl.BlockSpec((1, 2, ch, w), lambda b, c: (b, 0, c, 0))
    mat_spec = pl.BlockSpec((1, 2, 1, pw, w), lambda b, c: (b, 0, c, 0, 0))
    one_spec = pl.BlockSpec((1, ch, w), lambda b, c: (b, c, 0))
    return pl.pallas_call(
        functools.partial(_rwkv_chunk_kernel, n_chunks),
        grid=(bsz, n_chunks),
        in_specs=[pl.BlockSpec((1, ch, cols), lambda b, c: (b, c, 0)),
                  pl.BlockSpec((1, SUBLANES, cols), lambda b, c: (b, jnp.maximum(c * sub_per_chunk - 1, 0), 0)),
                  pl.BlockSpec((1, SUBLANES, cols),
                               lambda b, c: (b, jnp.minimum((c + 1) * sub_per_chunk, n_sub - 1), 0)),
                  full((2, cols)), full((2, w)), full((2, LORA_COLS, w)), full((2, w)), full((2, LORA_COLS, w)),
                  full((LORA_COLS, w)), full((1, w)), full((1, w)), full((1, w)), full((w, w))],
        out_specs=[tok_spec, tok_spec, mat_spec, mat_spec, one_spec, one_spec],
        out_shape=[tok(BF16), tok(F32), mat(BF16), mat(F32),
                   jax.ShapeDtypeStruct((bsz, seq, w), F32), jax.ShapeDtypeStruct((bsz, seq, w), F32)],
        compiler_params=_params("parallel", "parallel"),
        name="rwkv7_chunks",
    )(p_rw3, p_rw3, p_rw3, mu, w0, _pad_lora(w2), a0, _pad_lora(a2), g2, row(k_k), row(k_a), row(r_k), head_sum)


def _rwkv_scan_kernel(ra0_ref, ra1_ref, y00_ref, y01_ref, g0_ref, g1_ref, h0_ref, h1_ref, o0_ref, o1_ref, st_ref):
    pw = 2 * RW_HEAD

    @pl.when(pl.program_id(1) == 0)
    def _():
        st_ref[...] = jnp.zeros_like(st_ref)

    dirs = ((ra0_ref, y00_ref, g0_ref, h0_ref, o0_ref), (ra1_ref, y01_ref, g1_ref, h1_ref, o1_ref))
    for d, (ra_ref, y0_ref, g_ref, h_ref, o_ref) in enumerate(dirs):
        for p in range(RW_PAIRS):
            sl = slice(p * pw, (p + 1) * pw)
            s = st_ref[d, p]
            s16 = s.astype(BF16)
            o_ref[0, :, sl] = _dot_nt(ra_ref[0, 0, :, sl], s16) + y0_ref[0, 0, :, sl]
            st_ref[d, p] = _dot(s16, g_ref[0, 0, 0, :, sl]) + h_ref[0, 0, 0, :, sl]


def _rwkv_scan(ra, y0, g, h):
    bsz, _, seq, w = ra.shape
    ch, pw = RW_CHUNK, 2 * RW_HEAD
    n_chunks = seq // ch
    tok = lambda d: pl.BlockSpec((1, 1, ch, w), lambda b, c: (b, d, c if d == 0 else n_chunks - 1 - c, 0))
    mat = lambda d: pl.BlockSpec((1, 1, 1, pw, w), lambda b, c: (b, d, c if d == 0 else n_chunks - 1 - c, 0, 0))
    out = lambda d: pl.BlockSpec((1, ch, w), lambda b, c: (b, c if d == 0 else n_chunks - 1 - c, 0))
    return pl.pallas_call(
        _rwkv_scan_kernel,
        grid=(bsz, n_chunks),
        in_specs=[tok(0), tok(1), tok(0), tok(1), mat(0), mat(1), mat(0), mat(1)],
        out_specs=[out(0), out(1)],
        out_shape=[jax.ShapeDtypeStruct((bsz, seq, w), F32)] * 2,
        scratch_shapes=[pltpu.VMEM((2, RW_PAIRS, pw, pw), F32)],
        compiler_params=_params("parallel", "arbitrary"),
        name="rwkv7_scan",
    )(ra, ra, y0, y0, g, g, h, h)


def _outproj_kernel(att_ref, yf_ref, yb_ref, bonus_ref, gate_ref, lng_ref, lnb_ref, hs_ref, x_ref, mod_ref,
                    wa_ref, wr_ref, g_ref, x1_ref, h2_ref):
    head_sum = hs_ref[...]

    def per_head_mean(v):
        return _dot(v.astype(BF16), head_sum) * (1.0 / RW_HEAD)

    y = yf_ref[...] + yb_ref[...]
    yc = y - per_head_mean(y)
    yn = yc * lax.rsqrt(per_head_mean(yc * yc) + GN_EPS) * lng_ref[...] + lnb_ref[...]
    rw = (yn + bonus_ref[...]) * gate_ref[...]
    acc = _dot(att_ref[...].astype(BF16), wa_ref[...]) + _dot(rw.astype(BF16), wr_ref[...])
    m = mod_ref[0]
    x1 = x_ref[...] + m[2:3] * acc
    x1_ref[...] = x1
    ms = jnp.mean(x1 * x1, axis=-1, keepdims=True)
    yo = x1 * lax.rsqrt(ms + NORM_EPS) * g_ref[...]
    h2_ref[...] = (yo * (1.0 + m[4:5]) + m[3:4]).astype(BF16)


def _out_proj(att2, y_f, y_b, bonus, gate, ln_g, ln_b, x2, mod3, w_a, w_r, g, seq, tm=256):
    t, d = x2.shape
    ka, kr = att2.shape[1], y_f.shape[1]
    per_b = seq // tm
    rows = lambda n: pl.BlockSpec((tm, n), lambda i: (i, 0))
    full = lambda shape: pl.BlockSpec(shape, lambda i: (0,) * len(shape))
    head_sum = (jnp.arange(kr)[:, None] // RW_HEAD == jnp.arange(kr)[None, :] // RW_HEAD).astype(BF16)
    return pl.pallas_call(
        _outproj_kernel,
        grid=(t // tm,),
        in_specs=[rows(ka), rows(kr), rows(kr), rows(kr), rows(kr), full((1, kr)), full((1, kr)), full((kr, kr)),
                  rows(d), pl.BlockSpec((1, 6, d), lambda i: (i // per_b, 0, 0)),
                  full((ka, d)), full((kr, d)), full((1, d))],
        out_specs=[rows(d), rows(d)],
        out_shape=[jax.ShapeDtypeStruct((t, d), F32), jax.ShapeDtypeStruct((t, d), BF16)],
        compiler_params=_params("arbitrary"),
        name="out_proj",
    )(att2, y_f, y_b, bonus, gate, ln_g.reshape(1, kr), ln_b.reshape(1, kr), head_sum, x2, mod3, w_a, w_r,
      g.reshape(1, d))


def _topk_ranks(scores, vals_refs):
    nk, tm = scores[0].shape
    kiota = lax.broadcasted_iota(jnp.int32, (nk, tm), 0)

    def body(j, carry):
        out = []
        for (cur, rank), vals_ref in zip(carry, vals_refs):
            m = jnp.max(cur, axis=0, keepdims=True)
            pick = jnp.min(jnp.where(cur == m, kiota, nk), axis=0, keepdims=True)
            hit = kiota == pick
            vals_ref[pl.ds(j, 1), :] = m
            out.append((jnp.where(hit, NEG_INF, cur), jnp.where(hit, j, rank)))
        return tuple(out)

    init = tuple((s, jnp.full((nk, tm), PEER_TOPK, jnp.int32)) for s in scores)
    return [rank for _, rank in lax.fori_loop(0, PEER_TOPK, body, init)]


def _topk_ranks_tiled(s1, s2, v1_ref, v2_ref):
    n = s1.shape[1] // LANES
    lanes = lambda s, g: s[:, g * LANES:(g + 1) * LANES]
    ranks = _topk_ranks([lanes(s, g) for s in (s1, s2) for g in range(n)],
                        [ref.at[g] for ref in (v1_ref, v2_ref) for g in range(n)])
    cat = lambda parts: jnp.concatenate(parts, axis=1)
    return (cat(ranks[:n]), cat(ranks[n:]), cat([v1_ref[g] for g in range(n)]), cat([v2_ref[g] for g in range(n)]))


def _pair_candidates():
    pieces = []
    for lvl in range(4):
        cnt = PEER_TOPK // (lvl + 1)
        pieces.append(("row", lvl, lvl, cnt))
        if lvl + 1 < cnt:
            pieces.append(("col", lvl, lvl + 1, cnt))
    return pieces


def _route_kernel(h_ref, wq_ref, keys_ref, p1_ref, cnt_ref, p2_ref, rk2_ref, q_scr, v1_ref, v2_ref, sc_ref, ci_ref):
    tm = h_ref.shape[0]
    nk = PEER_NKEYS
    q = _dot(h_ref[...], wq_ref[...]).astype(BF16)
    for hp in range(2 * PEER_HEADS):
        q_scr[hp] = q[:, hp * nk:(hp + 1) * nk]

    pieces = _pair_candidates()
    big = 4 * PEER_TOPK * PEER_TOPK

    def head(h, carry):
        s1 = _dot_nt(keys_ref[2 * h], q_scr[2 * h])
        s2 = _dot_nt(keys_ref[2 * h + 1], q_scr[2 * h + 1])
        rank1, rank2, v1, v2 = _topk_ranks_tiled(s1, s2, v1_ref, v2_ref)
        cands, cis = [], []
        for kind, fixed, lo, hi in pieces:
            rows = SUBLANES * ((hi + SUBLANES - 1) // SUBLANES)
            ridx = lax.broadcasted_iota(jnp.int32, (rows, 1), 0)
            valid = (ridx >= lo) & (ridx < hi)
            if kind == "row":
                vsum = v1[fixed:fixed + 1] + v2[0:rows]
                ci = fixed * PEER_TOPK + ridx
            else:
                vsum = v1[0:rows] + v2[fixed:fixed + 1]
                ci = ridx * PEER_TOPK + fixed
            cands.append(jnp.where(valid, vsum, NEG_INF))
            cis.append(jnp.broadcast_to(jnp.where(valid, ci, big), (rows, tm)))
        cand = jnp.concatenate(cands, axis=0)
        ci = jnp.concatenate(cis, axis=0)

        def pick(j, cur):
            m = jnp.max(cur, axis=0, keepdims=True)
            sel = jnp.min(jnp.where(cur == m, ci, big), axis=0, keepdims=True)
            sc_ref[pl.ds(j, 1), :] = m
            ci_ref[pl.ds(j, 1), :] = sel
            return jnp.where(ci == sel, NEG_INF, cur)

        lax.fori_loop(0, PEER_TOPK, pick, cand)
        sc = sc_ref[...]
        arank = ci_ref[...] >> (PEER_TOPK.bit_length() - 1)
        zsum = jnp.sum(jnp.exp(sc - sc[0:1]), axis=0, keepdims=True)
        cnt = jnp.zeros((nk, tm), F32)
        for a in range(PEER_TOPK):
            c_a = jnp.sum((arank == a).astype(F32), axis=0, keepdims=True)
            cnt = jnp.where(rank1 == a, c_a, cnt)
        p1_ref[h] = jnp.where(rank1 < PEER_TOPK, jnp.exp(s1 - v1[0:1]) / zsum, 0.0)
        cnt_ref[h] = cnt
        p2_ref[h] = jnp.where(rank2 < PEER_TOPK, jnp.exp(s2 - v2[0:1]), 0.0).astype(BF16)
        rk2_ref[h] = rank2.astype(F32).astype(BF16)
        return carry

    lax.fori_loop(0, PEER_HEADS, head, 0)


def _route(h2, wq, keys, tm=256):
    t, d = h2.shape
    nq = wq.shape[1]
    nk = PEER_NKEYS
    out = lambda dt: jax.ShapeDtypeStruct((PEER_HEADS, nk, t), dt)
    ospec = pl.BlockSpec((PEER_HEADS, nk, tm), lambda i: (0, 0, i))
    return pl.pallas_call(
        _route_kernel,
        grid=(t // tm,),
        in_specs=[pl.BlockSpec((tm, d), lambda i: (i, 0)),
                  pl.BlockSpec((d, nq), lambda i: (0, 0)),
                  pl.BlockSpec((2 * PEER_HEADS, nk, nk), lambda i: (0, 0, 0))],
        out_specs=[ospec] * 4,
        out_shape=[out(F32), out(F32), out(BF16), out(BF16)],
        scratch_shapes=[pltpu.VMEM((2 * PEER_HEADS, tm, nk), BF16),
                        pltpu.VMEM((tm // LANES, PEER_TOPK, LANES), F32),
                        pltpu.VMEM((tm // LANES, PEER_TOPK, LANES), F32),
                        pltpu.VMEM((PEER_TOPK, tm), F32), pltpu.VMEM((PEER_TOPK, tm), jnp.int32)],
        compiler_params=_params("parallel"),
        name="peer_route",
    )(h2, wq, keys)


def _expert_kernel(n_eblk, h_ref, u_ref, vt_ref, p1_ref, cnt_ref, p2_ref, rk2_ref, x1_ref, mod_ref,
                   o_ref, acc_ref):
    j = pl.program_id(1)
    te = u_ref.shape[0]
    nk = PEER_NKEYS
    kb_rows = MXU_WIDTH
    per_kb = kb_rows // nk

    @pl.when(j == 0)
    def _():
        acc_ref[...] = jnp.zeros_like(acc_ref)

    hb = h_ref[...]
    acc = None
    for kb in range(te // kb_rows):
        act = _dot_nt(u_ref[kb * kb_rows:(kb + 1) * kb_rows, :], hb)
        ws = []
        for sb in range(per_kb):
            e1 = j * (te // nk) + kb * per_kb + sb
            gate = None
            for h in range(PEER_HEADS):
                c = cnt_ref[h, pl.ds(e1, 1), :].astype(BF16)
                p = p1_ref[h, pl.ds(e1, 1), :].astype(BF16)
                term = jnp.where(rk2_ref[h] < c, p2_ref[h], jnp.zeros_like(p2_ref[h])) * p
                gate = term if gate is None else gate + term
            a = act[sb * nk:(sb + 1) * nk]
            gelu = 0.5 * a * (1.0 + lax.erf(a * (2.0 ** -0.5)))
            ws.append(gate * gelu.astype(BF16))
        part = _dot(vt_ref[:, kb * kb_rows:(kb + 1) * kb_rows], jnp.concatenate(ws, axis=0))
        acc = part if acc is None else acc + part
    acc_ref[...] += acc

    @pl.when(j == n_eblk - 1)
    def _():
        o_ref[...] = x1_ref[...] + mod_ref[0][5:6] * acc_ref[...].T


def _experts(h2, u, vt, p1, cnt, p2, rk2, x1, mod3, seq, tm=512, te=1024):
    t, d = h2.shape
    ne = u.shape[0]
    nk = PEER_NKEYS
    n_eblk = ne // te
    per_b = seq // tm
    rspec = pl.BlockSpec((PEER_HEADS, nk, tm), lambda i, j: (0, 0, i))
    return pl.pallas_call(
        functools.partial(_expert_kernel, n_eblk),
        grid=(t // tm, n_eblk),
        in_specs=[pl.BlockSpec((tm, d), lambda i, j: (i, 0)),
                  pl.BlockSpec((te, d), lambda i, j: (j, 0)),
                  pl.BlockSpec((d, te), lambda i, j: (0, j)),
                  rspec, rspec, rspec, rspec,
                  pl.BlockSpec((tm, d), lambda i, j: (i, 0)),
                  pl.BlockSpec((1, 6, d), lambda i, j: (i // per_b, 0, 0))],
        out_specs=pl.BlockSpec((tm, d), lambda i, j: (i, 0)),
        out_shape=jax.ShapeDtypeStruct((t, d), F32),
        scratch_shapes=[pltpu.VMEM((d, tm), F32)],
        compiler_params=_params("parallel", "arbitrary"),
        name="peer_experts",
    )(h2, u, vt, p1, cnt, p2, rk2, x1, mod3)


def kernel(x, c, ada_w, ada_b, norm1_g, w_in, da_qnorm_g, da_knorm_g, da_lambda, da_subln_g, rw_shift_mu, rw_w0,
           rw_w2, rw_a0, rw_a2, rw_g2, rw_k_k, rw_k_a, rw_r_k, rw_ln_g, rw_ln_b, w_out, norm2_g, peer_wq,
           peer_keys, peer_u, peer_v):
    bsz, seq, d = x.shape
    depth = ada_w.shape[0]
    t = bsz * seq
    for l in range(depth):
        mod3 = _ada_mod(c, ada_w[l], ada_b[l]).reshape(bsz, 6, d)
        x2 = x.reshape(t, d)
        w_l = w_in[l].astype(BF16)
        p_att, p_rw = _in_proj(x2, mod3, norm1_g[l], w_l[:, :ATT_COLS], w_l[:, ATT_COLS:], seq)
        att = _diff_attention(p_att.reshape(bsz, seq, ATT_COLS), da_qnorm_g[l], da_knorm_g[l], da_lambda[l],
                              da_subln_g[l])
        ra, y0, g, h, bonus, gate = _rwkv_chunks(p_rw.reshape(bsz, seq, RWKV_COLS), rw_shift_mu[l], rw_w0[l],
                                                 rw_w2[l], rw_a0[l], rw_a2[l], rw_g2[l], rw_k_k[l], rw_k_a[l],
                                                 rw_r_k[l].reshape(-1))
        y_f, y_b = _rwkv_scan(ra, y0, g, h)
        wo = w_out[l].astype(BF16)
        aw = att.shape[-1]
        flat = lambda a: a.reshape(t, RW_WIDTH)
        x1, h2 = _out_proj(att.reshape(t, aw), flat(y_f), flat(y_b), flat(bonus), flat(gate), rw_ln_g[l],
                           rw_ln_b[l], x2, mod3, wo[:aw], wo[aw:], norm2_g[l], seq)
        keys = peer_keys[l].reshape(2 * PEER_HEADS, PEER_NKEYS, -1).astype(BF16)
        p1, cnt, p2, rk2 = _route(h2, peer_wq[l].astype(BF16), keys)
        x = _experts(h2, peer_u[l].astype(BF16), peer_v[l].T.astype(BF16), p1, cnt, p2, rk2, x1, mod3,
                     seq).reshape(bsz, seq, d)
    return x
```

```python
import functools
import math

import jax
import jax.numpy as jnp
from jax import lax
from jax.experimental import pallas as pl
from jax.experimental.pallas import tpu as pltpu

F32 = jnp.float32
BF16 = jnp.bfloat16
HIGHEST = lax.Precision.HIGHEST

LANES = 128
SUBLANES = 8
MXU_WIDTH = 256
VMEM_LIMIT_BYTES = 56 * 1024 * 1024

DA_HEAD_DIM = 64
DA_HEADS = 4
RW_HEAD = 64
RW_HEADS = 8
RW_PAIRS = RW_HEADS // 2
RW_WIDTH = RW_HEADS * RW_HEAD
LORA_COLS = 128
RWKV_COLS = 3 * RW_WIDTH + 3 * LORA_COLS
ATT_COLS = 3 * 2 * DA_HEAD_DIM * DA_HEADS
GN_EPS = 64e-5
NORM_EPS = 1e-6
LAMBDA_INIT = 0.8 - 0.6 * math.exp(-0.3 * 0)
PEER_HEADS = 8
PEER_NKEYS = 128
PEER_TOPK = 16
RW_CHUNK = 64
NEG_INF = float("-inf")


def _params(*sem):
    return pltpu.CompilerParams(dimension_semantics=sem, vmem_limit_bytes=VMEM_LIMIT_BYTES)


def _dot(a, b, precision=None):
    return jnp.dot(a, b, preferred_element_type=F32, precision=precision)


def _dot_nt(a, b, precision=None):
    return lax.dot_general(a, b, (((1,), (1,)), ((), ())), preferred_element_type=F32, precision=precision)


def _dot_tn(a, b, precision=None):
    return lax.dot_general(a, b, (((0,), (0,)), ((), ())), preferred_element_type=F32, precision=precision)


def _group_matrix(n, group, value):
    shift = group.bit_length() - 1
    r = lax.broadcasted_iota(jnp.int32, (n, n), 0) >> shift
    c = lax.broadcasted_iota(jnp.int32, (n, n), 1) >> shift
    return jnp.where(r == c, value, 0.0).astype(F32)


def _ada_kernel(c_ref, w_ref, b_ref, o_ref):
    c = c_ref[...]
    s = c * jax.nn.sigmoid(c)
    o_ref[...] = _dot(s, w_ref[...], HIGHEST) + b_ref[...]


def _ada_mod(c, w, b):
    bsz, d = c.shape
    n = w.shape[1]
    tn = 1024
    return pl.pallas_call(
        _ada_kernel,
        grid=(n // tn,),
        in_specs=[pl.BlockSpec((bsz, d), lambda j: (0, 0)),
                  pl.BlockSpec((d, tn), lambda j: (0, j)),
                  pl.BlockSpec((1, tn), lambda j: (0, j))],
        out_specs=pl.BlockSpec((bsz, tn), lambda j: (0, j)),
        out_shape=jax.ShapeDtypeStruct((bsz, n), F32),
        compiler_params=_params("arbitrary"),
        name="ada_mod",
    )(c, w, b.reshape(1, n))


def _inproj_kernel(x_ref, mod_ref, g_ref, wa_ref, wr_ref, oa_ref, or_ref):
    x = x_ref[...]
    ms = jnp.mean(x * x, axis=-1, keepdims=True)
    y = x * lax.rsqrt(ms + NORM_EPS) * g_ref[...]
    m = mod_ref[0]
    h = (y * (1.0 + m[1:2]) + m[0:1]).astype(BF16)
    oa_ref[...] = _dot(h, wa_ref[...])
    or_ref[...] = _dot(h, wr_ref[...])


def _in_proj(x2, mod3, g, w_att, w_rw, seq, tm=256):
    t, d = x2.shape
    na, nr = w_att.shape[1], w_rw.shape[1]
    per_b = seq // tm
    return pl.pallas_call(
        _inproj_kernel,
        grid=(t // tm,),
        in_specs=[pl.BlockSpec((tm, d), lambda i: (i, 0)),
                  pl.BlockSpec((1, 6, d), lambda i: (i // per_b, 0, 0)),
                  pl.BlockSpec((1, d), lambda i: (0, 0)),
                  pl.BlockSpec((d, na), lambda i: (0, 0)),
                  pl.BlockSpec((d, nr), lambda i: (0, 0))],
        out_specs=[pl.BlockSpec((tm, na), lambda i: (i, 0)),
                   pl.BlockSpec((tm, nr), lambda i: (i, 0))],
        out_shape=[jax.ShapeDtypeStruct((t, na), F32), jax.ShapeDtypeStruct((t, nr), F32)],
        compiler_params=_params("arbitrary"),
        name="in_proj",
    )(x2, mod3, g.reshape(1, d), w_att, w_rw)


def _attn_kernel(tq, q_ref, k_ref, v_ref, qg_ref, kg_ref, lam_ref, slope_ref, sg_ref, o_ref, qs, ks, vs):
    seq = q_ref.shape[1]
    width = 2 * DA_HEAD_DIM
    avg = _group_matrix(width, DA_HEAD_DIM, 1.0 / DA_HEAD_DIM)

    def qk_norm(x, g):
        ms = _dot(x * x, avg)
        return x * lax.rsqrt(ms + NORM_EPS) * g

    qs[...] = (qk_norm(q_ref[0], qg_ref[...]) * (DA_HEAD_DIM ** -0.5)).astype(BF16)
    ks[...] = qk_norm(k_ref[0], kg_ref[...]).astype(BF16)
    vs[...] = v_ref[0].astype(BF16)

    lam = lam_ref[...]
    lam_full = (jnp.exp(jnp.sum(lam[0:1] * lam[1:2], axis=-1, keepdims=True))
                - jnp.exp(jnp.sum(lam[2:3] * lam[3:4], axis=-1, keepdims=True)) + LAMBDA_INIT)
    slope = slope_ref[0][:, 0:1]
    first = lax.broadcasted_iota(jnp.int32, (1, width), 1) < DA_HEAD_DIM
    cols = lax.broadcasted_iota(jnp.int32, (1, seq), 1)
    sg = sg_ref[...] * (1.0 - LAMBDA_INIT)

    def body(i, carry):
        r0 = pl.multiple_of(i * tq, tq)
        qt = qs[pl.ds(r0, tq), :]
        q0 = jnp.where(first, qt, jnp.zeros_like(qt))
        q1 = jnp.where(first, jnp.zeros_like(qt), qt)
        kk = ks[...]
        rows = r0 + lax.broadcasted_iota(jnp.int32, (tq, 1), 0)
        bias = slope * jnp.abs(rows - cols).astype(F32)
        s0 = _dot_nt(q0, kk) - bias
        s1 = _dot_nt(q1, kk) - bias
        p0 = jnp.exp(s0 - jnp.max(s0, axis=-1, keepdims=True))
        p1 = jnp.exp(s1 - jnp.max(s1, axis=-1, keepdims=True))
        w0 = 1.0 / jnp.sum(p0, axis=-1, keepdims=True)
        w1 = lam_full / jnp.sum(p1, axis=-1, keepdims=True)
        a = (p0 * w0 - p1 * w1).astype(BF16)
        o = _dot(a, vs[...])
        ms = jnp.mean(o * o, axis=-1, keepdims=True)
        o_ref[0, pl.ds(r0, tq), :] = o * lax.rsqrt(ms + NORM_EPS) * sg
        return carry

    lax.fori_loop(0, seq // tq, body, 0)


def _diff_attention(p_att3, qn_g, kn_g, lam, subln_g, tq=256):
    bsz, seq, _ = p_att3.shape
    width = 2 * DA_HEAD_DIM
    slopes = jnp.asarray([2.0 ** (-8.0 * (h + 1) / DA_HEADS) for h in range(DA_HEADS)], F32)
    slopes = jnp.broadcast_to(slopes[:, None, None], (DA_HEADS, 1, width))
    blk = lambda off: pl.BlockSpec((1, seq, width), lambda b, h: (b, 0, off + h))
    full = lambda shape: pl.BlockSpec(shape, lambda b, h: (0,) * len(shape))
    return pl.pallas_call(
        functools.partial(_attn_kernel, tq),
        grid=(bsz, DA_HEADS),
        in_specs=[blk(0), blk(DA_HEADS), blk(2 * DA_HEADS),
                  full((1, width)), full((1, width)), full((4, DA_HEAD_DIM)),
                  pl.BlockSpec((1, 1, width), lambda b, h: (h, 0, 0)),
                  full((1, width))],
        out_specs=pl.BlockSpec((1, seq, width), lambda b, h: (b, 0, h)),
        out_shape=jax.ShapeDtypeStruct((bsz, seq, DA_HEADS * width), F32),
        scratch_shapes=[pltpu.VMEM((seq, width), BF16)] * 3,
        compiler_params=_params("arbitrary", "arbitrary"),
        name="diff_attention",
    )(p_att3, p_att3, p_att3,
      jnp.tile(qn_g.reshape(1, DA_HEAD_DIM), (1, 2)), jnp.tile(kn_g.reshape(1, DA_HEAD_DIM), (1, 2)),
      lam, slopes, subln_g.reshape(1, width))


def _rwkv_chunk_kernel(n_chunks, cur_ref, prev_ref, next_ref, mu_ref, w0_ref, w2_ref, a0_ref, a2_ref, g2_ref,
                       kk_ref, ka_ref, rk_ref, hs_ref, ra_ref, y0_ref, g_ref, h_ref, bonus_ref, gate_ref):
    ch = RW_CHUNK
    w = RW_WIDTH
    pw = 2 * RW_HEAD
    chunk = pl.program_id(1)

    pc = cur_ref[0]
    row = lax.broadcasted_iota(jnp.int32, (ch, 1), 0)
    pv = prev_ref[0][SUBLANES - 1:SUBLANES, :] * (chunk > 0).astype(F32)
    nx = next_ref[0][0:1, :] * (chunk < n_chunks - 1).astype(F32)
    prev = jnp.where(row == 0, pv, pltpu.roll(pc, 1, 0))
    nxt = jnp.where(row == ch - 1, nx, pltpu.roll(pc, ch - 1, 0))
    mu = mu_ref[...]
    ps = pc + mu[0:1] * (prev - pc) + mu[1:2] * (nxt - pc)
    r, k, v = ps[:, 0:w], ps[:, w:2 * w], ps[:, 2 * w:3 * w]
    wd = ps[:, 3 * w:3 * w + LORA_COLS]
    ad = ps[:, 3 * w + LORA_COLS:3 * w + 2 * LORA_COLS]
    gd = ps[:, 3 * w + 2 * LORA_COLS:3 * w + 3 * LORA_COLS]

    head_sum = hs_ref[...]

    def per_head_sum(x):
        return _dot(x.astype(BF16), head_sum)

    kkr = k * kk_ref[...]
    kk = kkr / jnp.maximum(jnp.sqrt(per_head_sum(kkr * kkr)), 1e-12)
    bonus_ref[0] = per_head_sum(r * k * rk_ref[...]) * v
    gate_ref[0] = _dot(jax.nn.sigmoid(gd), g2_ref[...])

    ti = lax.broadcasted_iota(jnp.int32, (ch, ch), 0)
    tj = lax.broadcasted_iota(jnp.int32, (ch, ch), 1)
    eye = (ti == tj).astype(F32)
    lane = lax.broadcasted_iota(jnp.int32, (1, pw), 1)
    head_lanes = (lane < RW_HEAD, lane >= RW_HEAD)
    pi = lax.broadcasted_iota(jnp.int32, (pw, pw), 0)
    pj = lax.broadcasted_iota(jnp.int32, (pw, pw), 1)
    same_head = (pi < RW_HEAD) == (pj < RW_HEAD)
    eye_pair = (pi == pj).astype(F32)
    v16 = v.astype(BF16)
    tanh_wd = jnp.tanh(wd)

    probs = []
    pairs = []
    for d in range(2):
        z = w0_ref[d:d + 1] + _dot(tanh_wd, w2_ref[d])
        logdec = -jax.nn.sigmoid(z) * math.exp(-0.5)
        a = jax.nn.sigmoid(a0_ref[d:d + 1] + _dot(ad, a2_ref[d]))
        kd = k * (1.0 + (a - 1.0) * ka_ref[...])
        before = (tj < ti) if d == 0 else (tj > ti)
        upto = before | (tj == ti)
        cum = _dot(upto.astype(F32), logdec, HIGHEST)
        total = cum[ch - 1:ch] if d == 0 else cum[0:1]
        w_in, w_ex, w_inv, w_tot = jnp.exp(cum), jnp.exp(cum - logdec), jnp.exp(-cum), jnp.exp(total)
        a_bar = -kk * w_ex
        b_bar = (kk * a * w_inv).astype(BF16)
        k_bar = (kd * w_inv).astype(BF16)
        r_bar = r * (w_in if d == 0 else w_ex)
        ymask = upto if d == 0 else before
        later, earlier = (ti, tj) if d == 0 else (tj, ti)
        off_masks = []
        for lvl in range(int(math.log2(ch))):
            same_pair = (ti >> (lvl + 1)) == (tj >> (lvl + 1))
            off_masks.append(same_pair & (((later >> lvl) & 1) == 1) & (((earlier >> lvl) & 1) == 0))
        for p in range(RW_PAIRS):
            sl = slice(p * pw, (p + 1) * pw)
            pairs.append(dict(d=d, p=p, b=b_bar[:, sl], k=k_bar[:, sl], v=v16[:, sl], r=r_bar[:, sl],
                              w_tot=w_tot[:, sl]))
            for hh in range(2):
                mh = head_lanes[hh]
                probs.append(dict(pair=len(pairs) - 1, before=before, ymask=ymask, off=off_masks,
                                  a=jnp.where(mh, a_bar[:, sl], 0.0).astype(BF16),
                                  r=jnp.where(mh, r_bar[:, sl], 0.0).astype(BF16),
                                  v=jnp.where(mh, v16[:, sl], jnp.zeros_like(v16[:, sl]))))

    for q in probs:
        pr = pairs[q["pair"]]
        q["ab"] = _dot_nt(q["a"], pr["b"])
        ak = _dot_nt(q["a"], pr["k"])
        q["rb"] = jnp.where(q["ymask"], _dot_nt(q["r"], pr["b"]), 0.0).astype(BF16)
        rk = _dot_nt(q["r"], pr["k"])
        q["ak"] = jnp.where(q["before"], ak, 0.0).astype(BF16)
        q["rk"] = jnp.where(q["ymask"], rk, 0.0).astype(BF16)
    for q in probs:
        q["akv"] = _dot(q["ak"], q["v"])
        q["rkv"] = _dot(q["rk"], q["v"])
        q["inv"] = eye + jnp.where(q["off"][0], q["ab"], 0.0)
    for lvl in range(1, int(math.log2(ch))):
        for q in probs:
            q["tmp"] = _dot(q["inv"].astype(BF16), jnp.where(q["off"][lvl], q["ab"], 0.0).astype(BF16))
        for q in probs:
            q["inv"] = q["inv"] + _dot(q["tmp"].astype(BF16), q["inv"].astype(BF16))
    for q in probs:
        rhs = jnp.concatenate([q["a"], q["akv"].astype(BF16)], axis=1)
        q["x"] = _dot(q["inv"].astype(BF16), rhs)
    for q in probs:
        q["z"] = _dot(q["rb"], q["x"].astype(BF16))
    for i, pr in enumerate(pairs):
        q0, q1 = probs[2 * i], probs[2 * i + 1]
        x = q0["x"] + q1["x"]
        z = q0["z"] + q1["z"]
        ra = pr["r"] + z[:, :pw]
        y0 = z[:, pw:] + q0["rkv"] + q1["rkv"]
        xb = _dot_tn(x.astype(BF16), pr["b"])
        vk = _dot_tn(pr["v"], pr["k"])
        g = (eye_pair + jnp.where(same_head, xb[:pw], 0.0)) * pr["w_tot"]
        h = jnp.where(same_head, xb[pw:] + vk, 0.0) * pr["w_tot"]
        d, p = pr["d"], pr["p"]
        sl = slice(p * pw, (p + 1) * pw)
        ra_ref[0, d, :, sl] = ra.astype(BF16)
        y0_ref[0, d, :, sl] = y0
        g_ref[0, d, 0, :, sl] = g.astype(BF16)
        h_ref[0, d, 0, :, sl] = h


def _pad_lora(w2):
    keep = jnp.arange(2)[:, None, None, None] == jnp.arange(2)[None, :, None, None]
    return jnp.where(keep, w2[None], 0.0).reshape(2, -1, w2.shape[-1])


def _rwkv_chunks(p_rw3, mu, w0, w2, a0, a2, g2, k_k, k_a, r_k):
    bsz, seq, cols = p_rw3.shape
    ch, w, pw = RW_CHUNK, RW_WIDTH, 2 * RW_HEAD
    n_chunks = seq // ch
    sub_per_chunk = ch // SUBLANES
    n_sub = seq // SUBLANES
    full = lambda shape: pl.BlockSpec(shape, lambda b, c: (0,) * len(shape))
    row = lambda x: x.reshape(1, w)
    head_sum = (jnp.arange(w)[:, None] // RW_HEAD == jnp.arange(w)[None, :] // RW_HEAD).astype(BF16)
    tok = lambda dt: jax.ShapeDtypeStruct((bsz, 2, seq, w), dt)
    mat = lambda dt: jax.ShapeDtypeStruct((bsz, 2, n_chunks, pw, w), dt)
    tok_spec = pl.BlockSpec((1, 2, ch, w), lambda b, c: (b, 0, c, 0))
    mat_spec = pl.BlockSpec((1, 2, 1, pw, w), lambda b, c: (b, 0, c, 0, 0))
    one_spec = pl.BlockSpec((1, ch, w), lambda b, c: (b, c, 0))
    return pl.pallas_call(
        functools.partial(_rwkv_chunk_kernel, n_chunks),
        grid=(bsz, n_chunks),
        in_specs=[pl.BlockSpec((1, ch, cols), lambda b, c: (b, c, 0)),
                  pl.BlockSpec((1, SUBLANES, cols), lambda b, c: (b, jnp.maximum(c * sub_per_chunk - 1, 0), 0)),
                  pl.BlockSpec((1, SUBLANES, cols),
                               lambda b, c: (b, jnp.minimum((c + 1) * sub_per_chunk, n_sub - 1), 0)),
                  full((2, cols)), full((2, w)), full((2, LORA_COLS, w)), full((2, w)), full((2, LORA_COLS, w)),
                  full((LORA_COLS, w)), full((1, w)), full((1, w)), full((1, w)), full((w, w))],
        out_specs=[tok_spec, tok_spec, mat_spec, mat_spec, one_spec, one_spec],
        out_shape=[tok(BF16), tok(F32), mat(BF16), mat(F32),
                   jax.ShapeDtypeStruct((bsz, seq, w), F32), jax.ShapeDtypeStruct((bsz, seq, w), F32)],
        compiler_params=_params("parallel", "parallel"),
        name="rwkv7_chunks",
    )(p_rw3, p_rw3, p_rw3, mu, w0, _pad_lora(w2), a0, _pad_lora(a2), g2, row(k_k), row(k_a), row(r_k), head_sum)


def _rwkv_scan_kernel(ra0_ref, ra1_ref, y00_ref, y01_ref, g0_ref, g1_ref, h0_ref, h1_ref, o0_ref, o1_ref, st_ref):
    pw = 2 * RW_HEAD

    @pl.when(pl.program_id(1) == 0)
    def _():
        st_ref[...] = jnp.zeros_like(st_ref)

    dirs = ((ra0_ref, y00_ref, g0_ref, h0_ref, o0_ref), (ra1_ref, y01_ref, g1_ref, h1_ref, o1_ref))
    for d, (ra_ref, y0_ref, g_ref, h_ref, o_ref) in enumerate(dirs):
        for p in range(RW_PAIRS):
            sl = slice(p * pw, (p + 1) * pw)
            s = st_ref[d, p]
            s16 = s.astype(BF16)
            o_ref[0, :, sl] = _dot_nt(ra_ref[0, 0, :, sl], s16) + y0_ref[0, 0, :, sl]
            st_ref[d, p] = _dot(s16, g_ref[0, 0, 0, :, sl]) + h_ref[0, 0, 0, :, sl]


def _rwkv_scan(ra, y0, g, h):
    bsz, _, seq, w = ra.shape
    ch, pw = RW_CHUNK, 2 * RW_HEAD
    n_chunks = seq // ch
    tok = lambda d: pl.BlockSpec((1, 1, ch, w), lambda b, c: (b, d, c if d == 0 else n_chunks - 1 - c, 0))
    mat = lambda d: pl.BlockSpec((1, 1, 1, pw, w), lambda b, c: (b, d, c if d == 0 else n_chunks - 1 - c, 0, 0))
    out = lambda d: pl.BlockSpec((1, ch, w), lambda b, c: (b, c if d == 0 else n_chunks - 1 - c, 0))
    return pl.pallas_call(
        _rwkv_scan_kernel,
        grid=(bsz, n_chunks),
        in_specs=[tok(0), tok(1), tok(0), tok(1), mat(0), mat(1), mat(0), mat(1)],
        out_specs=[out(0), out(1)],
        out_shape=[jax.ShapeDtypeStruct((bsz, seq, w), F32)] * 2,
        scratch_shapes=[pltpu.VMEM((2, RW_PAIRS, pw, pw), F32)],
        compiler_params=_params("parallel", "arbitrary"),
        name="rwkv7_scan",
    )(ra, ra, y0, y0, g, g, h, h)


def _outproj_kernel(att_ref, yf_ref, yb_ref, bonus_ref, gate_ref, lng_ref, lnb_ref, hs_ref, x_ref, mod_ref,
                    wa_ref, wr_ref, g_ref, x1_ref, h2_ref):
    head_sum = hs_ref[...]

    def per_head_mean(v):
        return _dot(v.astype(BF16), head_sum) * (1.0 / RW_HEAD)

    y = yf_ref[...] + yb_ref[...]
    yc = y - per_head_mean(y)
    yn = yc * lax.rsqrt(per_head_mean(yc * yc) + GN_EPS) * lng_ref[...] + lnb_ref[...]
    rw = (yn + bonus_ref[...]) * gate_ref[...]
    acc = _dot(att_ref[...].astype(BF16), wa_ref[...]) + _dot(rw.astype(BF16), wr_ref[...])
    m = mod_ref[0]
    x1 = x_ref[...] + m[2:3] * acc
    x1_ref[...] = x1
    ms = jnp.mean(x1 * x1, axis=-1, keepdims=True)
    yo = x1 * lax.rsqrt(ms + NORM_EPS) * g_ref[...]
    h2_ref[...] = (yo * (1.0 + m[4:5]) + m[3:4]).astype(BF16)


def _out_proj(att2, y_f, y_b, bonus, gate, ln_g, ln_b, x2, mod3, w_a, w_r, g, seq, tm=256):
    t, d = x2.shape
    ka, kr = att2.shape[1], y_f.shape[1]
    per_b = seq // tm
    rows = lambda n: pl.BlockSpec((tm, n), lambda i: (i, 0))
    full = lambda shape: pl.BlockSpec(shape, lambda i: (0,) * len(shape))
    head_sum = (jnp.arange(kr)[:, None] // RW_HEAD == jnp.arange(kr)[None, :] // RW_HEAD).astype(BF16)
    return pl.pallas_call(
        _outproj_kernel,
        grid=(t // tm,),
        in_specs=[rows(ka), rows(kr), rows(kr), rows(kr), rows(kr), full((1, kr)), full((1, kr)), full((kr, kr)),
                  rows(d), pl.BlockSpec((1, 6, d), lambda i: (i // per_b, 0, 0)),
                  full((ka, d)), full((kr, d)), full((1, d))],
        out_specs=[rows(d), rows(d)],
        out_shape=[jax.ShapeDtypeStruct((t, d), F32), jax.ShapeDtypeStruct((t, d), BF16)],
        compiler_params=_params("arbitrary"),
        name="out_proj",
    )(att2, y_f, y_b, bonus, gate, ln_g.reshape(1, kr), ln_b.reshape(1, kr), head_sum, x2, mod3, w_a, w_r,
      g.reshape(1, d))


def _topk_ranks(scores, vals_refs, rank_refs, break_ties):
    nk, tm = scores[0].shape
    kiota = lax.broadcasted_iota(jnp.int32, (nk, tm), 0)

    def body(j, carry):
        out = []
        for (cur, rank), vals_ref in zip(carry, vals_refs):
            m = jnp.max(cur, axis=0, keepdims=True)
            hit = cur == m
            if break_ties:
                hit = kiota == jnp.min(jnp.where(hit, kiota, nk), axis=0, keepdims=True)
            vals_ref[pl.ds(j, 1), :] = m
            out.append((jnp.where(hit, NEG_INF, cur), jnp.where(hit, j, rank)))
        return tuple(out)

    init = tuple((s, jnp.full((nk, tm), PEER_TOPK, jnp.int32)) for s in scores)
    excess = jnp.zeros((1, tm), F32)
    for (cur, rank), rank_ref in zip(lax.fori_loop(0, PEER_TOPK, body, init), rank_refs):
        rank_ref[...] = rank
        removed = jnp.sum(jnp.where(cur == NEG_INF, 1.0, 0.0), axis=0, keepdims=True)
        excess = jnp.maximum(excess, jnp.abs(removed - PEER_TOPK))
    return jnp.max(excess)


def _pair_candidates():
    pieces = []
    for lvl in range(4):
        cnt = PEER_TOPK // (lvl + 1)
        pieces.append(("row", lvl, lvl, cnt))
        if lvl + 1 < cnt:
            pieces.append(("col", lvl, lvl + 1, cnt))
    return pieces


def _select_pairs(cand, ci, valid, sel_ref, break_ties):
    big = 4 * PEER_TOPK * PEER_TOPK

    def body(j, cur):
        m = jnp.max(cur, axis=0, keepdims=True)
        hit = cur == m
        if break_ties:
            hit = ci == jnp.min(jnp.where(hit, ci, big), axis=0, keepdims=True)
        return jnp.where(hit, NEG_INF, cur)

    cur = lax.fori_loop(0, PEER_TOPK, body, cand)
    sel = jnp.where((cur == NEG_INF) & valid, 1.0, 0.0)
    sel_ref[...] = sel
    return jnp.max(jnp.abs(jnp.sum(sel, axis=0, keepdims=True) - PEER_TOPK))


def _route_kernel(h_ref, wq_ref, keys_ref, p1_ref, cnt_ref, p2_ref, rk2_ref, q_scr, v1_ref, v2_ref, rank_ref,
                  sel_ref):
    tm = h_ref.shape[0]
    nk = PEER_NKEYS
    n_grp = tm // LANES
    q = _dot(h_ref[...], wq_ref[...]).astype(BF16)
    for hp in range(2 * PEER_HEADS):
        q_scr[hp] = q[:, hp * nk:(hp + 1) * nk]

    pieces = _pair_candidates()
    big = 4 * PEER_TOPK * PEER_TOPK
    cat = lambda parts: jnp.concatenate(parts, axis=1)

    def head(h, carry):
        s1 = _dot_nt(keys_ref[2 * h], q_scr[2 * h])
        s2 = _dot_nt(keys_ref[2 * h + 1], q_scr[2 * h + 1])
        probs = [s[:, g * LANES:(g + 1) * LANES] for s in (s1, s2) for g in range(n_grp)]
        vals = [ref.at[g] for ref in (v1_ref, v2_ref) for g in range(n_grp)]
        ranks = [rank_ref.at[i] for i in range(2 * n_grp)]
        excess = 0.0
        for prob, val, rank in zip(probs, vals, ranks):
            excess = jnp.maximum(excess, _topk_ranks([prob], [val], [rank], break_ties=False))

        @pl.when(excess > 0.0)
        def _():
            _topk_ranks(probs, vals, ranks, break_ties=True)

        rank1 = cat([rank_ref[g] for g in range(n_grp)])
        rank2 = cat([rank_ref[n_grp + g] for g in range(n_grp)])
        v1 = cat([v1_ref[g] for g in range(n_grp)])
        v2 = cat([v2_ref[g] for g in range(n_grp)])

        cands, cis, valids = [], [], []
        for kind, fixed, lo, hi in pieces:
            rows = SUBLANES * ((hi + SUBLANES - 1) // SUBLANES)
            ridx = lax.broadcasted_iota(jnp.int32, (rows, 1), 0)
            valid = (ridx >= lo) & (ridx < hi)
            if kind == "row":
                vsum = v1[fixed:fixed + 1] + v2[0:rows]
                ci = fixed * PEER_TOPK + ridx
            else:
                vsum = v1[0:rows] + v2[fixed:fixed + 1]
                ci = ridx * PEER_TOPK + fixed
            cands.append(jnp.where(valid, vsum, NEG_INF))
            cis.append(jnp.broadcast_to(jnp.where(valid, ci, big), (rows, tm)))
            valids.append(jnp.broadcast_to(valid, (rows, tm)))
        cand = jnp.concatenate(cands, axis=0)
        ci = jnp.concatenate(cis, axis=0)
        valid = jnp.concatenate(valids, axis=0)
        off = _select_pairs(cand, ci, valid, sel_ref, break_ties=False)

        @pl.when(off > 0.0)
        def _():
            _select_pairs(cand, ci, valid, sel_ref, break_ties=True)

        sel = sel_ref[...]
        zsum = jnp.sum(sel * jnp.exp(cand - (v1[0:1] + v2[0:1])), axis=0, keepdims=True)
        aidx = lax.broadcasted_iota(jnp.int32, (PEER_TOPK, 1), 0)
        per_a = jnp.zeros((PEER_TOPK, tm), F32)
        row0 = 0
        for kind, fixed, lo, hi in pieces:
            rows = SUBLANES * ((hi + SUBLANES - 1) // SUBLANES)
            part = sel[row0:row0 + rows]
            row0 += rows
            if kind == "row":
                per_a = per_a + jnp.where(aidx == fixed, jnp.sum(part, axis=0, keepdims=True), 0.0)
            elif rows == PEER_TOPK:
                per_a = per_a + part
            else:
                per_a = per_a + jnp.concatenate([part, jnp.zeros((PEER_TOPK - rows, tm), F32)], axis=0)
        cnt = jnp.zeros((nk, tm), F32)
        for a in range(PEER_TOPK):
            cnt = jnp.where(rank1 == a, per_a[a:a + 1], cnt)
        p1_ref[h] = jnp.where(rank1 < PEER_TOPK, jnp.exp(s1 - v1[0:1]) / (zsum * 2.0 ** 0.5), 0.0)
        cnt_ref[h] = cnt
        p2_ref[h] = jnp.where(rank2 < PEER_TOPK, jnp.exp(s2 - v2[0:1]), 0.0).astype(BF16)
        rk2_ref[h] = rank2.astype(F32).astype(BF16)
        return carry

    lax.fori_loop(0, PEER_HEADS, head, 0)


def _route(h2, wq, keys, tm=256):
    t, d = h2.shape
    nq = wq.shape[1]
    nk = PEER_NKEYS
    out = lambda dt: jax.ShapeDtypeStruct((PEER_HEADS, nk, t), dt)
    ospec = pl.BlockSpec((PEER_HEADS, nk, tm), lambda i: (0, 0, i))
    return pl.pallas_call(
        _route_kernel,
        grid=(t // tm,),
        in_specs=[pl.BlockSpec((tm, d), lambda i: (i, 0)),
                  pl.BlockSpec((d, nq), lambda i: (0, 0)),
                  pl.BlockSpec((2 * PEER_HEADS, nk, nk), lambda i: (0, 0, 0))],
        out_specs=[ospec] * 4,
        out_shape=[out(F32), out(F32), out(BF16), out(BF16)],
        scratch_shapes=[pltpu.VMEM((2 * PEER_HEADS, tm, nk), BF16),
                        pltpu.VMEM((tm // LANES, PEER_TOPK, LANES), F32),
                        pltpu.VMEM((tm // LANES, PEER_TOPK, LANES), F32),
                        pltpu.VMEM((2 * tm // LANES, nk, LANES), jnp.int32),
                        pltpu.VMEM((sum(SUBLANES * ((hi + SUBLANES - 1) // SUBLANES)
                                        for _, _, _, hi in _pair_candidates()), tm), F32)],
        compiler_params=_params("parallel"),
        name="peer_route",
    )(h2, wq, keys)


def _expert_kernel(n_eblk, h_ref, u_ref, vt_ref, p1_ref, cnt_ref, p2_ref, rk2_ref, x1_ref, mod_ref,
                   o_ref, acc_ref, act_ref, gate_ref):
    j = pl.program_id(1)
    te = u_ref.shape[0]
    tm = h_ref.shape[0]
    nk = PEER_NKEYS
    kb_rows = MXU_WIDTH
    per_kb = kb_rows // nk
    tile = 2 * SUBLANES

    @pl.when(j == 0)
    def _():
        acc_ref[...] = jnp.zeros_like(acc_ref)

    for sb in range(te // nk):
        e1 = j * (te // nk) + sb
        gate = None
        for h in range(PEER_HEADS):
            c = jnp.broadcast_to(cnt_ref[h, pl.ds(e1, 1), :], (tile, tm)).astype(BF16)
            p = jnp.broadcast_to(p1_ref[h, pl.ds(e1, 1), :], (tile, tm)).astype(BF16)
            rk2 = rk2_ref[h].reshape(nk // tile, tile, tm)
            p2 = p2_ref[h].reshape(nk // tile, tile, tm)
            term = jnp.where(rk2 < c[None], p2, jnp.zeros_like(p2)) * p[None]
            gate = term if gate is None else gate + term
        gate_ref[sb * nk:(sb + 1) * nk, :] = gate.reshape(nk, tm)
    hb = h_ref[...]
    for kb in range(te // kb_rows):
        rows = slice(kb * kb_rows, (kb + 1) * kb_rows)
        act_ref[rows, :] = _dot_nt(u_ref[rows, :], hb)
    acc = None
    for kb in range(te // kb_rows):
        rows = slice(kb * kb_rows, (kb + 1) * kb_rows)
        t = act_ref[rows, :]
        w = gate_ref[rows, :] * (t + t * lax.erf(t)).astype(BF16)
        part = _dot(vt_ref[:, rows], w)
        acc = part if acc is None else acc + part
    acc_ref[...] += acc

    @pl.when(j == n_eblk - 1)
    def _():
        o_ref[...] = x1_ref[...] + mod_ref[0][5:6] * acc_ref[...].T


def _experts(h2, u, vt, p1, cnt, p2, rk2, x1, mod3, seq, tm=512, te=1024):
    t, d = h2.shape
    ne = u.shape[0]
    nk = PEER_NKEYS
    n_eblk = ne // te
    per_b = seq // tm
    rspec = pl.BlockSpec((PEER_HEADS, nk, tm), lambda i, j: (0, 0, i))
    return pl.pallas_call(
        functools.partial(_expert_kernel, n_eblk),
        grid=(t // tm, n_eblk),
        in_specs=[pl.BlockSpec((tm, d), lambda i, j: (i, 0)),
                  pl.BlockSpec((te, d), lambda i, j: (j, 0)),
                  pl.BlockSpec((d, te), lambda i, j: (0, j)),
                  rspec, rspec, rspec, rspec,
                  pl.BlockSpec((tm, d), lambda i, j: (i, 0)),
                  pl.BlockSpec((1, 6, d), lambda i, j: (i // per_b, 0, 0))],
        out_specs=pl.BlockSpec((tm, d), lambda i, j: (i, 0)),
        out_shape=jax.ShapeDtypeStruct((t, d), F32),
        scratch_shapes=[pltpu.VMEM((d, tm), F32), pltpu.VMEM((te, tm), F32), pltpu.VMEM((te, tm), BF16)],
        compiler_params=_params("parallel", "arbitrary"),
        name="peer_experts",
    )(h2, u, vt, p1, cnt, p2, rk2, x1, mod3)


def kernel(x, c, ada_w, ada_b, norm1_g, w_in, da_qnorm_g, da_knorm_g, da_lambda, da_subln_g, rw_shift_mu, rw_w0,
           rw_w2, rw_a0, rw_a2, rw_g2, rw_k_k, rw_k_a, rw_r_k, rw_ln_g, rw_ln_b, w_out, norm2_g, peer_wq,
           peer_keys, peer_u, peer_v):
    bsz, seq, d = x.shape
    depth = ada_w.shape[0]
    t = bsz * seq
    for l in range(depth):
        mod3 = _ada_mod(c, ada_w[l], ada_b[l]).reshape(bsz, 6, d)
        x2 = x.reshape(t, d)
        w_l = w_in[l].astype(BF16)
        p_att, p_rw = _in_proj(x2, mod3, norm1_g[l], w_l[:, :ATT_COLS], w_l[:, ATT_COLS:], seq)
        att = _diff_attention(p_att.reshape(bsz, seq, ATT_COLS), da_qnorm_g[l], da_knorm_g[l], da_lambda[l],
                              da_subln_g[l])
        ra, y0, g, h, bonus, gate = _rwkv_chunks(p_rw.reshape(bsz, seq, RWKV_COLS), rw_shift_mu[l], rw_w0[l],
                                                 rw_w2[l], rw_a0[l], rw_a2[l], rw_g2[l], rw_k_k[l], rw_k_a[l],
                                                 rw_r_k[l].reshape(-1))
        y_f, y_b = _rwkv_scan(ra, y0, g, h)
        wo = w_out[l].astype(BF16)
        aw = att.shape[-1]
        flat = lambda a: a.reshape(t, RW_WIDTH)
        x1, h2 = _out_proj(att.reshape(t, aw), flat(y_f), flat(y_b), flat(bonus), flat(gate), rw_ln_g[l],
                           rw_ln_b[l], x2, mod3, wo[:aw], wo[aw:], norm2_g[l], seq)
        keys = peer_keys[l].reshape(2 * PEER_HEADS, PEER_NKEYS, -1).astype(BF16)
        p1, cnt, p2, rk2 = _route(h2, peer_wq[l].astype(BF16), keys)
        u_scaled = (peer_u[l] * 2.0 ** -0.5).astype(BF16)
        x = _experts(h2, u_scaled, peer_v[l].T.astype(BF16), p1, cnt, p2, rk2, x1, mod3, seq).reshape(bsz, seq, d)
    return x
```

```python
import functools
import math

import jax
import jax.numpy as jnp
from jax import lax
from jax.experimental import pallas as pl
from jax.experimental.pallas import tpu as pltpu

F32 = jnp.float32
BF16 = jnp.bfloat16
HIGHEST = lax.Precision.HIGHEST

LANES = 128
SUBLANES = 8
MXU_WIDTH = 256
VMEM_LIMIT_BYTES = 56 * 1024 * 1024

DA_HEAD_DIM = 64
DA_HEADS = 4
RW_HEAD = 64
RW_HEADS = 8
RW_PAIRS = RW_HEADS // 2
RW_WIDTH = RW_HEADS * RW_HEAD
LORA_COLS = 128
RWKV_COLS = 3 * RW_WIDTH + 3 * LORA_COLS
ATT_COLS = 3 * 2 * DA_HEAD_DIM * DA_HEADS
GN_EPS = 64e-5
NORM_EPS = 1e-6
LAMBDA_INIT = 0.8 - 0.6 * math.exp(-0.3 * 0)
PEER_HEADS = 8
PEER_NKEYS = 128
PEER_TOPK = 16
RW_CHUNK = 64
NEG_INF = float("-inf")


def _params(*sem):
    return pltpu.CompilerParams(dimension_semantics=sem, vmem_limit_bytes=VMEM_LIMIT_BYTES)


def _dot(a, b, precision=None):
    return jnp.dot(a, b, preferred_element_type=F32, precision=precision)


def _dot_nt(a, b, precision=None):
    return lax.dot_general(a, b, (((1,), (1,)), ((), ())), preferred_element_type=F32, precision=precision)


def _dot_tn(a, b, precision=None):
    return lax.dot_general(a, b, (((0,), (0,)), ((), ())), preferred_element_type=F32, precision=precision)


def _group_matrix(n, group, value):
    shift = group.bit_length() - 1
    r = lax.broadcasted_iota(jnp.int32, (n, n), 0) >> shift
    c = lax.broadcasted_iota(jnp.int32, (n, n), 1) >> shift
    return jnp.where(r == c, value, 0.0).astype(F32)


def _ada_kernel(c_ref, w_ref, b_ref, o_ref):
    c = c_ref[...]
    s = c * jax.nn.sigmoid(c)
    o_ref[...] = _dot(s, w_ref[...], HIGHEST) + b_ref[...]


def _ada_mod(c, w, b):
    bsz, d = c.shape
    n = w.shape[1]
    tn = 1024
    return pl.pallas_call(
        _ada_kernel,
        grid=(n // tn,),
        in_specs=[pl.BlockSpec((bsz, d), lambda j: (0, 0)),
                  pl.BlockSpec((d, tn), lambda j: (0, j)),
                  pl.BlockSpec((1, tn), lambda j: (0, j))],
        out_specs=pl.BlockSpec((bsz, tn), lambda j: (0, j)),
        out_shape=jax.ShapeDtypeStruct((bsz, n), F32),
        compiler_params=_params("arbitrary"),
        name="ada_mod",
    )(c, w, b.reshape(1, n))


def _inproj_kernel(x_ref, mod_ref, g_ref, wa_ref, wr_ref, oa_ref, or_ref):
    x = x_ref[...]
    ms = jnp.mean(x * x, axis=-1, keepdims=True)
    y = x * lax.rsqrt(ms + NORM_EPS) * g_ref[...]
    m = mod_ref[0]
    h = (y * (1.0 + m[1:2]) + m[0:1]).astype(BF16)
    oa_ref[...] = _dot(h, wa_ref[...])
    or_ref[...] = _dot(h, wr_ref[...])


def _in_proj(x2, mod3, g, w_att, w_rw, seq, tm=256):
    t, d = x2.shape
    na, nr = w_att.shape[1], w_rw.shape[1]
    per_b = seq // tm
    return pl.pallas_call(
        _inproj_kernel,
        grid=(t // tm,),
        in_specs=[pl.BlockSpec((tm, d), lambda i: (i, 0)),
                  pl.BlockSpec((1, 6, d), lambda i: (i // per_b, 0, 0)),
                  pl.BlockSpec((1, d), lambda i: (0, 0)),
                  pl.BlockSpec((d, na), lambda i: (0, 0)),
                  pl.BlockSpec((d, nr), lambda i: (0, 0))],
        out_specs=[pl.BlockSpec((tm, na), lambda i: (i, 0)),
                   pl.BlockSpec((tm, nr), lambda i: (i, 0))],
        out_shape=[jax.ShapeDtypeStruct((t, na), F32), jax.ShapeDtypeStruct((t, nr), F32)],
        compiler_params=_params("arbitrary"),
        name="in_proj",
    )(x2, mod3, g.reshape(1, d), w_att, w_rw)


def _attn_kernel(tq, q_ref, k_ref, v_ref, qg_ref, kg_ref, lam_ref, slope_ref, sg_ref, o_ref, qs, ks, vs):
    seq = q_ref.shape[1]
    width = 2 * DA_HEAD_DIM
    avg = _group_matrix(width, DA_HEAD_DIM, 1.0 / DA_HEAD_DIM)

    def qk_norm(x, g):
        ms = _dot(x * x, avg)
        return x * lax.rsqrt(ms + NORM_EPS) * g

    qs[...] = (qk_norm(q_ref[0], qg_ref[...]) * (DA_HEAD_DIM ** -0.5)).astype(BF16)
    ks[...] = qk_norm(k_ref[0], kg_ref[...]).astype(BF16)
    vs[...] = v_ref[0].astype(BF16)

    lam = lam_ref[...]
    lam_full = (jnp.exp(jnp.sum(lam[0:1] * lam[1:2], axis=-1, keepdims=True))
                - jnp.exp(jnp.sum(lam[2:3] * lam[3:4], axis=-1, keepdims=True)) + LAMBDA_INIT)
    slope = slope_ref[0][:, 0:1]
    first = lax.broadcasted_iota(jnp.int32, (1, width), 1) < DA_HEAD_DIM
    cols = lax.broadcasted_iota(jnp.int32, (1, seq), 1)
    sg = sg_ref[...] * (1.0 - LAMBDA_INIT)

    def body(i, carry):
        r0 = pl.multiple_of(i * tq, tq)
        qt = qs[pl.ds(r0, tq), :]
        q0 = jnp.where(first, qt, jnp.zeros_like(qt))
        q1 = jnp.where(first, jnp.zeros_like(qt), qt)
        kk = ks[...]
        rows = r0 + lax.broadcasted_iota(jnp.int32, (tq, 1), 0)
        bias = slope * jnp.abs(rows - cols).astype(F32)
        s0 = _dot_nt(q0, kk) - bias
        s1 = _dot_nt(q1, kk) - bias
        p0 = jnp.exp(s0 - jnp.max(s0, axis=-1, keepdims=True))
        p1 = jnp.exp(s1 - jnp.max(s1, axis=-1, keepdims=True))
        w0 = 1.0 / jnp.sum(p0, axis=-1, keepdims=True)
        w1 = lam_full / jnp.sum(p1, axis=-1, keepdims=True)
        o = _dot(p0.astype(BF16), vs[...]) * w0 - _dot(p1.astype(BF16), vs[...]) * w1
        ms = jnp.mean(o * o, axis=-1, keepdims=True)
        o_ref[0, pl.ds(r0, tq), :] = o * lax.rsqrt(ms + NORM_EPS) * sg
        return carry

    lax.fori_loop(0, seq // tq, body, 0)


def _diff_attention(p_att3, qn_g, kn_g, lam, subln_g, tq=256):
    bsz, seq, _ = p_att3.shape
    width = 2 * DA_HEAD_DIM
    slopes = jnp.asarray([2.0 ** (-8.0 * (h + 1) / DA_HEADS) for h in range(DA_HEADS)], F32)
    slopes = jnp.broadcast_to(slopes[:, None, None], (DA_HEADS, 1, width))
    blk = lambda off: pl.BlockSpec((1, seq, width), lambda b, h: (b, 0, off + h))
    full = lambda shape: pl.BlockSpec(shape, lambda b, h: (0,) * len(shape))
    return pl.pallas_call(
        functools.partial(_attn_kernel, tq),
        grid=(bsz, DA_HEADS),
        in_specs=[blk(0), blk(DA_HEADS), blk(2 * DA_HEADS),
                  full((1, width)), full((1, width)), full((4, DA_HEAD_DIM)),
                  pl.BlockSpec((1, 1, width), lambda b, h: (h, 0, 0)),
                  full((1, width))],
        out_specs=pl.BlockSpec((1, seq, width), lambda b, h: (b, 0, h)),
        out_shape=jax.ShapeDtypeStruct((bsz, seq, DA_HEADS * width), F32),
        scratch_shapes=[pltpu.VMEM((seq, width), BF16)] * 3,
        compiler_params=_params("arbitrary", "arbitrary"),
        name="diff_attention",
    )(p_att3, p_att3, p_att3,
      jnp.tile(qn_g.reshape(1, DA_HEAD_DIM), (1, 2)), jnp.tile(kn_g.reshape(1, DA_HEAD_DIM), (1, 2)),
      lam, slopes, subln_g.reshape(1, width))


def _rwkv_chunk_kernel(n_chunks, cur_ref, prev_ref, next_ref, mu_ref, w0_ref, w2_ref, a0_ref, a2_ref, g2_ref,
                       kk_ref, ka_ref, rk_ref, hs_ref, ra_ref, y0_ref, g_ref, h_ref, bonus_ref, gate_ref):
    ch = RW_CHUNK
    w = RW_WIDTH
    pw = 2 * RW_HEAD
    chunk = pl.program_id(1)

    pc = cur_ref[0]
    row = lax.broadcasted_iota(jnp.int32, (ch, 1), 0)
    pv = prev_ref[0][SUBLANES - 1:SUBLANES, :] * (chunk > 0).astype(F32)
    nx = next_ref[0][0:1, :] * (chunk < n_chunks - 1).astype(F32)
    prev = jnp.where(row == 0, pv, pltpu.roll(pc, 1, 0))
    nxt = jnp.where(row == ch - 1, nx, pltpu.roll(pc, ch - 1, 0))
    mu = mu_ref[...]
    ps = pc + mu[0:1] * (prev - pc) + mu[1:2] * (nxt - pc)
    r, k, v = ps[:, 0:w], ps[:, w:2 * w], ps[:, 2 * w:3 * w]
    wd = ps[:, 3 * w:3 * w + LORA_COLS]
    ad = ps[:, 3 * w + LORA_COLS:3 * w + 2 * LORA_COLS]
    gd = ps[:, 3 * w + 2 * LORA_COLS:3 * w + 3 * LORA_COLS]

    head_sum = hs_ref[...]

    def per_head_sum(x):
        return _dot(x.astype(BF16), head_sum)

    kkr = k * kk_ref[...]
    kk = kkr / jnp.maximum(jnp.sqrt(per_head_sum(kkr * kkr)), 1e-12)
    bonus_ref[0] = per_head_sum(r * k * rk_ref[...]) * v
    gate_ref[0] = _dot(jax.nn.sigmoid(gd), g2_ref[...])

    ti = lax.broadcasted_iota(jnp.int32, (ch, ch), 0)
    tj = lax.broadcasted_iota(jnp.int32, (ch, ch), 1)
    eye = (ti == tj).astype(F32)
    lane = lax.broadcasted_iota(jnp.int32, (1, pw), 1)
    head_lanes = (lane < RW_HEAD, lane >= RW_HEAD)
    pi = lax.broadcasted_iota(jnp.int32, (pw, pw), 0)
    pj = lax.broadcasted_iota(jnp.int32, (pw, pw), 1)
    same_head = (pi < RW_HEAD) == (pj < RW_HEAD)
    eye_pair = (pi == pj).astype(F32)
    v16 = v.astype(BF16)
    tanh_wd = jnp.tanh(wd)

    probs = []
    pairs = []
    for d in range(2):
        z = w0_ref[d:d + 1] + _dot(tanh_wd, w2_ref[d])
        logdec = -jax.nn.sigmoid(z) * math.exp(-0.5)
        a = jax.nn.sigmoid(a0_ref[d:d + 1] + _dot(ad, a2_ref[d]))
        kd = k * (1.0 + (a - 1.0) * ka_ref[...])
        before = (tj < ti) if d == 0 else (tj > ti)
        upto = before | (tj == ti)
        cum = _dot(upto.astype(F32), logdec, HIGHEST)
        total = cum[ch - 1:ch] if d == 0 else cum[0:1]
        w_in, w_ex, w_inv, w_tot = jnp.exp(cum), jnp.exp(cum - logdec), jnp.exp(-cum), jnp.exp(total)
        a_bar = -kk * w_ex
        b_bar = (kk * a * w_inv).astype(BF16)
        k_bar = (kd * w_inv).astype(BF16)
        r_bar = r * (w_in if d == 0 else w_ex)
        ymask = upto if d == 0 else before
        later, earlier = (ti, tj) if d == 0 else (tj, ti)
        off_masks = []
        for lvl in range(int(math.log2(ch))):
            same_pair = (ti >> (lvl + 1)) == (tj >> (lvl + 1))
            off_masks.append(same_pair & (((later >> lvl) & 1) == 1) & (((earlier >> lvl) & 1) == 0))
        for p in range(RW_PAIRS):
            sl = slice(p * pw, (p + 1) * pw)
            pairs.append(dict(d=d, p=p, b=b_bar[:, sl], k=k_bar[:, sl], v=v16[:, sl], r=r_bar[:, sl],
                              w_tot=w_tot[:, sl]))
            for hh in range(2):
                mh = head_lanes[hh]
                probs.append(dict(pair=len(pairs) - 1, before=before, ymask=ymask, off=off_masks,
                                  a=jnp.where(mh, a_bar[:, sl], 0.0).astype(BF16),
                                  r=jnp.where(mh, r_bar[:, sl], 0.0).astype(BF16),
                                  v=jnp.where(mh, v16[:, sl], jnp.zeros_like(v16[:, sl]))))

    for q in probs:
        pr = pairs[q["pair"]]
        q["ab"] = _dot_nt(q["a"], pr["b"])
        ak = _dot_nt(q["a"], pr["k"])
        q["rb"] = jnp.where(q["ymask"], _dot_nt(q["r"], pr["b"]), 0.0).astype(BF16)
        rk = _dot_nt(q["r"], pr["k"])
        q["ak"] = jnp.where(q["before"], ak, 0.0).astype(BF16)
        q["rk"] = jnp.where(q["ymask"], rk, 0.0).astype(BF16)
    for q in probs:
        q["akv"] = _dot(q["ak"], q["v"])
        q["rkv"] = _dot(q["rk"], q["v"])
        q["inv"] = eye + jnp.where(q["off"][0], q["ab"], 0.0)
    for lvl in range(1, int(math.log2(ch))):
        for q in probs:
            q["tmp"] = _dot(q["inv"].astype(BF16), jnp.where(q["off"][lvl], q["ab"], 0.0).astype(BF16))
        for q in probs:
            q["inv"] = q["inv"] + _dot(q["tmp"].astype(BF16), q["inv"].astype(BF16))
    for q in probs:
        rhs = jnp.concatenate([q["a"], q["akv"].astype(BF16)], axis=1)
        q["x"] = _dot(q["inv"].astype(BF16), rhs)
    for q in probs:
        q["z"] = _dot(q["rb"], q["x"].astype(BF16))
    for i, pr in enumerate(pairs):
        q0, q1 = probs[2 * i], probs[2 * i + 1]
        x = q0["x"] + q1["x"]
        z = q0["z"] + q1["z"]
        ra = pr["r"] + z[:, :pw]
        y0 = z[:, pw:] + q0["rkv"] + q1["rkv"]
        xb = _dot_tn(x.astype(BF16), pr["b"])
        vk = _dot_tn(pr["v"], pr["k"])
        g = (eye_pair + jnp.where(same_head, xb[:pw], 0.0)) * pr["w_tot"]
        h = jnp.where(same_head, xb[pw:] + vk, 0.0) * pr["w_tot"]
        d, p = pr["d"], pr["p"]
        sl = slice(p * pw, (p + 1) * pw)
        ra_ref[0, d, :, sl] = ra.astype(BF16)
        y0_ref[0, d, :, sl] = y0.astype(BF16)
        g_ref[0, d, 0, :, sl] = g.astype(BF16)
        h_ref[0, d, 0, :, sl] = h.astype(BF16)


def _pad_lora(w2):
    keep = jnp.arange(2)[:, None, None, None] == jnp.arange(2)[None, :, None, None]
    return jnp.where(keep, w2[None], 0.0).reshape(2, -1, w2.shape[-1])


def _rwkv_chunks(p_rw3, mu, w0, w2, a0, a2, g2, k_k, k_a, r_k):
    bsz, seq, cols = p_rw3.shape
    ch, w, pw = RW_CHUNK, RW_WIDTH, 2 * RW_HEAD
    n_chunks = seq // ch
    sub_per_chunk = ch // SUBLANES
    n_sub = seq // SUBLANES
    full = lambda shape: pl.BlockSpec(shape, lambda b, c: (0,) * len(shape))
    row = lambda x: x.reshape(1, w)
    head_sum = (jnp.arange(w)[:, None] // RW_HEAD == jnp.arange(w)[None, :] // RW_HEAD).astype(BF16)
    tok = lambda dt: jax.ShapeDtypeStruct((bsz, 2, seq, w), dt)
    mat = lambda dt: jax.ShapeDtypeStruct((bsz, 2, n_chunks, pw, w), dt)
    tok_spec = pl.BlockSpec((1, 2, ch, w), lambda b, c: (b, 0, c, 0))
    mat_spec = pl.BlockSpec((1, 2, 1, pw, w), lambda b, c: (b, 0, c, 0, 0))
    one_spec = pl.BlockSpec((1, ch, w), lambda b, c: (b, c, 0))
    return pl.pallas_call(
        functools.partial(_rwkv_chunk_kernel, n_chunks),
        grid=(bsz, n_chunks),
        in_specs=[pl.BlockSpec((1, ch, cols), lambda b, c: (b, c, 0)),
                  pl.BlockSpec((1, SUBLANES, cols), lambda b, c: (b, jnp.maximum(c * sub_per_chunk - 1, 0), 0)),
                  pl.BlockSpec((1, SUBLANES, cols),
                               lambda b, c: (b, jnp.minimum((c + 1) * sub_per_chunk, n_sub - 1), 0)),
                  full((2, cols)), full((2, w)), full((2, LORA_COLS, w)), full((2, w)), full((2, LORA_COLS, w)),
                  full((LORA_COLS, w)), full((1, w)), full((1, w)), full((1, w)), full((w, w))],
        out_specs=[tok_spec, tok_spec, mat_spec, mat_spec, one_spec, one_spec],
        out_shape=[tok(BF16), tok(BF16), mat(BF16), mat(BF16),
                   jax.ShapeDtypeStruct((bsz, seq, w), F32), jax.ShapeDtypeStruct((bsz, seq, w), F32)],
        compiler_params=_params("parallel", "parallel"),
        name="rwkv7_chunks",
    )(p_rw3, p_rw3, p_rw3, mu, w0, _pad_lora(w2), a0, _pad_lora(a2), g2, row(k_k), row(k_a), row(r_k), head_sum)


def _rwkv_scan_kernel(ra0_ref, ra1_ref, y00_ref, y01_ref, g0_ref, g1_ref, h0_ref, h1_ref, o0_ref, o1_ref, st_ref):
    pw = 2 * RW_HEAD

    @pl.when(pl.program_id(1) == 0)
    def _():
        st_ref[...] = jnp.zeros_like(st_ref)

    dirs = ((ra0_ref, y00_ref, g0_ref, h0_ref, o0_ref), (ra1_ref, y01_ref, g1_ref, h1_ref, o1_ref))
    for d, (ra_ref, y0_ref, g_ref, h_ref, o_ref) in enumerate(dirs):
        for p in range(RW_PAIRS):
            sl = slice(p * pw, (p + 1) * pw)
            s = st_ref[d, p]
            s16 = s.astype(BF16)
            o_ref[0, :, sl] = _dot_nt(ra_ref[0, 0, :, sl], s16) + y0_ref[0, 0, :, sl]
            st_ref[d, p] = _dot(s16, g_ref[0, 0, 0, :, sl]) + h_ref[0, 0, 0, :, sl]


def _rwkv_scan(ra, y0, g, h):
    bsz, _, seq, w = ra.shape
    ch, pw = RW_CHUNK, 2 * RW_HEAD
    n_chunks = seq // ch
    tok = lambda d: pl.BlockSpec((1, 1, ch, w), lambda b, c: (b, d, c if d == 0 else n_chunks - 1 - c, 0))
    mat = lambda d: pl.BlockSpec((1, 1, 1, pw, w), lambda b, c: (b, d, c if d == 0 else n_chunks - 1 - c, 0, 0))
    out = lambda d: pl.BlockSpec((1, ch, w), lambda b, c: (b, c if d == 0 else n_chunks - 1 - c, 0))
    return pl.pallas_call(
        _rwkv_scan_kernel,
        grid=(bsz, n_chunks),
        in_specs=[tok(0), tok(1), tok(0), tok(1), mat(0), mat(1), mat(0), mat(1)],
        out_specs=[out(0), out(1)],
        out_shape=[jax.ShapeDtypeStruct((bsz, seq, w), F32)] * 2,
        scratch_shapes=[pltpu.VMEM((2, RW_PAIRS, pw, pw), F32)],
        compiler_params=_params("parallel", "arbitrary"),
        name="rwkv7_scan",
    )(ra, ra, y0, y0, g, g, h, h)


def _outproj_kernel(att_ref, yf_ref, yb_ref, bonus_ref, gate_ref, lng_ref, lnb_ref, hs_ref, x_ref, mod_ref,
                    wa_ref, wr_ref, g_ref, x1_ref, h2_ref):
    head_sum = hs_ref[...]

    def per_head_mean(v):
        return _dot(v.astype(BF16), head_sum) * (1.0 / RW_HEAD)

    y = yf_ref[...] + yb_ref[...]
    yc = y - per_head_mean(y)
    yn = yc * lax.rsqrt(per_head_mean(yc * yc) + GN_EPS) * lng_ref[...] + lnb_ref[...]
    rw = (yn + bonus_ref[...]) * gate_ref[...]
    acc = _dot(att_ref[...].astype(BF16), wa_ref[...]) + _dot(rw.astype(BF16), wr_ref[...])
    m = mod_ref[0]
    x1 = x_ref[...] + m[2:3] * acc
    x1_ref[...] = x1
    ms = jnp.mean(x1 * x1, axis=-1, keepdims=True)
    yo = x1 * lax.rsqrt(ms + NORM_EPS) * g_ref[...]
    h2_ref[...] = (yo * (1.0 + m[4:5]) + m[3:4]).astype(BF16)


def _out_proj(att2, y_f, y_b, bonus, gate, ln_g, ln_b, x2, mod3, w_a, w_r, g, seq, tm=256):
    t, d = x2.shape
    ka, kr = att2.shape[1], y_f.shape[1]
    per_b = seq // tm
    rows = lambda n: pl.BlockSpec((tm, n), lambda i: (i, 0))
    full = lambda shape: pl.BlockSpec(shape, lambda i: (0,) * len(shape))
    head_sum = (jnp.arange(kr)[:, None] // RW_HEAD == jnp.arange(kr)[None, :] // RW_HEAD).astype(BF16)
    return pl.pallas_call(
        _outproj_kernel,
        grid=(t // tm,),
        in_specs=[rows(ka), rows(kr), rows(kr), rows(kr), rows(kr), full((1, kr)), full((1, kr)), full((kr, kr)),
                  rows(d), pl.BlockSpec((1, 6, d), lambda i: (i // per_b, 0, 0)),
                  full((ka, d)), full((kr, d)), full((1, d))],
        out_specs=[rows(d), rows(d)],
        out_shape=[jax.ShapeDtypeStruct((t, d), F32), jax.ShapeDtypeStruct((t, d), BF16)],
        compiler_params=_params("arbitrary"),
        name="out_proj",
    )(att2, y_f, y_b, bonus, gate, ln_g.reshape(1, kr), ln_b.reshape(1, kr), head_sum, x2, mod3, w_a, w_r,
      g.reshape(1, d))


def _topk_ranks(scores, vals_refs, rank_refs, break_ties):
    nk, tm = scores[0].shape
    kiota = lax.broadcasted_iota(jnp.int32, (nk, tm), 0)

    def body(j, carry):
        out = []
        for (cur, rank), vals_ref in zip(carry, vals_refs):
            m = jnp.max(cur, axis=0, keepdims=True)
            hit = cur == m
            if break_ties:
                hit = kiota == jnp.min(jnp.where(hit, kiota, nk), axis=0, keepdims=True)
            vals_ref[pl.ds(j, 1), :] = m
            out.append((jnp.where(hit, NEG_INF, cur), None if rank is None else jnp.where(hit, j, rank)))
        return tuple(out)

    init = tuple((s, None if ref is None else jnp.full((nk, tm), PEER_TOPK, jnp.int32))
                 for s, ref in zip(scores, rank_refs))
    excess = jnp.zeros((1, tm), F32)
    for (cur, rank), rank_ref in zip(lax.fori_loop(0, PEER_TOPK, body, init), rank_refs):
        if rank_ref is not None:
            rank_ref[...] = rank
        removed = jnp.sum(jnp.where(cur == NEG_INF, 1.0, 0.0), axis=0, keepdims=True)
        excess = jnp.maximum(excess, jnp.abs(removed - PEER_TOPK))
    return jnp.max(excess)


def _pair_candidates():
    pieces = []
    for lvl in range(4):
        cnt = PEER_TOPK // (lvl + 1)
        pieces.append(("row", lvl, lvl, cnt))
        if lvl + 1 < cnt:
            pieces.append(("col", lvl, lvl + 1, cnt))
    return pieces


def _select_pairs(cand, ci, valid, sel_ref, break_ties):
    big = 4 * PEER_TOPK * PEER_TOPK

    def body(j, cur):
        m = jnp.max(cur, axis=0, keepdims=True)
        hit = cur == m
        if break_ties:
            hit = ci == jnp.min(jnp.where(hit, ci, big), axis=0, keepdims=True)
        return jnp.where(hit, NEG_INF, cur)

    cur = lax.fori_loop(0, PEER_TOPK, body, cand)
    sel = jnp.where((cur == NEG_INF) & valid, 1.0, 0.0)
    sel_ref[...] = sel
    return jnp.max(jnp.abs(jnp.sum(sel, axis=0, keepdims=True) - PEER_TOPK))


def _route_kernel(h_ref, wq_ref, keys_ref, p1_ref, cnt_ref, p2_ref, rk2_ref, q_scr, v1_ref, v2_ref, rank_ref,
                  sel_ref, key_ref, kref_ref):
    tm = h_ref.shape[0]
    nk = PEER_NKEYS
    n_grp = tm // LANES
    q = _dot(h_ref[...], wq_ref[...]).astype(BF16)
    for hp in range(2 * PEER_HEADS):
        q_scr[hp] = q[:, hp * nk:(hp + 1) * nk]

    pieces = _pair_candidates()
    big = 4 * PEER_TOPK * PEER_TOPK
    cat = lambda parts: jnp.concatenate(parts, axis=1)

    def head(h, carry):
        s1 = _dot_nt(keys_ref[2 * h], q_scr[2 * h])
        s2 = _dot_nt(keys_ref[2 * h + 1], q_scr[2 * h + 1])
        probs = [s[:, g * LANES:(g + 1) * LANES] for s in (s1, s2) for g in range(n_grp)]
        vals = [ref.at[g] for ref in (v1_ref, v2_ref) for g in range(n_grp)]
        ranks = [rank_ref.at[i] for i in range(2 * n_grp)]
        excess = 0.0
        for g in range(n_grp):
            excess = jnp.maximum(excess, _topk_ranks([probs[g], probs[n_grp + g]], [vals[g], vals[n_grp + g]],
                                                     [None, ranks[n_grp + g]], break_ties=False))
        v1 = cat([v1_ref[g] for g in range(n_grp)])
        key_ref[...] = s1
        kref_ref[...] = v1

        @pl.when(excess > 0.0)
        def _():
            _topk_ranks(probs, vals, ranks, break_ties=True)
            key_ref[...] = cat([rank_ref[g] for g in range(n_grp)]).astype(F32)
            kref_ref[...] = jnp.broadcast_to(lax.broadcasted_iota(jnp.int32, (PEER_TOPK, 1), 0).astype(F32),
                                             (PEER_TOPK, tm))

        rank2 = cat([rank_ref[n_grp + g] for g in range(n_grp)])
        v1 = cat([v1_ref[g] for g in range(n_grp)])
        v2 = cat([v2_ref[g] for g in range(n_grp)])

        cands, cis, valids = [], [], []
        for kind, fixed, lo, hi in pieces:
            rows = SUBLANES * ((hi + SUBLANES - 1) // SUBLANES)
            ridx = lax.broadcasted_iota(jnp.int32, (rows, 1), 0)
            valid = (ridx >= lo) & (ridx < hi)
            if kind == "row":
                vsum = v1[fixed:fixed + 1] + v2[0:rows]
                ci = fixed * PEER_TOPK + ridx
            else:
                vsum = v1[0:rows] + v2[fixed:fixed + 1]
                ci = ridx * PEER_TOPK + fixed
            cands.append(jnp.where(valid, vsum, NEG_INF))
            cis.append(jnp.broadcast_to(jnp.where(valid, ci, big), (rows, tm)))
            valids.append(jnp.broadcast_to(valid, (rows, tm)))
        cand = jnp.concatenate(cands, axis=0)
        ci = jnp.concatenate(cis, axis=0)
        valid = jnp.concatenate(valids, axis=0)
        off = _select_pairs(cand, ci, valid, sel_ref, break_ties=False)

        @pl.when(off > 0.0)
        def _():
            _select_pairs(cand, ci, valid, sel_ref, break_ties=True)

        sel = sel_ref[...]
        zsum = jnp.sum(sel * jnp.exp(cand - (v1[0:1] + v2[0:1])), axis=0, keepdims=True)
        aidx = lax.broadcasted_iota(jnp.int32, (PEER_TOPK, 1), 0)
        per_a = jnp.zeros((PEER_TOPK, tm), F32)
        row0 = 0
        for kind, fixed, lo, hi in pieces:
            rows = SUBLANES * ((hi + SUBLANES - 1) // SUBLANES)
            part = sel[row0:row0 + rows]
            row0 += rows
            if kind == "row":
                per_a = per_a + jnp.where(aidx == fixed, jnp.sum(part, axis=0, keepdims=True), 0.0)
            elif rows == PEER_TOPK:
                per_a = per_a + part
            else:
                per_a = per_a + jnp.concatenate([part, jnp.zeros((PEER_TOPK - rows, tm), F32)], axis=0)
        key = key_ref[...]
        kref = kref_ref[...]
        cnt = jnp.zeros((nk, tm), F32)
        for a in range(PEER_TOPK):
            cnt = jnp.where(key == kref[a:a + 1], per_a[a:a + 1], cnt)
        p1_ref[h] = jnp.exp(s1 - v1[0:1]) / (zsum * 2.0 ** 0.5)
        cnt_ref[h] = cnt
        p2_ref[h] = jnp.exp(s2 - v2[0:1]).astype(BF16)
        rk2_ref[h] = rank2.astype(F32).astype(BF16)
        return carry

    lax.fori_loop(0, PEER_HEADS, head, 0)


def _route(h2, wq, keys, tm=256):
    t, d = h2.shape
    nq = wq.shape[1]
    nk = PEER_NKEYS
    out = lambda dt: jax.ShapeDtypeStruct((PEER_HEADS, nk, t), dt)
    ospec = pl.BlockSpec((PEER_HEADS, nk, tm), lambda i: (0, 0, i))
    return pl.pallas_call(
        _route_kernel,
        grid=(t // tm,),
        in_specs=[pl.BlockSpec((tm, d), lambda i: (i, 0)),
                  pl.BlockSpec((d, nq), lambda i: (0, 0)),
                  pl.BlockSpec((2 * PEER_HEADS, nk, nk), lambda i: (0, 0, 0))],
        out_specs=[ospec] * 4,
        out_shape=[out(F32), out(F32), out(BF16), out(BF16)],
        scratch_shapes=[pltpu.VMEM((2 * PEER_HEADS, tm, nk), BF16),
                        pltpu.VMEM((tm // LANES, PEER_TOPK, LANES), F32),
                        pltpu.VMEM((tm // LANES, PEER_TOPK, LANES), F32),
                        pltpu.VMEM((2 * tm // LANES, nk, LANES), jnp.int32),
                        pltpu.VMEM((sum(SUBLANES * ((hi + SUBLANES - 1) // SUBLANES)
                                        for _, _, _, hi in _pair_candidates()), tm), F32),
                        pltpu.VMEM((nk, tm), F32), pltpu.VMEM((PEER_TOPK, tm), F32)],
        compiler_params=_params("parallel"),
        name="peer_route",
    )(h2, wq, keys)


def _expert_kernel(n_eblk, h_ref, u_ref, vt_ref, p1_ref, cnt_ref, p2_ref, rk2_ref, x1_ref, mod_ref,
                   o_ref, acc_ref, act_ref, gate_ref):
    j = pl.program_id(1)
    te = u_ref.shape[0]
    tm = h_ref.shape[0]
    nk = PEER_NKEYS
    kb_rows = MXU_WIDTH
    per_kb = kb_rows // nk
    tile = 2 * SUBLANES

    @pl.when(j == 0)
    def _():
        acc_ref[...] = jnp.zeros_like(acc_ref)

    for sb in range(te // nk):
        e1 = j * (te // nk) + sb
        gate = None
        for h in range(PEER_HEADS):
            c = jnp.broadcast_to(cnt_ref[h, pl.ds(e1, 1), :], (tile, tm)).astype(BF16)
            p = jnp.broadcast_to(p1_ref[h, pl.ds(e1, 1), :], (tile, tm)).astype(BF16)
            rk2 = rk2_ref[h].reshape(nk // tile, tile, tm)
            p2 = p2_ref[h].reshape(nk // tile, tile, tm)
            term = jnp.where(rk2 < c[None], p2, jnp.zeros_like(p2)) * p[None]
            gate = term if gate is None else gate + term
        gate_ref[sb * nk:(sb + 1) * nk, :] = gate.reshape(nk, tm)
    hb = h_ref[...]
    for kb in range(te // kb_rows):
        rows = slice(kb * kb_rows, (kb + 1) * kb_rows)
        act_ref[rows, :] = _dot_nt(u_ref[rows, :], hb)
    acc = None
    for kb in range(te // kb_rows):
        rows = slice(kb * kb_rows, (kb + 1) * kb_rows)
        t = act_ref[rows, :]
        w = gate_ref[rows, :] * (t + t * lax.erf(t)).astype(BF16)
        part = _dot(vt_ref[:, rows], w)
        acc = part if acc is None else acc + part
    acc_ref[...] += acc

    @pl.when(j == n_eblk - 1)
    def _():
        o_ref[...] = x1_ref[...] + mod_ref[0][5:6] * acc_ref[...].T


def _experts(h2, u, vt, p1, cnt, p2, rk2, x1, mod3, seq, tm=512, te=1024):
    t, d = h2.shape
    ne = u.shape[0]
    nk = PEER_NKEYS
    n_eblk = ne // te
    per_b = seq // tm
    rspec = pl.BlockSpec((PEER_HEADS, nk, tm), lambda i, j: (0, 0, i))
    return pl.pallas_call(
        functools.partial(_expert_kernel, n_eblk),
        grid=(t // tm, n_eblk),
        in_specs=[pl.BlockSpec((tm, d), lambda i, j: (i, 0)),
                  pl.BlockSpec((te, d), lambda i, j: (j, 0)),
                  pl.BlockSpec((d, te), lambda i, j: (0, j)),
                  rspec, rspec, rspec, rspec,
                  pl.BlockSpec((tm, d), lambda i, j: (i, 0)),
                  pl.BlockSpec((1, 6, d), lambda i, j: (i // per_b, 0, 0))],
        out_specs=pl.BlockSpec((tm, d), lambda i, j: (i, 0)),
        out_shape=jax.ShapeDtypeStruct((t, d), F32),
        scratch_shapes=[pltpu.VMEM((d, tm), F32), pltpu.VMEM((te, tm), F32), pltpu.VMEM((te, tm), BF16)],
        compiler_params=_params("parallel", "arbitrary"),
        name="peer_experts",
    )(h2, u, vt, p1, cnt, p2, rk2, x1, mod3)


def kernel(x, c, ada_w, ada_b, norm1_g, w_in, da_qnorm_g, da_knorm_g, da_lambda, da_subln_g, rw_shift_mu, rw_w0,
           rw_w2, rw_a0, rw_a2, rw_g2, rw_k_k, rw_k_a, rw_r_k, rw_ln_g, rw_ln_b, w_out, norm2_g, peer_wq,
           peer_keys, peer_u, peer_v):
    bsz, seq, d = x.shape
    depth = ada_w.shape[0]
    t = bsz * seq
    for l in range(depth):
        mod3 = _ada_mod(c, ada_w[l], ada_b[l]).reshape(bsz, 6, d)
        x2 = x.reshape(t, d)
        w_l = w_in[l].astype(BF16)
        p_att, p_rw = _in_proj(x2, mod3, norm1_g[l], w_l[:, :ATT_COLS], w_l[:, ATT_COLS:], seq)
        att = _diff_attention(p_att.reshape(bsz, seq, ATT_COLS), da_qnorm_g[l], da_knorm_g[l], da_lambda[l],
                              da_subln_g[l])
        ra, y0, g, h, bonus, gate = _rwkv_chunks(p_rw.reshape(bsz, seq, RWKV_COLS), rw_shift_mu[l], rw_w0[l],
                                                 rw_w2[l], rw_a0[l], rw_a2[l], rw_g2[l], rw_k_k[l], rw_k_a[l],
                                                 rw_r_k[l].reshape(-1))
        y_f, y_b = _rwkv_scan(ra, y0, g, h)
        wo = w_out[l].astype(BF16)
        aw = att.shape[-1]
        flat = lambda a: a.reshape(t, RW_WIDTH)
        x1, h2 = _out_proj(att.reshape(t, aw), flat(y_f), flat(y_b), flat(bonus), flat(gate), rw_ln_g[l],
                           rw_ln_b[l], x2, mod3, wo[:aw], wo[aw:], norm2_g[l], seq)
        keys = peer_keys[l].reshape(2 * PEER_HEADS, PEER_NKEYS, -1).astype(BF16)
        p1, cnt, p2, rk2 = _route(h2, peer_wq[l].astype(BF16), keys)
        u_scaled = (peer_u[l] * 2.0 ** -0.5).astype(BF16)
        x = _experts(h2, u_scaled, peer_v[l].T.astype(BF16), p1, cnt, p2, rk2, x1, mod3, seq).reshape(bsz, seq, d)
    return x
```

```python
import functools
import math

import jax
import jax.numpy as jnp
from jax import lax
from jax.experimental import pallas as pl
from jax.experimental.pallas import tpu as pltpu

F32 = jnp.float32
BF16 = jnp.bfloat16
HIGHEST = lax.Precision.HIGHEST

LANES = 128
SUBLANES = 8
MXU_WIDTH = 256
VMEM_LIMIT_BYTES = 56 * 1024 * 1024

DA_HEAD_DIM = 64
DA_HEADS = 4
RW_HEAD = 64
RW_HEADS = 8
RW_PAIRS = RW_HEADS // 2
RW_WIDTH = RW_HEADS * RW_HEAD
LORA_COLS = 128
RWKV_COLS = 3 * RW_WIDTH + 3 * LORA_COLS
ATT_COLS = 3 * 2 * DA_HEAD_DIM * DA_HEADS
GN_EPS = 64e-5
NORM_EPS = 1e-6
LAMBDA_INIT = 0.8 - 0.6 * math.exp(-0.3 * 0)
PEER_HEADS = 8
PEER_NKEYS = 128
PEER_TOPK = 16
RW_CHUNK = 64
NEG_INF = float("-inf")


def _params(*sem):
    return pltpu.CompilerParams(dimension_semantics=sem, vmem_limit_bytes=VMEM_LIMIT_BYTES)


def _dot(a, b, precision=None):
    return jnp.dot(a, b, preferred_element_type=F32, precision=precision)


def _dot_nt(a, b, precision=None):
    return lax.dot_general(a, b, (((1,), (1,)), ((), ())), preferred_element_type=F32, precision=precision)


def _dot_tn(a, b, precision=None):
    return lax.dot_general(a, b, (((0,), (0,)), ((), ())), preferred_element_type=F32, precision=precision)


def _group_matrix(n, group, value):
    shift = group.bit_length() - 1
    r = lax.broadcasted_iota(jnp.int32, (n, n), 0) >> shift
    c = lax.broadcasted_iota(jnp.int32, (n, n), 1) >> shift
    return jnp.where(r == c, value, 0.0).astype(F32)


def _ada_kernel(c_ref, w_ref, b_ref, o_ref):
    c = c_ref[...]
    s = c * jax.nn.sigmoid(c)
    o_ref[...] = _dot(s, w_ref[...], HIGHEST) + b_ref[...]


def _ada_mod(c, w, b):
    bsz, d = c.shape
    n = w.shape[1]
    tn = 1024
    return pl.pallas_call(
        _ada_kernel,
        grid=(n // tn,),
        in_specs=[pl.BlockSpec((bsz, d), lambda j: (0, 0)),
                  pl.BlockSpec((d, tn), lambda j: (0, j)),
                  pl.BlockSpec((1, tn), lambda j: (0, j))],
        out_specs=pl.BlockSpec((bsz, tn), lambda j: (0, j)),
        out_shape=jax.ShapeDtypeStruct((bsz, n), F32),
        compiler_params=_params("arbitrary"),
        name="ada_mod",
    )(c, w, b.reshape(1, n))


def _inproj_kernel(x_ref, mod_ref, g_ref, wa_ref, wr_ref, oa_ref, or_ref):
    x = x_ref[...]
    ms = jnp.mean(x * x, axis=-1, keepdims=True)
    y = x * lax.rsqrt(ms + NORM_EPS) * g_ref[...]
    m = mod_ref[0]
    h = (y * (1.0 + m[1:2]) + m[0:1]).astype(BF16)
    oa_ref[...] = _dot(h, wa_ref[...])
    or_ref[...] = _dot(h, wr_ref[...])


def _in_proj(x2, mod3, g, w_att, w_rw, seq, tm=256):
    t, d = x2.shape
    na, nr = w_att.shape[1], w_rw.shape[1]
    per_b = seq // tm
    return pl.pallas_call(
        _inproj_kernel,
        grid=(t // tm,),
        in_specs=[pl.BlockSpec((tm, d), lambda i: (i, 0)),
                  pl.BlockSpec((1, 6, d), lambda i: (i // per_b, 0, 0)),
                  pl.BlockSpec((1, d), lambda i: (0, 0)),
                  pl.BlockSpec((d, na), lambda i: (0, 0)),
                  pl.BlockSpec((d, nr), lambda i: (0, 0))],
        out_specs=[pl.BlockSpec((tm, na), lambda i: (i, 0)),
                   pl.BlockSpec((tm, nr), lambda i: (i, 0))],
        out_shape=[jax.ShapeDtypeStruct((t, na), F32), jax.ShapeDtypeStruct((t, nr), F32)],
        compiler_params=_params("arbitrary"),
        name="in_proj",
    )(x2, mod3, g.reshape(1, d), w_att, w_rw)


def _attn_kernel(tq, q_ref, k_ref, v_ref, qg_ref, kg_ref, lam_ref, slope_ref, sg_ref, o_ref, qs, ks, vs, bias_ref):
    seq = q_ref.shape[1]
    width = 2 * DA_HEAD_DIM
    avg = _group_matrix(width, DA_HEAD_DIM, 1.0 / DA_HEAD_DIM)

    def qk_norm(x, g):
        ms = _dot(x * x, avg)
        return x * lax.rsqrt(ms + NORM_EPS) * g

    qs[...] = (qk_norm(q_ref[0], qg_ref[...]) * (DA_HEAD_DIM ** -0.5)).astype(BF16)
    ks[...] = qk_norm(k_ref[0], kg_ref[...]).astype(BF16)
    vs[...] = v_ref[0].astype(BF16)

    lam = lam_ref[...]
    lam_full = (jnp.exp(jnp.sum(lam[0:1] * lam[1:2], axis=-1, keepdims=True))
                - jnp.exp(jnp.sum(lam[2:3] * lam[3:4], axis=-1, keepdims=True)) + LAMBDA_INIT)
    slope = slope_ref[0][:, 0:1]
    first = lax.broadcasted_iota(jnp.int32, (1, width), 1) < DA_HEAD_DIM
    sg = sg_ref[...] * (1.0 - LAMBDA_INIT)

    n_blk = seq // tq
    dist = lax.broadcasted_iota(jnp.int32, (tq, tq), 0) - lax.broadcasted_iota(jnp.int32, (tq, tq), 1)
    for d in range(2 * n_blk - 1):
        bias_ref[d] = slope * jnp.abs(dist - (d - (n_blk - 1)) * tq).astype(F32)

    def body(i, carry):
        r0 = pl.multiple_of(i * tq, tq)
        qt = qs[pl.ds(r0, tq), :]
        q0 = jnp.where(first, qt, jnp.zeros_like(qt))
        q1 = jnp.where(first, jnp.zeros_like(qt), qt)
        kk = ks[...]
        bias = jnp.concatenate([bias_ref[j - i + (n_blk - 1)] for j in range(n_blk)], axis=1)
        s0 = _dot_nt(q0, kk) - bias
        s1 = _dot_nt(q1, kk) - bias
        p0 = jnp.exp(s0 - jnp.max(s0, axis=-1, keepdims=True))
        p1 = jnp.exp(s1 - jnp.max(s1, axis=-1, keepdims=True))
        w0 = 1.0 / jnp.sum(p0, axis=-1, keepdims=True)
        w1 = lam_full / jnp.sum(p1, axis=-1, keepdims=True)
        o = _dot(p0.astype(BF16), vs[...]) * w0 - _dot(p1.astype(BF16), vs[...]) * w1
        ms = jnp.mean(o * o, axis=-1, keepdims=True)
        o_ref[0, pl.ds(r0, tq), :] = o * lax.rsqrt(ms + NORM_EPS) * sg
        return carry

    lax.fori_loop(0, seq // tq, body, 0)


def _diff_attention(p_att3, qn_g, kn_g, lam, subln_g, tq=256):
    bsz, seq, _ = p_att3.shape
    width = 2 * DA_HEAD_DIM
    slopes = jnp.asarray([2.0 ** (-8.0 * (h + 1) / DA_HEADS) for h in range(DA_HEADS)], F32)
    slopes = jnp.broadcast_to(slopes[:, None, None], (DA_HEADS, 1, width))
    blk = lambda off: pl.BlockSpec((1, seq, width), lambda b, h: (b, 0, off + h))
    full = lambda shape: pl.BlockSpec(shape, lambda b, h: (0,) * len(shape))
    return pl.pallas_call(
        functools.partial(_attn_kernel, tq),
        grid=(bsz, DA_HEADS),
        in_specs=[blk(0), blk(DA_HEADS), blk(2 * DA_HEADS),
                  full((1, width)), full((1, width)), full((4, DA_HEAD_DIM)),
                  pl.BlockSpec((1, 1, width), lambda b, h: (h, 0, 0)),
                  full((1, width))],
        out_specs=pl.BlockSpec((1, seq, width), lambda b, h: (b, 0, h)),
        out_shape=jax.ShapeDtypeStruct((bsz, seq, DA_HEADS * width), F32),
        scratch_shapes=[pltpu.VMEM((seq, width), BF16)] * 3 + [pltpu.VMEM((2 * (seq // tq) - 1, tq, tq), F32)],
        compiler_params=_params("arbitrary", "arbitrary"),
        name="diff_attention",
    )(p_att3, p_att3, p_att3,
      jnp.tile(qn_g.reshape(1, DA_HEAD_DIM), (1, 2)), jnp.tile(kn_g.reshape(1, DA_HEAD_DIM), (1, 2)),
      lam, slopes, subln_g.reshape(1, width))


def _rwkv_chunk_kernel(n_chunks, cur_ref, prev_ref, next_ref, mu_ref, w0_ref, w2_ref, a0_ref, a2_ref, g2_ref,
                       kk_ref, ka_ref, rk_ref, hs_ref, ra_ref, y0_ref, g_ref, h_ref, bonus_ref, gate_ref):
    ch = RW_CHUNK
    w = RW_WIDTH
    pw = 2 * RW_HEAD
    chunk = pl.program_id(1)

    pc = cur_ref[0]
    row = lax.broadcasted_iota(jnp.int32, (ch, 1), 0)
    pv = prev_ref[0][SUBLANES - 1:SUBLANES, :] * (chunk > 0).astype(F32)
    nx = next_ref[0][0:1, :] * (chunk < n_chunks - 1).astype(F32)
    prev = jnp.where(row == 0, pv, pltpu.roll(pc, 1, 0))
    nxt = jnp.where(row == ch - 1, nx, pltpu.roll(pc, ch - 1, 0))
    mu = mu_ref[...]
    ps = pc + mu[0:1] * (prev - pc) + mu[1:2] * (nxt - pc)
    r, k, v = ps[:, 0:w], ps[:, w:2 * w], ps[:, 2 * w:3 * w]
    wd = ps[:, 3 * w:3 * w + LORA_COLS]
    ad = ps[:, 3 * w + LORA_COLS:3 * w + 2 * LORA_COLS]
    gd = ps[:, 3 * w + 2 * LORA_COLS:3 * w + 3 * LORA_COLS]

    head_sum = hs_ref[...]

    def per_head_sum(x):
        return _dot(x.astype(BF16), head_sum)

    kkr = k * kk_ref[...]
    kk = kkr / jnp.maximum(jnp.sqrt(per_head_sum(kkr * kkr)), 1e-12)
    bonus_ref[0] = per_head_sum(r * k * rk_ref[...]) * v
    gate_ref[0] = _dot(jax.nn.sigmoid(gd), g2_ref[...])

    ti = lax.broadcasted_iota(jnp.int32, (ch, ch), 0)
    tj = lax.broadcasted_iota(jnp.int32, (ch, ch), 1)
    eye = (ti == tj).astype(F32)
    lane = lax.broadcasted_iota(jnp.int32, (1, pw), 1)
    head_lanes = (lane < RW_HEAD, lane >= RW_HEAD)
    pi = lax.broadcasted_iota(jnp.int32, (pw, pw), 0)
    pj = lax.broadcasted_iota(jnp.int32, (pw, pw), 1)
    same_head = (pi < RW_HEAD) == (pj < RW_HEAD)
    eye_pair = (pi == pj).astype(F32)
    v16 = v.astype(BF16)
    tanh_wd = jnp.tanh(wd)

    probs = []
    pairs = []
    for d in range(2):
        z = w0_ref[d:d + 1] + _dot(tanh_wd, w2_ref[d])
        logdec = -jax.nn.sigmoid(z) * math.exp(-0.5)
        a = jax.nn.sigmoid(a0_ref[d:d + 1] + _dot(ad, a2_ref[d]))
        kd = k * (1.0 + (a - 1.0) * ka_ref[...])
        before = (tj < ti) if d == 0 else (tj > ti)
        upto = before | (tj == ti)
        cum = _dot(upto.astype(F32), logdec, HIGHEST)
        total = cum[ch - 1:ch] if d == 0 else cum[0:1]
        w_in, w_ex, w_inv, w_tot = jnp.exp(cum), jnp.exp(cum - logdec), jnp.exp(-cum), jnp.exp(total)
        a_bar = -kk * w_ex
        b_bar = (kk * a * w_inv).astype(BF16)
        k_bar = (kd * w_inv).astype(BF16)
        r_bar = r * (w_in if d == 0 else w_ex)
        ymask = upto if d == 0 else before
        later, earlier = (ti, tj) if d == 0 else (tj, ti)
        off_masks = []
        for lvl in range(int(math.log2(ch))):
            same_pair = (ti >> (lvl + 1)) == (tj >> (lvl + 1))
            off_masks.append(same_pair & (((later >> lvl) & 1) == 1) & (((earlier >> lvl) & 1) == 0))
        for p in range(RW_PAIRS):
            sl = slice(p * pw, (p + 1) * pw)
            pairs.append(dict(d=d, p=p, b=b_bar[:, sl], k=k_bar[:, sl], v=v16[:, sl], r=r_bar[:, sl],
                              w_tot=w_tot[:, sl]))
            for hh in range(2):
                mh = head_lanes[hh]
                probs.append(dict(pair=len(pairs) - 1, before=before, ymask=ymask, off=off_masks,
                                  a=jnp.where(mh, a_bar[:, sl], 0.0).astype(BF16),
                                  r=jnp.where(mh, r_bar[:, sl], 0.0).astype(BF16),
                                  v=jnp.where(mh, v16[:, sl], jnp.zeros_like(v16[:, sl]))))

    for q in probs:
        pr = pairs[q["pair"]]
        q["ab"] = _dot_nt(q["a"], pr["b"])
        ak = _dot_nt(q["a"], pr["k"])
        q["rb"] = jnp.where(q["ymask"], _dot_nt(q["r"], pr["b"]), 0.0).astype(BF16)
        rk = _dot_nt(q["r"], pr["k"])
        q["ak"] = jnp.where(q["before"], ak, 0.0).astype(BF16)
        q["rk"] = jnp.where(q["ymask"], rk, 0.0).astype(BF16)
    for q in probs:
        q["akv"] = _dot(q["ak"], q["v"])
        q["rkv"] = _dot(q["rk"], q["v"])
        q["inv"] = eye + jnp.where(q["off"][0], q["ab"], 0.0)
    for lvl in range(1, int(math.log2(ch))):
        for q in probs:
            q["tmp"] = _dot(q["inv"].astype(BF16), jnp.where(q["off"][lvl], q["ab"], 0.0).astype(BF16))
        for q in probs:
            q["inv"] = q["inv"] + _dot(q["tmp"].astype(BF16), q["inv"].astype(BF16))
    for q in probs:
        rhs = jnp.concatenate([q["a"], q["akv"].astype(BF16)], axis=1)
        q["x"] = _dot(q["inv"].astype(BF16), rhs)
    for q in probs:
        q["z"] = _dot(q["rb"], q["x"].astype(BF16))
    for i, pr in enumerate(pairs):
        q0, q1 = probs[2 * i], probs[2 * i + 1]
        x = q0["x"] + q1["x"]
        z = q0["z"] + q1["z"]
        ra = pr["r"] + z[:, :pw]
        y0 = z[:, pw:] + q0["rkv"] + q1["rkv"]
        xb = _dot_tn(x.astype(BF16), pr["b"])
        vk = _dot_tn(pr["v"], pr["k"])
        g = (eye_pair + jnp.where(same_head, xb[:pw], 0.0)) * pr["w_tot"]
        h = jnp.where(same_head, xb[pw:] + vk, 0.0) * pr["w_tot"]
        d, p = pr["d"], pr["p"]
        sl = slice(p * pw, (p + 1) * pw)
        ra_ref[0, d, :, sl] = ra.astype(BF16)
        y0_ref[0, d, :, sl] = y0.astype(BF16)
        g_ref[0, d, 0, :, sl] = g.astype(BF16)
        h_ref[0, d, 0, :, sl] = h.astype(BF16)


def _pad_lora(w2):
    keep = jnp.arange(2)[:, None, None, None] == jnp.arange(2)[None, :, None, None]
    return jnp.where(keep, w2[None], 0.0).reshape(2, -1, w2.shape[-1])


def _rwkv_chunks(p_rw3, mu, w0, w2, a0, a2, g2, k_k, k_a, r_k):
    bsz, seq, cols = p_rw3.shape
    ch, w, pw = RW_CHUNK, RW_WIDTH, 2 * RW_HEAD
    n_chunks = seq // ch
    sub_per_chunk = ch // SUBLANES
    n_sub = seq // SUBLANES
    full = lambda shape: pl.BlockSpec(shape, lambda b, c: (0,) * len(shape))
    row = lambda x: x.reshape(1, w)
    head_sum = (jnp.arange(w)[:, None] // RW_HEAD == jnp.arange(w)[None, :] // RW_HEAD).astype(BF16)
    tok = lambda dt: jax.ShapeDtypeStruct((bsz, 2, seq, w), dt)
    mat = lambda dt: jax.ShapeDtypeStruct((bsz, 2, n_chunks, pw, w), dt)
    tok_spec = pl.BlockSpec((1, 2, ch, w), lambda b, c: (b, 0, c, 0))
    mat_spec = pl.BlockSpec((1, 2, 1, pw, w), lambda b, c: (b, 0, c, 0, 0))
    one_spec = pl.BlockSpec((1, ch, w), lambda b, c: (b, c, 0))
    return pl.pallas_call(
        functools.partial(_rwkv_chunk_kernel, n_chunks),
        grid=(bsz, n_chunks),
        in_specs=[pl.BlockSpec((1, ch, cols), lambda b, c: (b, c, 0)),
                  pl.BlockSpec((1, SUBLANES, cols), lambda b, c: (b, jnp.maximum(c * sub_per_chunk - 1, 0), 0)),
                  pl.BlockSpec((1, SUBLANES, cols),
                               lambda b, c: (b, jnp.minimum((c + 1) * sub_per_chunk, n_sub - 1), 0)),
                  full((2, cols)), full((2, w)), full((2, LORA_COLS, w)), full((2, w)), full((2, LORA_COLS, w)),
                  full((LORA_COLS, w)), full((1, w)), full((1, w)), full((1, w)), full((w, w))],
        out_specs=[tok_spec, tok_spec, mat_spec, mat_spec, one_spec, one_spec],
        out_shape=[tok(BF16), tok(BF16), mat(BF16), mat(BF16),
                   jax.ShapeDtypeStruct((bsz, seq, w), F32), jax.ShapeDtypeStruct((bsz, seq, w), F32)],
        compiler_params=_params("parallel", "parallel"),
        name="rwkv7_chunks",
    )(p_rw3, p_rw3, p_rw3, mu, w0, _pad_lora(w2), a0, _pad_lora(a2), g2, row(k_k), row(k_a), row(r_k), head_sum)


def _rwkv_scan_kernel(ra0_ref, ra1_ref, y00_ref, y01_ref, g0_ref, g1_ref, h0_ref, h1_ref, o0_ref, o1_ref, st_ref):
    pw = 2 * RW_HEAD

    @pl.when(pl.program_id(1) == 0)
    def _():
        st_ref[...] = jnp.zeros_like(st_ref)

    ch = RW_CHUNK
    n_sub = g0_ref.shape[2]
    dirs = ((ra0_ref, y00_ref, g0_ref, h0_ref, o0_ref), (ra1_ref, y01_ref, g1_ref, h1_ref, o1_ref))
    states = {(d, p): st_ref[d, p] for d in range(2) for p in range(RW_PAIRS)}
    for step in range(n_sub):
        for d, (ra_ref, y0_ref, g_ref, h_ref, o_ref) in enumerate(dirs):
            c = step if d == 0 else n_sub - 1 - step
            rows = slice(c * ch, (c + 1) * ch)
            for p in range(RW_PAIRS):
                sl = slice(p * pw, (p + 1) * pw)
                s16 = states[d, p].astype(BF16)
                o_ref[0, rows, sl] = _dot_nt(ra_ref[0, 0, rows, sl], s16) + y0_ref[0, 0, rows, sl]
                states[d, p] = _dot(s16, g_ref[0, 0, c, :, sl]) + h_ref[0, 0, c, :, sl]
    for (d, p), s in states.items():
        st_ref[d, p] = s


def _rwkv_scan(ra, y0, g, h, n_sub=4):
    bsz, _, seq, w = ra.shape
    ch, pw = RW_CHUNK, 2 * RW_HEAD
    n_blk = seq // (ch * n_sub)
    blk = lambda d, c: c if d == 0 else n_blk - 1 - c
    tok = lambda d: pl.BlockSpec((1, 1, n_sub * ch, w), lambda b, c: (b, d, blk(d, c), 0))
    mat = lambda d: pl.BlockSpec((1, 1, n_sub, pw, w), lambda b, c: (b, d, blk(d, c), 0, 0))
    out = lambda d: pl.BlockSpec((1, n_sub * ch, w), lambda b, c: (b, blk(d, c), 0))
    return pl.pallas_call(
        _rwkv_scan_kernel,
        grid=(bsz, n_blk),
        in_specs=[tok(0), tok(1), tok(0), tok(1), mat(0), mat(1), mat(0), mat(1)],
        out_specs=[out(0), out(1)],
        out_shape=[jax.ShapeDtypeStruct((bsz, seq, w), F32)] * 2,
        scratch_shapes=[pltpu.VMEM((2, RW_PAIRS, pw, pw), F32)],
        compiler_params=_params("parallel", "arbitrary"),
        name="rwkv7_scan",
    )(ra, ra, y0, y0, g, g, h, h)


def _outproj_kernel(att_ref, yf_ref, yb_ref, bonus_ref, gate_ref, lng_ref, lnb_ref, hs_ref, x_ref, mod_ref,
                    wa_ref, wr_ref, g_ref, x1_ref, h2_ref):
    head_sum = hs_ref[...]

    def per_head_mean(v):
        return _dot(v.astype(BF16), head_sum) * (1.0 / RW_HEAD)

    y = yf_ref[...] + yb_ref[...]
    yc = y - per_head_mean(y)
    yn = yc * lax.rsqrt(per_head_mean(yc * yc) + GN_EPS) * lng_ref[...] + lnb_ref[...]
    rw = (yn + bonus_ref[...]) * gate_ref[...]
    acc = _dot(att_ref[...].astype(BF16), wa_ref[...]) + _dot(rw.astype(BF16), wr_ref[...])
    m = mod_ref[0]
    x1 = x_ref[...] + m[2:3] * acc
    x1_ref[...] = x1
    ms = jnp.mean(x1 * x1, axis=-1, keepdims=True)
    yo = x1 * lax.rsqrt(ms + NORM_EPS) * g_ref[...]
    h2_ref[...] = (yo * (1.0 + m[4:5]) + m[3:4]).astype(BF16)


def _out_proj(att2, y_f, y_b, bonus, gate, ln_g, ln_b, x2, mod3, w_a, w_r, g, seq, tm=256):
    t, d = x2.shape
    ka, kr = att2.shape[1], y_f.shape[1]
    per_b = seq // tm
    rows = lambda n: pl.BlockSpec((tm, n), lambda i: (i, 0))
    full = lambda shape: pl.BlockSpec(shape, lambda i: (0,) * len(shape))
    head_sum = (jnp.arange(kr)[:, None] // RW_HEAD == jnp.arange(kr)[None, :] // RW_HEAD).astype(BF16)
    return pl.pallas_call(
        _outproj_kernel,
        grid=(t // tm,),
        in_specs=[rows(ka), rows(kr), rows(kr), rows(kr), rows(kr), full((1, kr)), full((1, kr)), full((kr, kr)),
                  rows(d), pl.BlockSpec((1, 6, d), lambda i: (i // per_b, 0, 0)),
                  full((ka, d)), full((kr, d)), full((1, d))],
        out_specs=[rows(d), rows(d)],
        out_shape=[jax.ShapeDtypeStruct((t, d), F32), jax.ShapeDtypeStruct((t, d), BF16)],
        compiler_params=_params("arbitrary"),
        name="out_proj",
    )(att2, y_f, y_b, bonus, gate, ln_g.reshape(1, kr), ln_b.reshape(1, kr), head_sum, x2, mod3, w_a, w_r,
      g.reshape(1, d))


def _topk_ranks(scores, vals_refs, rank_refs, break_ties):
    nk, tm = scores[0].shape
    kiota = lax.broadcasted_iota(jnp.int32, (nk, tm), 0)

    def body(j, carry):
        out = []
        for (cur, rank), vals_ref in zip(carry, vals_refs):
            m = jnp.max(cur, axis=0, keepdims=True)
            hit = cur == m
            if break_ties:
                hit = kiota == jnp.min(jnp.where(hit, kiota, nk), axis=0, keepdims=True)
            vals_ref[pl.ds(j, 1), :] = m
            out.append((jnp.where(hit, NEG_INF, cur), None if rank is None else jnp.where(hit, j, rank)))
        return tuple(out)

    init = tuple((s, None if ref is None else jnp.full((nk, tm), PEER_TOPK, jnp.int32))
                 for s, ref in zip(scores, rank_refs))
    excess = jnp.zeros((1, tm), F32)
    for (cur, rank), rank_ref in zip(lax.fori_loop(0, PEER_TOPK, body, init), rank_refs):
        if rank_ref is not None:
            rank_ref[...] = rank
        removed = jnp.sum(jnp.where(cur == NEG_INF, 1.0, 0.0), axis=0, keepdims=True)
        excess = jnp.maximum(excess, jnp.abs(removed - PEER_TOPK))
    return jnp.max(excess)


def _pair_candidates():
    pieces = []
    for lvl in range(4):
        cnt = PEER_TOPK // (lvl + 1)
        pieces.append(("row", lvl, lvl, cnt))
        if lvl + 1 < cnt:
            pieces.append(("col", lvl, lvl + 1, cnt))
    return pieces


def _select_pairs(cand, ci, valid, sel_ref, break_ties):
    big = 4 * PEER_TOPK * PEER_TOPK

    def body(j, cur):
        m = jnp.max(cur, axis=0, keepdims=True)
        hit = cur == m
        if break_ties:
            hit = ci == jnp.min(jnp.where(hit, ci, big), axis=0, keepdims=True)
        return jnp.where(hit, NEG_INF, cur)

    cur = lax.fori_loop(0, PEER_TOPK, body, cand)
    sel = jnp.where((cur == NEG_INF) & valid, 1.0, 0.0)
    sel_ref[...] = sel
    return jnp.max(jnp.abs(jnp.sum(sel, axis=0, keepdims=True) - PEER_TOPK))


def _route_kernel(h_ref, wq_ref, keys_ref, p1_ref, cnt_ref, p2_ref, rk2_ref, q_scr, v1_ref, v2_ref, rank_ref,
                  sel_ref, key_ref, kref_ref):
    tm = h_ref.shape[0]
    nk = PEER_NKEYS
    n_grp = tm // LANES
    q = _dot(h_ref[...], wq_ref[...]).astype(BF16)
    for hp in range(2 * PEER_HEADS):
        q_scr[hp] = q[:, hp * nk:(hp + 1) * nk]

    pieces = _pair_candidates()
    big = 4 * PEER_TOPK * PEER_TOPK
    cat = lambda parts: jnp.concatenate(parts, axis=1)

    def head(h, carry):
        s1 = _dot_nt(keys_ref[2 * h], q_scr[2 * h])
        s2 = _dot_nt(keys_ref[2 * h + 1], q_scr[2 * h + 1])
        probs = [s[:, g * LANES:(g + 1) * LANES] for s in (s1, s2) for g in range(n_grp)]
        vals = [ref.at[g] for ref in (v1_ref, v2_ref) for g in range(n_grp)]
        ranks = [rank_ref.at[i] for i in range(2 * n_grp)]
        excess = 0.0
        for g in range(n_grp):
            excess = jnp.maximum(excess, _topk_ranks([probs[g], probs[n_grp + g]], [vals[g], vals[n_grp + g]],
                                                     [None, ranks[n_grp + g]], break_ties=False))
        v1 = cat([v1_ref[g] for g in range(n_grp)])
        key_ref[...] = s1
        kref_ref[...] = v1

        @pl.when(excess > 0.0)
        def _():
            _topk_ranks(probs, vals, ranks, break_ties=True)
            key_ref[...] = cat([rank_ref[g] for g in range(n_grp)]).astype(F32)
            kref_ref[...] = jnp.broadcast_to(lax.broadcasted_iota(jnp.int32, (PEER_TOPK, 1), 0).astype(F32),
                                             (PEER_TOPK, tm))

        rank2 = cat([rank_ref[n_grp + g] for g in range(n_grp)])
        v1 = cat([v1_ref[g] for g in range(n_grp)])
        v2 = cat([v2_ref[g] for g in range(n_grp)])

        cands, cis, valids = [], [], []
        for kind, fixed, lo, hi in pieces:
            rows = SUBLANES * ((hi + SUBLANES - 1) // SUBLANES)
            ridx = lax.broadcasted_iota(jnp.int32, (rows, 1), 0)
            valid = (ridx >= lo) & (ridx < hi)
            if kind == "row":
                vsum = v1[fixed:fixed + 1] + v2[0:rows]
                ci = fixed * PEER_TOPK + ridx
            else:
                vsum = v1[0:rows] + v2[fixed:fixed + 1]
                ci = ridx * PEER_TOPK + fixed
            cands.append(jnp.where(valid, vsum, NEG_INF))
            cis.append(jnp.broadcast_to(jnp.where(valid, ci, big), (rows, tm)))
            valids.append(jnp.broadcast_to(valid, (rows, tm)))
        cand = jnp.concatenate(cands, axis=0)
        ci = jnp.concatenate(cis, axis=0)
        valid = jnp.concatenate(valids, axis=0)
        off = _select_pairs(cand, ci, valid, sel_ref, break_ties=False)

        @pl.when(off > 0.0)
        def _():
            _select_pairs(cand, ci, valid, sel_ref, break_ties=True)

        sel = sel_ref[...]
        zsum = jnp.sum(sel * jnp.exp(cand - (v1[0:1] + v2[0:1])), axis=0, keepdims=True)
        aidx = lax.broadcasted_iota(jnp.int32, (PEER_TOPK, 1), 0)
        per_a = jnp.zeros((PEER_TOPK, tm), F32)
        row0 = 0
        for kind, fixed, lo, hi in pieces:
            rows = SUBLANES * ((hi + SUBLANES - 1) // SUBLANES)
            part = sel[row0:row0 + rows]
            row0 += rows
            if kind == "row":
                per_a = per_a + jnp.where(aidx == fixed, jnp.sum(part, axis=0, keepdims=True), 0.0)
            elif rows == PEER_TOPK:
                per_a = per_a + part
            else:
                per_a = per_a + jnp.concatenate([part, jnp.zeros((PEER_TOPK - rows, tm), F32)], axis=0)
        key = key_ref[...]
        kref = kref_ref[...]
        cnt = jnp.zeros((nk, tm), F32)
        for a in range(PEER_TOPK):
            cnt = jnp.where(key == kref[a:a + 1], per_a[a:a + 1], cnt)
        p1_ref[h] = jnp.exp(s1 - v1[0:1]) / (zsum * 2.0 ** 0.5)
        cnt_ref[h] = cnt
        p2_ref[h] = jnp.exp(s2 - v2[0:1]).astype(BF16)
        rk2_ref[h] = rank2.astype(F32).astype(BF16)
        return carry

    lax.fori_loop(0, PEER_HEADS, head, 0)


def _route(h2, wq, keys, tm=512):
    t, d = h2.shape
    nq = wq.shape[1]
    nk = PEER_NKEYS
    out = lambda dt: jax.ShapeDtypeStruct((PEER_HEADS, nk, t), dt)
    ospec = pl.BlockSpec((PEER_HEADS, nk, tm), lambda i: (0, 0, i))
    return pl.pallas_call(
        _route_kernel,
        grid=(t // tm,),
        in_specs=[pl.BlockSpec((tm, d), lambda i: (i, 0)),
                  pl.BlockSpec((d, nq), lambda i: (0, 0)),
                  pl.BlockSpec((2 * PEER_HEADS, nk, nk), lambda i: (0, 0, 0))],
        out_specs=[ospec] * 4,
        out_shape=[out(F32), out(F32), out(BF16), out(BF16)],
        scratch_shapes=[pltpu.VMEM((2 * PEER_HEADS, tm, nk), BF16),
                        pltpu.VMEM((tm // LANES, PEER_TOPK, LANES), F32),
                        pltpu.VMEM((tm // LANES, PEER_TOPK, LANES), F32),
                        pltpu.VMEM((2 * tm // LANES, nk, LANES), jnp.int32),
                        pltpu.VMEM((sum(SUBLANES * ((hi + SUBLANES - 1) // SUBLANES)
                                        for _, _, _, hi in _pair_candidates()), tm), F32),
                        pltpu.VMEM((nk, tm), F32), pltpu.VMEM((PEER_TOPK, tm), F32)],
        compiler_params=_params("parallel"),
        name="peer_route",
    )(h2, wq, keys)


def _expert_kernel(n_eblk, h_ref, u_ref, vt_ref, p1_ref, cnt_ref, p2_ref, rk2_ref, x1_ref, mod_ref,
                   o_ref, acc_ref, act_ref, gate_ref):
    j = pl.program_id(1)
    te = u_ref.shape[0]
    tm = h_ref.shape[0]
    nk = PEER_NKEYS
    kb_rows = MXU_WIDTH
    per_kb = kb_rows // nk
    tile = 2 * SUBLANES

    @pl.when(j == 0)
    def _():
        acc_ref[...] = jnp.zeros_like(acc_ref)

    for sb in range(te // nk):
        e1 = j * (te // nk) + sb
        gate = None
        for h in range(PEER_HEADS):
            c = jnp.broadcast_to(cnt_ref[h, pl.ds(e1, 1), :], (tile, tm)).astype(BF16)
            p = jnp.broadcast_to(p1_ref[h, pl.ds(e1, 1), :], (tile, tm)).astype(BF16)
            rk2 = rk2_ref[h].reshape(nk // tile, tile, tm)
            p2 = p2_ref[h].reshape(nk // tile, tile, tm)
            term = jnp.where(rk2 < c[None], p2, jnp.zeros_like(p2)) * p[None]
            gate = term if gate is None else gate + term
        gate_ref[sb * nk:(sb + 1) * nk, :] = gate.reshape(nk, tm)
    hb = h_ref[...]
    for kb in range(te // kb_rows):
        rows = slice(kb * kb_rows, (kb + 1) * kb_rows)
        act_ref[rows, :] = _dot_nt(u_ref[rows, :], hb)
    acc = None
    for kb in range(te // kb_rows):
        rows = slice(kb * kb_rows, (kb + 1) * kb_rows)
        t = act_ref[rows, :]
        w = gate_ref[rows, :] * (t + t * lax.erf(t)).astype(BF16)
        part = _dot(vt_ref[:, rows], w)
        acc = part if acc is None else acc + part
    acc_ref[...] += acc

    @pl.when(j == n_eblk - 1)
    def _():
        o_ref[...] = x1_ref[...] + mod_ref[0][5:6] * acc_ref[...].T


def _experts(h2, u, vt, p1, cnt, p2, rk2, x1, mod3, seq, tm=512, te=2048):
    t, d = h2.shape
    ne = u.shape[0]
    nk = PEER_NKEYS
    n_eblk = ne // te
    per_b = seq // tm
    rspec = pl.BlockSpec((PEER_HEADS, nk, tm), lambda i, j: (0, 0, i))
    return pl.pallas_call(
        functools.partial(_expert_kernel, n_eblk),
        grid=(t // tm, n_eblk),
        in_specs=[pl.BlockSpec((tm, d), lambda i, j: (i, 0)),
                  pl.BlockSpec((te, d), lambda i, j: (j, 0)),
                  pl.BlockSpec((d, te), lambda i, j: (0, j)),
                  rspec, rspec, rspec, rspec,
                  pl.BlockSpec((tm, d), lambda i, j: (i, 0)),
                  pl.BlockSpec((1, 6, d), lambda i, j: (i // per_b, 0, 0))],
        out_specs=pl.BlockSpec((tm, d), lambda i, j: (i, 0)),
        out_shape=jax.ShapeDtypeStruct((t, d), F32),
        scratch_shapes=[pltpu.VMEM((d, tm), F32), pltpu.VMEM((te, tm), F32), pltpu.VMEM((te, tm), BF16)],
        compiler_params=_params("parallel", "arbitrary"),
        name="peer_experts",
    )(h2, u, vt, p1, cnt, p2, rk2, x1, mod3)


def kernel(x, c, ada_w, ada_b, norm1_g, w_in, da_qnorm_g, da_knorm_g, da_lambda, da_subln_g, rw_shift_mu, rw_w0,
           rw_w2, rw_a0, rw_a2, rw_g2, rw_k_k, rw_k_a, rw_r_k, rw_ln_g, rw_ln_b, w_out, norm2_g, peer_wq,
           peer_keys, peer_u, peer_v):
    bsz, seq, d = x.shape
    depth = ada_w.shape[0]
    t = bsz * seq
    for l in range(depth):
        mod3 = _ada_mod(c, ada_w[l], ada_b[l]).reshape(bsz, 6, d)
        x2 = x.reshape(t, d)
        w_l = w_in[l].astype(BF16)
        p_att, p_rw = _in_proj(x2, mod3, norm1_g[l], w_l[:, :ATT_COLS], w_l[:, ATT_COLS:], seq)
        att = _diff_attention(p_att.reshape(bsz, seq, ATT_COLS), da_qnorm_g[l], da_knorm_g[l], da_lambda[l],
                              da_subln_g[l])
        ra, y0, g, h, bonus, gate = _rwkv_chunks(p_rw.reshape(bsz, seq, RWKV_COLS), rw_shift_mu[l], rw_w0[l],
                                                 rw_w2[l], rw_a0[l], rw_a2[l], rw_g2[l], rw_k_k[l], rw_k_a[l],
                                                 rw_r_k[l].reshape(-1))
        y_f, y_b = _rwkv_scan(ra, y0, g, h)
        wo = w_out[l].astype(BF16)
        aw = att.shape[-1]
        flat = lambda a: a.reshape(t, RW_WIDTH)
        x1, h2 = _out_proj(att.reshape(t, aw), flat(y_f), flat(y_b), flat(bonus), flat(gate), rw_ln_g[l],
                           rw_ln_b[l], x2, mod3, wo[:aw], wo[aw:], norm2_g[l], seq)
        keys = peer_keys[l].reshape(2 * PEER_HEADS, PEER_NKEYS, -1).astype(BF16)
        p1, cnt, p2, rk2 = _route(h2, peer_wq[l].astype(BF16), keys)
        u_scaled = (peer_u[l] * 2.0 ** -0.5).astype(BF16)
        x = _experts(h2, u_scaled, peer_v[l].T.astype(BF16), p1, cnt, p2, rk2, x1, mod3, seq).reshape(bsz, seq, d)
    return x
```

```python
import functools
import math

import jax
import jax.numpy as jnp
from jax import lax
from jax.experimental import pallas as pl
from jax.experimental.pallas import tpu as pltpu

F32 = jnp.float32
BF16 = jnp.bfloat16
HIGHEST = lax.Precision.HIGHEST

LANES = 128
SUBLANES = 8
MXU_WIDTH = 256
VMEM_LIMIT_BYTES = 56 * 1024 * 1024

DA_HEAD_DIM = 64
DA_HEADS = 4
RW_HEAD = 64
RW_HEADS = 8
RW_PAIRS = RW_HEADS // 2
RW_WIDTH = RW_HEADS * RW_HEAD
LORA_COLS = 128
RWKV_COLS = 3 * RW_WIDTH + 3 * LORA_COLS
ATT_COLS = 3 * 2 * DA_HEAD_DIM * DA_HEADS
GN_EPS = 64e-5
NORM_EPS = 1e-6
LAMBDA_INIT = 0.8 - 0.6 * math.exp(-0.3 * 0)
PEER_HEADS = 8
PEER_NKEYS = 128
PEER_TOPK = 16
RW_CHUNK = 64
NEG_INF = float("-inf")


def _params(*sem):
    return pltpu.CompilerParams(dimension_semantics=sem, vmem_limit_bytes=VMEM_LIMIT_BYTES)


def _dot(a, b, precision=None):
    return jnp.dot(a, b, preferred_element_type=F32, precision=precision)


def _dot_nt(a, b, precision=None):
    return lax.dot_general(a, b, (((1,), (1,)), ((), ())), preferred_element_type=F32, precision=precision)


def _dot_tn(a, b, precision=None):
    return lax.dot_general(a, b, (((0,), (0,)), ((), ())), preferred_element_type=F32, precision=precision)


def _group_matrix(n, group, value):
    shift = group.bit_length() - 1
    r = lax.broadcasted_iota(jnp.int32, (n, n), 0) >> shift
    c = lax.broadcasted_iota(jnp.int32, (n, n), 1) >> shift
    return jnp.where(r == c, value, 0.0).astype(F32)


def _ada_kernel(c_ref, w_ref, b_ref, o_ref):
    c = c_ref[...]
    s = c * jax.nn.sigmoid(c)
    o_ref[...] = _dot(s, w_ref[...], HIGHEST) + b_ref[...]


def _ada_mod(c, w, b):
    bsz, d = c.shape
    n = w.shape[1]
    tn = 1024
    return pl.pallas_call(
        _ada_kernel,
        grid=(n // tn,),
        in_specs=[pl.BlockSpec((bsz, d), lambda j: (0, 0)),
                  pl.BlockSpec((d, tn), lambda j: (0, j)),
                  pl.BlockSpec((1, tn), lambda j: (0, j))],
        out_specs=pl.BlockSpec((bsz, tn), lambda j: (0, j)),
        out_shape=jax.ShapeDtypeStruct((bsz, n), F32),
        compiler_params=_params("arbitrary"),
        name="ada_mod",
    )(c, w, b.reshape(1, n))


def _inproj_kernel(x_ref, mod_ref, g_ref, wa_ref, wr_ref, oa_ref, or_ref):
    x = x_ref[...]
    ms = jnp.mean(x * x, axis=-1, keepdims=True)
    y = x * lax.rsqrt(ms + NORM_EPS) * g_ref[...]
    m = mod_ref[0]
    h = (y * (1.0 + m[1:2]) + m[0:1]).astype(BF16)
    oa_ref[...] = _dot(h, wa_ref[...])
    or_ref[...] = _dot(h, wr_ref[...])


def _in_proj(x2, mod3, g, w_att, w_rw, seq, tm=512):
    t, d = x2.shape
    na, nr = w_att.shape[1], w_rw.shape[1]
    per_b = seq // tm
    return pl.pallas_call(
        _inproj_kernel,
        grid=(t // tm,),
        in_specs=[pl.BlockSpec((tm, d), lambda i: (i, 0)),
                  pl.BlockSpec((1, 6, d), lambda i: (i // per_b, 0, 0)),
                  pl.BlockSpec((1, d), lambda i: (0, 0)),
                  pl.BlockSpec((d, na), lambda i: (0, 0)),
                  pl.BlockSpec((d, nr), lambda i: (0, 0))],
        out_specs=[pl.BlockSpec((tm, na), lambda i: (i, 0)),
                   pl.BlockSpec((tm, nr), lambda i: (i, 0))],
        out_shape=[jax.ShapeDtypeStruct((t, na), F32), jax.ShapeDtypeStruct((t, nr), F32)],
        compiler_params=_params("arbitrary"),
        name="in_proj",
    )(x2, mod3, g.reshape(1, d), w_att, w_rw)


def _attn_kernel(tq, q_ref, k_ref, v_ref, qg_ref, kg_ref, lam_ref, slope_ref, sg_ref, o_ref, qs, ks, vs, bias_ref):
    seq = q_ref.shape[1]
    width = 2 * DA_HEAD_DIM
    avg = _group_matrix(width, DA_HEAD_DIM, 1.0 / DA_HEAD_DIM)

    def qk_norm(x, g):
        ms = _dot(x * x, avg)
        return x * lax.rsqrt(ms + NORM_EPS) * g

    qs[...] = (qk_norm(q_ref[0], qg_ref[...]) * (DA_HEAD_DIM ** -0.5)).astype(BF16)
    ks[...] = qk_norm(k_ref[0], kg_ref[...]).astype(BF16)
    vs[...] = v_ref[0].astype(BF16)

    lam = lam_ref[...]
    lam_full = (jnp.exp(jnp.sum(lam[0:1] * lam[1:2], axis=-1, keepdims=True))
                - jnp.exp(jnp.sum(lam[2:3] * lam[3:4], axis=-1, keepdims=True)) + LAMBDA_INIT)
    slope = slope_ref[0][:, 0:1]
    first = lax.broadcasted_iota(jnp.int32, (1, width), 1) < DA_HEAD_DIM
    sg = sg_ref[...] * (1.0 - LAMBDA_INIT)

    n_blk = seq // tq
    dist = lax.broadcasted_iota(jnp.int32, (tq, tq), 0) - lax.broadcasted_iota(jnp.int32, (tq, tq), 1)
    for d in range(2 * n_blk - 1):
        bias_ref[d] = slope * jnp.abs(dist - (d - (n_blk - 1)) * tq).astype(F32)

    def body(i, carry):
        r0 = pl.multiple_of(i * tq, tq)
        qt = qs[pl.ds(r0, tq), :]
        q0 = jnp.where(first, qt, jnp.zeros_like(qt))
        q1 = jnp.where(first, jnp.zeros_like(qt), qt)
        kk = ks[...]
        bias = jnp.concatenate([bias_ref[j - i + (n_blk - 1)] for j in range(n_blk)], axis=1)
        s0 = _dot_nt(q0, kk) - bias
        s1 = _dot_nt(q1, kk) - bias
        p0 = jnp.exp(s0 - jnp.max(s0, axis=-1, keepdims=True))
        p1 = jnp.exp(s1 - jnp.max(s1, axis=-1, keepdims=True))
        w0 = 1.0 / jnp.sum(p0, axis=-1, keepdims=True)
        w1 = lam_full / jnp.sum(p1, axis=-1, keepdims=True)
        o = _dot(p0.astype(BF16), vs[...]) * w0 - _dot(p1.astype(BF16), vs[...]) * w1
        ms = jnp.mean(o * o, axis=-1, keepdims=True)
        o_ref[0, pl.ds(r0, tq), :] = o * lax.rsqrt(ms + NORM_EPS) * sg
        return carry

    lax.fori_loop(0, seq // tq, body, 0)


def _diff_attention(p_att3, qn_g, kn_g, lam, subln_g, tq=256):
    bsz, seq, _ = p_att3.shape
    width = 2 * DA_HEAD_DIM
    slopes = jnp.asarray([2.0 ** (-8.0 * (h + 1) / DA_HEADS) for h in range(DA_HEADS)], F32)
    slopes = jnp.broadcast_to(slopes[:, None, None], (DA_HEADS, 1, width))
    blk = lambda off: pl.BlockSpec((1, seq, width), lambda b, h: (b, 0, off + h))
    full = lambda shape: pl.BlockSpec(shape, lambda b, h: (0,) * len(shape))
    return pl.pallas_call(
        functools.partial(_attn_kernel, tq),
        grid=(bsz, DA_HEADS),
        in_specs=[blk(0), blk(DA_HEADS), blk(2 * DA_HEADS),
                  full((1, width)), full((1, width)), full((4, DA_HEAD_DIM)),
                  pl.BlockSpec((1, 1, width), lambda b, h: (h, 0, 0)),
                  full((1, width))],
        out_specs=pl.BlockSpec((1, seq, width), lambda b, h: (b, 0, h)),
        out_shape=jax.ShapeDtypeStruct((bsz, seq, DA_HEADS * width), F32),
        scratch_shapes=[pltpu.VMEM((seq, width), BF16)] * 3 + [pltpu.VMEM((2 * (seq // tq) - 1, tq, tq), F32)],
        compiler_params=_params("arbitrary", "arbitrary"),
        name="diff_attention",
    )(p_att3, p_att3, p_att3,
      jnp.tile(qn_g.reshape(1, DA_HEAD_DIM), (1, 2)), jnp.tile(kn_g.reshape(1, DA_HEAD_DIM), (1, 2)),
      lam, slopes, subln_g.reshape(1, width))


def _rwkv_chunk_kernel(n_chunks, cur_ref, prev_ref, next_ref, mu_ref, w0_ref, w2_ref, a0_ref, a2_ref, g2_ref,
                       kk_ref, ka_ref, rk_ref, hs_ref, ra_ref, y0_ref, g_ref, h_ref, bonus_ref, gate_ref):
    ch = RW_CHUNK
    w = RW_WIDTH
    pw = 2 * RW_HEAD
    chunk = pl.program_id(1)

    pc = cur_ref[0]
    row = lax.broadcasted_iota(jnp.int32, (ch, 1), 0)
    pv = prev_ref[0][SUBLANES - 1:SUBLANES, :] * (chunk > 0).astype(F32)
    nx = next_ref[0][0:1, :] * (chunk < n_chunks - 1).astype(F32)
    prev = jnp.where(row == 0, pv, pltpu.roll(pc, 1, 0))
    nxt = jnp.where(row == ch - 1, nx, pltpu.roll(pc, ch - 1, 0))
    mu = mu_ref[...]
    ps = pc + mu[0:1] * (prev - pc) + mu[1:2] * (nxt - pc)
    r, k, v = ps[:, 0:w], ps[:, w:2 * w], ps[:, 2 * w:3 * w]
    wd = ps[:, 3 * w:3 * w + LORA_COLS]
    ad = ps[:, 3 * w + LORA_COLS:3 * w + 2 * LORA_COLS]
    gd = ps[:, 3 * w + 2 * LORA_COLS:3 * w + 3 * LORA_COLS]

    head_sum = hs_ref[...]

    def per_head_sum(x):
        return _dot(x.astype(BF16), head_sum)

    kkr = k * kk_ref[...]
    kk = kkr / jnp.maximum(jnp.sqrt(per_head_sum(kkr * kkr)), 1e-12)
    bonus_ref[0] = per_head_sum(r * k * rk_ref[...]) * v
    gate_ref[0] = _dot(jax.nn.sigmoid(gd), g2_ref[...])

    ti = lax.broadcasted_iota(jnp.int32, (ch, ch), 0)
    tj = lax.broadcasted_iota(jnp.int32, (ch, ch), 1)
    eye = (ti == tj).astype(F32)
    lane = lax.broadcasted_iota(jnp.int32, (1, pw), 1)
    head_lanes = (lane < RW_HEAD, lane >= RW_HEAD)
    pi = lax.broadcasted_iota(jnp.int32, (pw, pw), 0)
    pj = lax.broadcasted_iota(jnp.int32, (pw, pw), 1)
    same_head = (pi < RW_HEAD) == (pj < RW_HEAD)
    eye_pair = (pi == pj).astype(F32)
    v16 = v.astype(BF16)
    tanh_wd = jnp.tanh(wd)

    probs = []
    pairs = []
    for d in range(2):
        z = w0_ref[d:d + 1] + _dot(tanh_wd, w2_ref[d])
        logdec = -jax.nn.sigmoid(z) * math.exp(-0.5)
        a = jax.nn.sigmoid(a0_ref[d:d + 1] + _dot(ad, a2_ref[d]))
        kd = k * (1.0 + (a - 1.0) * ka_ref[...])
        before = (tj < ti) if d == 0 else (tj > ti)
        upto = before | (tj == ti)
        cum = _dot(upto.astype(F32), logdec, HIGHEST)
        total = cum[ch - 1:ch] if d == 0 else cum[0:1]
        w_in, w_ex, w_inv, w_tot = jnp.exp(cum), jnp.exp(cum - logdec), jnp.exp(-cum), jnp.exp(total)
        a_bar = -kk * w_ex
        b_bar = (kk * a * w_inv).astype(BF16)
        k_bar = (kd * w_inv).astype(BF16)
        r_bar = r * (w_in if d == 0 else w_ex)
        ymask = upto if d == 0 else before
        later, earlier = (ti, tj) if d == 0 else (tj, ti)
        off_masks = []
        for lvl in range(int(math.log2(ch))):
            same_pair = (ti >> (lvl + 1)) == (tj >> (lvl + 1))
            off_masks.append(same_pair & (((later >> lvl) & 1) == 1) & (((earlier >> lvl) & 1) == 0))
        for p in range(RW_PAIRS):
            sl = slice(p * pw, (p + 1) * pw)
            pairs.append(dict(d=d, p=p, b=b_bar[:, sl], k=k_bar[:, sl], v=v16[:, sl], r=r_bar[:, sl],
                              w_tot=w_tot[:, sl]))
            for hh in range(2):
                mh = head_lanes[hh]
                probs.append(dict(pair=len(pairs) - 1, before=before, ymask=ymask, off=off_masks,
                                  a=jnp.where(mh, a_bar[:, sl], 0.0).astype(BF16),
                                  r=jnp.where(mh, r_bar[:, sl], 0.0).astype(BF16),
                                  v=jnp.where(mh, v16[:, sl], jnp.zeros_like(v16[:, sl]))))

    for q in probs:
        pr = pairs[q["pair"]]
        q["ab"] = _dot_nt(q["a"], pr["b"])
        ak = _dot_nt(q["a"], pr["k"])
        q["rb"] = jnp.where(q["ymask"], _dot_nt(q["r"], pr["b"]), 0.0).astype(BF16)
        rk = _dot_nt(q["r"], pr["k"])
        q["ak"] = jnp.where(q["before"], ak, 0.0).astype(BF16)
        q["rk"] = jnp.where(q["ymask"], rk, 0.0).astype(BF16)
    for q in probs:
        q["akv"] = _dot(q["ak"], q["v"])
        q["rkv"] = _dot(q["rk"], q["v"])
        q["inv"] = eye + jnp.where(q["off"][0], q["ab"], 0.0)
    for lvl in range(1, int(math.log2(ch))):
        for q in probs:
            q["tmp"] = _dot(q["inv"].astype(BF16), jnp.where(q["off"][lvl], q["ab"], 0.0).astype(BF16))
        for q in probs:
            q["inv"] = q["inv"] + _dot(q["tmp"].astype(BF16), q["inv"].astype(BF16))
    for q in probs:
        rhs = jnp.concatenate([q["a"], q["akv"].astype(BF16)], axis=1)
        q["x"] = _dot(q["inv"].astype(BF16), rhs)
    for q in probs:
        q["z"] = _dot(q["rb"], q["x"].astype(BF16))
    for i, pr in enumerate(pairs):
        q0, q1 = probs[2 * i], probs[2 * i + 1]
        x = q0["x"] + q1["x"]
        z = q0["z"] + q1["z"]
        ra = pr["r"] + z[:, :pw]
        y0 = z[:, pw:] + q0["rkv"] + q1["rkv"]
        xb = _dot_tn(x.astype(BF16), pr["b"])
        vk = _dot_tn(pr["v"], pr["k"])
        g = (eye_pair + jnp.where(same_head, xb[:pw], 0.0)) * pr["w_tot"]
        h = jnp.where(same_head, xb[pw:] + vk, 0.0) * pr["w_tot"]
        d, p = pr["d"], pr["p"]
        sl = slice(p * pw, (p + 1) * pw)
        ra_ref[0, d, :, sl] = ra.astype(BF16)
        y0_ref[0, d, :, sl] = y0.astype(BF16)
        g_ref[0, d, 0, :, sl] = g.astype(BF16)
        h_ref[0, d, 0, :, sl] = h.astype(BF16)


def _pad_lora(w2):
    keep = jnp.arange(2)[:, None, None, None] == jnp.arange(2)[None, :, None, None]
    return jnp.where(keep, w2[None], 0.0).reshape(2, -1, w2.shape[-1])


def _rwkv_chunks(p_rw3, mu, w0, w2, a0, a2, g2, k_k, k_a, r_k):
    bsz, seq, cols = p_rw3.shape
    ch, w, pw = RW_CHUNK, RW_WIDTH, 2 * RW_HEAD
    n_chunks = seq // ch
    sub_per_chunk = ch // SUBLANES
    n_sub = seq // SUBLANES
    full = lambda shape: pl.BlockSpec(shape, lambda b, c: (0,) * len(shape))
    row = lambda x: x.reshape(1, w)
    head_sum = (jnp.arange(w)[:, None] // RW_HEAD == jnp.arange(w)[None, :] // RW_HEAD).astype(BF16)
    tok = lambda dt: jax.ShapeDtypeStruct((bsz, 2, seq, w), dt)
    mat = lambda dt: jax.ShapeDtypeStruct((bsz, 2, n_chunks, pw, w), dt)
    tok_spec = pl.BlockSpec((1, 2, ch, w), lambda b, c: (b, 0, c, 0))
    mat_spec = pl.BlockSpec((1, 2, 1, pw, w), lambda b, c: (b, 0, c, 0, 0))
    one_spec = pl.BlockSpec((1, ch, w), lambda b, c: (b, c, 0))
    return pl.pallas_call(
        functools.partial(_rwkv_chunk_kernel, n_chunks),
        grid=(bsz, n_chunks),
        in_specs=[pl.BlockSpec((1, ch, cols), lambda b, c: (b, c, 0)),
                  pl.BlockSpec((1, SUBLANES, cols), lambda b, c: (b, jnp.maximum(c * sub_per_chunk - 1, 0), 0)),
                  pl.BlockSpec((1, SUBLANES, cols),
                               lambda b, c: (b, jnp.minimum((c + 1) * sub_per_chunk, n_sub - 1), 0)),
                  full((2, cols)), full((2, w)), full((2, LORA_COLS, w)), full((2, w)), full((2, LORA_COLS, w)),
                  full((LORA_COLS, w)), full((1, w)), full((1, w)), full((1, w)), full((w, w))],
        out_specs=[tok_spec, tok_spec, mat_spec, mat_spec, one_spec, one_spec],
        out_shape=[tok(BF16), tok(BF16), mat(BF16), mat(BF16),
                   jax.ShapeDtypeStruct((bsz, seq, w), F32), jax.ShapeDtypeStruct((bsz, seq, w), F32)],
        compiler_params=_params("parallel", "parallel"),
        name="rwkv7_chunks",
    )(p_rw3, p_rw3, p_rw3, mu, w0, _pad_lora(w2), a0, _pad_lora(a2), g2, row(k_k), row(k_a), row(r_k), head_sum)


def _rwkv_scan_kernel(ra0_ref, ra1_ref, y00_ref, y01_ref, g0_ref, g1_ref, h0_ref, h1_ref, o0_ref, o1_ref, st_ref):
    pw = 2 * RW_HEAD

    @pl.when(pl.program_id(1) == 0)
    def _():
        st_ref[...] = jnp.zeros_like(st_ref)

    ch = RW_CHUNK
    n_sub = g0_ref.shape[2]
    dirs = ((ra0_ref, y00_ref, g0_ref, h0_ref, o0_ref), (ra1_ref, y01_ref, g1_ref, h1_ref, o1_ref))
    states = {(d, p): st_ref[d, p] for d in range(2) for p in range(RW_PAIRS)}
    for step in range(n_sub):
        for d, (ra_ref, y0_ref, g_ref, h_ref, o_ref) in enumerate(dirs):
            c = step if d == 0 else n_sub - 1 - step
            rows = slice(c * ch, (c + 1) * ch)
            for p in range(RW_PAIRS):
                sl = slice(p * pw, (p + 1) * pw)
                s16 = states[d, p].astype(BF16)
                o_ref[0, rows, sl] = _dot_nt(ra_ref[0, 0, rows, sl], s16) + y0_ref[0, 0, rows, sl]
                states[d, p] = _dot(s16, g_ref[0, 0, c, :, sl]) + h_ref[0, 0, c, :, sl]
    for (d, p), s in states.items():
        st_ref[d, p] = s


def _rwkv_scan(ra, y0, g, h, n_sub=4):
    bsz, _, seq, w = ra.shape
    ch, pw = RW_CHUNK, 2 * RW_HEAD
    n_blk = seq // (ch * n_sub)
    blk = lambda d, c: c if d == 0 else n_blk - 1 - c
    tok = lambda d: pl.BlockSpec((1, 1, n_sub * ch, w), lambda b, c: (b, d, blk(d, c), 0))
    mat = lambda d: pl.BlockSpec((1, 1, n_sub, pw, w), lambda b, c: (b, d, blk(d, c), 0, 0))
    out = lambda d: pl.BlockSpec((1, n_sub * ch, w), lambda b, c: (b, blk(d, c), 0))
    return pl.pallas_call(
        _rwkv_scan_kernel,
        grid=(bsz, n_blk),
        in_specs=[tok(0), tok(1), tok(0), tok(1), mat(0), mat(1), mat(0), mat(1)],
        out_specs=[out(0), out(1)],
        out_shape=[jax.ShapeDtypeStruct((bsz, seq, w), F32)] * 2,
        scratch_shapes=[pltpu.VMEM((2, RW_PAIRS, pw, pw), F32)],
        compiler_params=_params("parallel", "arbitrary"),
        name="rwkv7_scan",
    )(ra, ra, y0, y0, g, g, h, h)


def _outproj_kernel(att_ref, yf_ref, yb_ref, bonus_ref, gate_ref, lng_ref, lnb_ref, hs_ref, x_ref, mod_ref,
                    wa_ref, wr_ref, g_ref, x1_ref, h2_ref):
    head_sum = hs_ref[...]

    def per_head_mean(v):
        return _dot(v.astype(BF16), head_sum) * (1.0 / RW_HEAD)

    y = yf_ref[...] + yb_ref[...]
    yc = y - per_head_mean(y)
    yn = yc * lax.rsqrt(per_head_mean(yc * yc) + GN_EPS) * lng_ref[...] + lnb_ref[...]
    rw = (yn + bonus_ref[...]) * gate_ref[...]
    acc = _dot(att_ref[...].astype(BF16), wa_ref[...]) + _dot(rw.astype(BF16), wr_ref[...])
    m = mod_ref[0]
    x1 = x_ref[...] + m[2:3] * acc
    x1_ref[...] = x1
    ms = jnp.mean(x1 * x1, axis=-1, keepdims=True)
    yo = x1 * lax.rsqrt(ms + NORM_EPS) * g_ref[...]
    h2_ref[...] = (yo * (1.0 + m[4:5]) + m[3:4]).astype(BF16)


def _out_proj(att2, y_f, y_b, bonus, gate, ln_g, ln_b, x2, mod3, w_a, w_r, g, seq, tm=512):
    t, d = x2.shape
    ka, kr = att2.shape[1], y_f.shape[1]
    per_b = seq // tm
    rows = lambda n: pl.BlockSpec((tm, n), lambda i: (i, 0))
    full = lambda shape: pl.BlockSpec(shape, lambda i: (0,) * len(shape))
    head_sum = (jnp.arange(kr)[:, None] // RW_HEAD == jnp.arange(kr)[None, :] // RW_HEAD).astype(BF16)
    return pl.pallas_call(
        _outproj_kernel,
        grid=(t // tm,),
        in_specs=[rows(ka), rows(kr), rows(kr), rows(kr), rows(kr), full((1, kr)), full((1, kr)), full((kr, kr)),
                  rows(d), pl.BlockSpec((1, 6, d), lambda i: (i // per_b, 0, 0)),
                  full((ka, d)), full((kr, d)), full((1, d))],
        out_specs=[rows(d), rows(d)],
        out_shape=[jax.ShapeDtypeStruct((t, d), F32), jax.ShapeDtypeStruct((t, d), BF16)],
        compiler_params=_params("arbitrary"),
        name="out_proj",
    )(att2, y_f, y_b, bonus, gate, ln_g.reshape(1, kr), ln_b.reshape(1, kr), head_sum, x2, mod3, w_a, w_r,
      g.reshape(1, d))


def _topk_ranks(scores, vals_refs, rank_refs, break_ties):
    nk, tm = scores[0].shape
    kiota = lax.broadcasted_iota(jnp.int32, (nk, tm), 0)

    def body(j, carry):
        out = []
        for (cur, rank), vals_ref in zip(carry, vals_refs):
            m = jnp.max(cur, axis=0, keepdims=True)
            hit = cur == m
            if break_ties:
                hit = kiota == jnp.min(jnp.where(hit, kiota, nk), axis=0, keepdims=True)
            vals_ref[pl.ds(j, 1), :] = m
            out.append((jnp.where(hit, NEG_INF, cur), None if rank is None else jnp.where(hit, j, rank)))
        return tuple(out)

    init = tuple((s, None if ref is None else jnp.full((nk, tm), PEER_TOPK, jnp.int32))
                 for s, ref in zip(scores, rank_refs))
    excess = jnp.zeros((1, tm), F32)
    for (cur, rank), rank_ref in zip(lax.fori_loop(0, PEER_TOPK, body, init), rank_refs):
        if rank_ref is not None:
            rank_ref[...] = rank
        removed = jnp.sum(jnp.where(cur == NEG_INF, 1.0, 0.0), axis=0, keepdims=True)
        excess = jnp.maximum(excess, jnp.abs(removed - PEER_TOPK))
    return jnp.max(excess)


def _pair_candidates():
    pieces = []
    for lvl in range(4):
        cnt = PEER_TOPK // (lvl + 1)
        pieces.append(("row", lvl, lvl, cnt))
        if lvl + 1 < cnt:
            pieces.append(("col", lvl, lvl + 1, cnt))
    return pieces


def _select_pairs(cand, ci, valid, sel_ref, break_ties):
    big = 4 * PEER_TOPK * PEER_TOPK

    def body(j, cur):
        m = jnp.max(cur, axis=0, keepdims=True)
        hit = cur == m
        if break_ties:
            hit = ci == jnp.min(jnp.where(hit, ci, big), axis=0, keepdims=True)
        return jnp.where(hit, NEG_INF, cur)

    cur = lax.fori_loop(0, PEER_TOPK, body, cand)
    sel = jnp.where((cur == NEG_INF) & valid, 1.0, 0.0)
    sel_ref[...] = sel
    return jnp.max(jnp.abs(jnp.sum(sel, axis=0, keepdims=True) - PEER_TOPK))


def _route_kernel(h_ref, wq_ref, keys_ref, p1_ref, cnt_ref, p2_ref, rk2_ref, q_scr, v1_ref, v2_ref, rank_ref,
                  sel_ref, key_ref, kref_ref):
    tm = h_ref.shape[0]
    nk = PEER_NKEYS
    n_grp = tm // LANES
    q = _dot(h_ref[...], wq_ref[...]).astype(BF16)
    for hp in range(2 * PEER_HEADS):
        q_scr[hp] = q[:, hp * nk:(hp + 1) * nk]

    pieces = _pair_candidates()
    big = 4 * PEER_TOPK * PEER_TOPK
    cat = lambda parts: jnp.concatenate(parts, axis=1)

    def head(h, carry):
        s1 = _dot_nt(keys_ref[2 * h], q_scr[2 * h])
        s2 = _dot_nt(keys_ref[2 * h + 1], q_scr[2 * h + 1])
        probs = [s[:, g * LANES:(g + 1) * LANES] for s in (s1, s2) for g in range(n_grp)]
        vals = [ref.at[g] for ref in (v1_ref, v2_ref) for g in range(n_grp)]
        ranks = [rank_ref.at[i] for i in range(2 * n_grp)]
        excess = 0.0
        for i in range(0, len(probs), 3):
            excess = jnp.maximum(excess, _topk_ranks(probs[i:i + 3], vals[i:i + 3], [None] * len(probs[i:i + 3]),
                                                     break_ties=False))
        key_ref[0] = s1
        key_ref[1] = s2
        kref_ref[0] = cat([v1_ref[g] for g in range(n_grp)])
        kref_ref[1] = cat([v2_ref[g] for g in range(n_grp)])

        @pl.when(excess > 0.0)
        def _():
            _topk_ranks(probs, vals, ranks, break_ties=True)
            rank_row = jnp.broadcast_to(lax.broadcasted_iota(jnp.int32, (PEER_TOPK, 1), 0).astype(F32),
                                        (PEER_TOPK, tm))
            for half in range(2):
                key_ref[half] = cat([rank_ref[half * n_grp + g] for g in range(n_grp)]).astype(F32)
                kref_ref[half] = rank_row

        v1 = cat([v1_ref[g] for g in range(n_grp)])
        v2 = cat([v2_ref[g] for g in range(n_grp)])

        cands, cis, valids = [], [], []
        for kind, fixed, lo, hi in pieces:
            rows = SUBLANES * ((hi + SUBLANES - 1) // SUBLANES)
            ridx = lax.broadcasted_iota(jnp.int32, (rows, 1), 0)
            valid = (ridx >= lo) & (ridx < hi)
            if kind == "row":
                vsum = v1[fixed:fixed + 1] + v2[0:rows]
                ci = fixed * PEER_TOPK + ridx
            else:
                vsum = v1[0:rows] + v2[fixed:fixed + 1]
                ci = ridx * PEER_TOPK + fixed
            cands.append(jnp.where(valid, vsum, NEG_INF))
            cis.append(jnp.broadcast_to(jnp.where(valid, ci, big), (rows, tm)))
            valids.append(jnp.broadcast_to(valid, (rows, tm)))
        cand = jnp.concatenate(cands, axis=0)
        ci = jnp.concatenate(cis, axis=0)
        valid = jnp.concatenate(valids, axis=0)
        off = _select_pairs(cand, ci, valid, sel_ref, break_ties=False)

        @pl.when(off > 0.0)
        def _():
            _select_pairs(cand, ci, valid, sel_ref, break_ties=True)

        sel = sel_ref[...]
        zsum = jnp.sum(sel * jnp.exp(cand - (v1[0:1] + v2[0:1])), axis=0, keepdims=True)
        aidx = lax.broadcasted_iota(jnp.int32, (PEER_TOPK, 1), 0)
        per_a = jnp.zeros((PEER_TOPK, tm), F32)
        row0 = 0
        for kind, fixed, lo, hi in pieces:
            rows = SUBLANES * ((hi + SUBLANES - 1) // SUBLANES)
            part = sel[row0:row0 + rows]
            row0 += rows
            if kind == "row":
                per_a = per_a + jnp.where(aidx == fixed, jnp.sum(part, axis=0, keepdims=True), 0.0)
            elif rows == PEER_TOPK:
                per_a = per_a + part
            else:
                per_a = per_a + jnp.concatenate([part, jnp.zeros((PEER_TOPK - rows, tm), F32)], axis=0)
        key1, key2 = key_ref[0], key_ref[1]
        kref1, kref2 = kref_ref[0], kref_ref[1]
        cnt = jnp.zeros((nk, tm), F32)
        rank2 = jnp.full((nk, tm), float(PEER_TOPK), F32)
        for a in range(PEER_TOPK):
            cnt = jnp.where(key1 == kref1[a:a + 1], per_a[a:a + 1], cnt)
            rank2 = jnp.where(key2 == kref2[a:a + 1], float(a), rank2)
        p1_ref[h] = jnp.exp(s1 - v1[0:1]) / (zsum * 2.0 ** 0.5)
        cnt_ref[h] = cnt
        p2_ref[h] = jnp.exp(s2 - v2[0:1]).astype(BF16)
        rk2_ref[h] = rank2.astype(BF16)
        return carry

    lax.fori_loop(0, PEER_HEADS, head, 0)


def _route(h2, wq, keys, tm=512):
    t, d = h2.shape
    nq = wq.shape[1]
    nk = PEER_NKEYS
    out = lambda dt: jax.ShapeDtypeStruct((PEER_HEADS, nk, t), dt)
    ospec = pl.BlockSpec((PEER_HEADS, nk, tm), lambda i: (0, 0, i))
    return pl.pallas_call(
        _route_kernel,
        grid=(t // tm,),
        in_specs=[pl.BlockSpec((tm, d), lambda i: (i, 0)),
                  pl.BlockSpec((d, nq), lambda i: (0, 0)),
                  pl.BlockSpec((2 * PEER_HEADS, nk, nk), lambda i: (0, 0, 0))],
        out_specs=[ospec] * 4,
        out_shape=[out(F32), out(F32), out(BF16), out(BF16)],
        scratch_shapes=[pltpu.VMEM((2 * PEER_HEADS, tm, nk), BF16),
                        pltpu.VMEM((tm // LANES, PEER_TOPK, LANES), F32),
                        pltpu.VMEM((tm // LANES, PEER_TOPK, LANES), F32),
                        pltpu.VMEM((2 * tm // LANES, nk, LANES), jnp.int32),
                        pltpu.VMEM((sum(SUBLANES * ((hi + SUBLANES - 1) // SUBLANES)
                                        for _, _, _, hi in _pair_candidates()), tm), F32),
                        pltpu.VMEM((2, nk, tm), F32), pltpu.VMEM((2, PEER_TOPK, tm), F32)],
        compiler_params=_params("parallel"),
        name="peer_route",
    )(h2, wq, keys)


def _expert_kernel(n_eblk, h_ref, u_ref, vt_ref, p1_ref, cnt_ref, p2_ref, rk2_ref, x1_ref, mod_ref,
                   o_ref, acc_ref, act_ref, gate_ref):
    j = pl.program_id(1)
    te = u_ref.shape[0]
    tm = h_ref.shape[0]
    nk = PEER_NKEYS
    kb_rows = MXU_WIDTH
    per_kb = kb_rows // nk
    tile = 2 * SUBLANES

    @pl.when(j == 0)
    def _():
        acc_ref[...] = jnp.zeros_like(acc_ref)

    hb = h_ref[...]
    n_kb = te // kb_rows
    second_k = 2 * MXU_WIDTH

    def activations(kb):
        rows = slice(kb * kb_rows, (kb + 1) * kb_rows)
        act_ref[rows, :] = _dot_nt(u_ref[rows, :], hb)

    def gates(kb):
        for sb in range(kb * per_kb, (kb + 1) * per_kb):
            e1 = j * (te // nk) + sb
            gate = None
            for h in range(PEER_HEADS):
                c = jnp.broadcast_to(cnt_ref[h, pl.ds(e1, 1), :], (tile, tm)).astype(BF16)
                p = jnp.broadcast_to(p1_ref[h, pl.ds(e1, 1), :], (tile, tm)).astype(BF16)
                rk2 = rk2_ref[h].reshape(nk // tile, tile, tm)
                p2 = p2_ref[h].reshape(nk // tile, tile, tm)
                term = jnp.where(rk2 < c[None], p2, jnp.zeros_like(p2)) * p[None]
                gate = term if gate is None else gate + term
            gate_ref[sb * nk:(sb + 1) * nk, :] = gate.reshape(nk, tm)

    for kb in range(n_kb):
        gates(kb)
    for kb in range(n_kb):
        activations(kb)
    acc = acc_ref[...]
    for c0 in range(0, te, second_k):
        rows = slice(c0, c0 + second_k)
        t = act_ref[rows, :]
        w = gate_ref[rows, :] * (t + t * lax.erf(t)).astype(BF16)
        acc = acc + _dot(vt_ref[:, rows], w)
    acc_ref[...] = acc

    @pl.when(j == n_eblk - 1)
    def _():
        o_ref[...] = x1_ref[...] + mod_ref[0][5:6] * acc_ref[...].T


def _experts(h2, u, vt, p1, cnt, p2, rk2, x1, mod3, seq, tm=512, te=2048):
    t, d = h2.shape
    ne = u.shape[0]
    nk = PEER_NKEYS
    n_eblk = ne // te
    per_b = seq // tm
    rspec = pl.BlockSpec((PEER_HEADS, nk, tm), lambda i, j: (0, 0, i))
    return pl.pallas_call(
        functools.partial(_expert_kernel, n_eblk),
        grid=(t // tm, n_eblk),
        in_specs=[pl.BlockSpec((tm, d), lambda i, j: (i, 0)),
                  pl.BlockSpec((te, d), lambda i, j: (j, 0)),
                  pl.BlockSpec((d, te), lambda i, j: (0, j)),
                  rspec, rspec, rspec, rspec,
                  pl.BlockSpec((tm, d), lambda i, j: (i, 0)),
                  pl.BlockSpec((1, 6, d), lambda i, j: (i // per_b, 0, 0))],
        out_specs=pl.BlockSpec((tm, d), lambda i, j: (i, 0)),
        out_shape=jax.ShapeDtypeStruct((t, d), F32),
        scratch_shapes=[pltpu.VMEM((d, tm), F32), pltpu.VMEM((te, tm), F32), pltpu.VMEM((te, tm), BF16)],
        compiler_params=_params("parallel", "arbitrary"),
        name="peer_experts",
    )(h2, u, vt, p1, cnt, p2, rk2, x1, mod3)


def kernel(x, c, ada_w, ada_b, norm1_g, w_in, da_qnorm_g, da_knorm_g, da_lambda, da_subln_g, rw_shift_mu, rw_w0,
           rw_w2, rw_a0, rw_a2, rw_g2, rw_k_k, rw_k_a, rw_r_k, rw_ln_g, rw_ln_b, w_out, norm2_g, peer_wq,
           peer_keys, peer_u, peer_v):
    bsz, seq, d = x.shape
    depth = ada_w.shape[0]
    t = bsz * seq
    for l in range(depth):
        mod3 = _ada_mod(c, ada_w[l], ada_b[l]).reshape(bsz, 6, d)
        x2 = x.reshape(t, d)
        w_l = w_in[l].astype(BF16)
        p_att, p_rw = _in_proj(x2, mod3, norm1_g[l], w_l[:, :ATT_COLS], w_l[:, ATT_COLS:], seq)
        att = _diff_attention(p_att.reshape(bsz, seq, ATT_COLS), da_qnorm_g[l], da_knorm_g[l], da_lambda[l],
                              da_subln_g[l])
        ra, y0, g, h, bonus, gate = _rwkv_chunks(p_rw.reshape(bsz, seq, RWKV_COLS), rw_shift_mu[l], rw_w0[l],
                                                 rw_w2[l], rw_a0[l], rw_a2[l], rw_g2[l], rw_k_k[l], rw_k_a[l],
                                                 rw_r_k[l].reshape(-1))
        y_f, y_b = _rwkv_scan(ra, y0, g, h)
        wo = w_out[l].astype(BF16)
        aw = att.shape[-1]
        flat = lambda a: a.reshape(t, RW_WIDTH)
        x1, h2 = _out_proj(att.reshape(t, aw), flat(y_f), flat(y_b), flat(bonus), flat(gate), rw_ln_g[l],
                           rw_ln_b[l], x2, mod3, wo[:aw], wo[aw:], norm2_g[l], seq)
        keys = peer_keys[l].reshape(2 * PEER_HEADS, PEER_NKEYS, -1).astype(BF16)
        p1, cnt, p2, rk2 = _route(h2, peer_wq[l].astype(BF16), keys)
        u_scaled = (peer_u[l] * 2.0 ** -0.5).astype(BF16)
        x = _experts(h2, u_scaled, peer_v[l].T.astype(BF16), p1, cnt, p2, rk2, x1, mod3, seq).reshape(bsz, seq, d)
    return x
```

```python
import functools
import math

import jax
import jax.numpy as jnp
from jax import lax
from jax.experimental import pallas as pl
from jax.experimental.pallas import tpu as pltpu

F32 = jnp.float32
BF16 = jnp.bfloat16
HIGHEST = lax.Precision.HIGHEST

LANES = 128
SUBLANES = 8
MXU_WIDTH = 256
VMEM_LIMIT_BYTES = 56 * 1024 * 1024

DA_HEAD_DIM = 64
DA_HEADS = 4
RW_HEAD = 64
RW_HEADS = 8
RW_PAIRS = RW_HEADS // 2
RW_WIDTH = RW_HEADS * RW_HEAD
LORA_COLS = 128
RWKV_COLS = 3 * RW_WIDTH + 3 * LORA_COLS
ATT_COLS = 3 * 2 * DA_HEAD_DIM * DA_HEADS
GN_EPS = 64e-5
NORM_EPS = 1e-6
LAMBDA_INIT = 0.8 - 0.6 * math.exp(-0.3 * 0)
PEER_HEADS = 8
PEER_NKEYS = 128
PEER_TOPK = 16
RW_CHUNK = 64
NEG_INF = float("-inf")


def _params(*sem):
    return pltpu.CompilerParams(dimension_semantics=sem, vmem_limit_bytes=VMEM_LIMIT_BYTES)


def _dot(a, b, precision=None):
    return jnp.dot(a, b, preferred_element_type=F32, precision=precision)


def _dot_nt(a, b, precision=None):
    return lax.dot_general(a, b, (((1,), (1,)), ((), ())), preferred_element_type=F32, precision=precision)


def _dot_tn(a, b, precision=None):
    return lax.dot_general(a, b, (((0,), (0,)), ((), ())), preferred_element_type=F32, precision=precision)


def _group_matrix(n, group, value):
    shift = group.bit_length() - 1
    r = lax.broadcasted_iota(jnp.int32, (n, n), 0) >> shift
    c = lax.broadcasted_iota(jnp.int32, (n, n), 1) >> shift
    return jnp.where(r == c, value, 0.0).astype(F32)


def _ada_kernel(c_ref, w_ref, b_ref, o_ref):
    c = c_ref[...]
    s = c * jax.nn.sigmoid(c)
    o_ref[...] = _dot(s, w_ref[...], HIGHEST) + b_ref[...]


def _ada_mod(c, w, b):
    bsz, d = c.shape
    n = w.shape[1]
    tn = 1024
    return pl.pallas_call(
        _ada_kernel,
        grid=(n // tn,),
        in_specs=[pl.BlockSpec((bsz, d), lambda j: (0, 0)),
                  pl.BlockSpec((d, tn), lambda j: (0, j)),
                  pl.BlockSpec((1, tn), lambda j: (0, j))],
        out_specs=pl.BlockSpec((bsz, tn), lambda j: (0, j)),
        out_shape=jax.ShapeDtypeStruct((bsz, n), F32),
        compiler_params=_params("arbitrary"),
        name="ada_mod",
    )(c, w, b.reshape(1, n))


def _inproj_kernel(x_ref, mod_ref, g_ref, wa_ref, wr_ref, oa_ref, or_ref):
    x = x_ref[...]
    ms = jnp.mean(x * x, axis=-1, keepdims=True)
    y = x * lax.rsqrt(ms + NORM_EPS) * g_ref[...]
    m = mod_ref[0]
    h = (y * (1.0 + m[1:2]) + m[0:1]).astype(BF16)
    oa_ref[...] = _dot(h, wa_ref[...])
    or_ref[...] = _dot(h, wr_ref[...])


def _in_proj(x2, mod3, g, w_att, w_rw, seq, tm=512):
    t, d = x2.shape
    na, nr = w_att.shape[1], w_rw.shape[1]
    per_b = seq // tm
    return pl.pallas_call(
        _inproj_kernel,
        grid=(t // tm,),
        in_specs=[pl.BlockSpec((tm, d), lambda i: (i, 0)),
                  pl.BlockSpec((1, 6, d), lambda i: (i // per_b, 0, 0)),
                  pl.BlockSpec((1, d), lambda i: (0, 0)),
                  pl.BlockSpec((d, na), lambda i: (0, 0)),
                  pl.BlockSpec((d, nr), lambda i: (0, 0))],
        out_specs=[pl.BlockSpec((tm, na), lambda i: (i, 0)),
                   pl.BlockSpec((tm, nr), lambda i: (i, 0))],
        out_shape=[jax.ShapeDtypeStruct((t, na), F32), jax.ShapeDtypeStruct((t, nr), F32)],
        compiler_params=_params("arbitrary"),
        name="in_proj",
    )(x2, mod3, g.reshape(1, d), w_att, w_rw)


def _attn_kernel(tq, q_ref, k_ref, v_ref, qg_ref, kg_ref, lam_ref, slope_ref, sg_ref, o_ref, qs, ks, vs, bias_ref):
    seq = q_ref.shape[1]
    width = 2 * DA_HEAD_DIM
    avg = _group_matrix(width, DA_HEAD_DIM, 1.0 / DA_HEAD_DIM)

    def qk_norm(x, g):
        ms = _dot(x * x, avg)
        return x * lax.rsqrt(ms + NORM_EPS) * g

    qs[...] = (qk_norm(q_ref[0], qg_ref[...]) * (DA_HEAD_DIM ** -0.5)).astype(BF16)
    ks[...] = qk_norm(k_ref[0], kg_ref[...]).astype(BF16)
    ones_col = (lax.broadcasted_iota(jnp.int32, (seq, width), 1) == 0).astype(BF16)
    vs[...] = jnp.concatenate([v_ref[0].astype(BF16), ones_col], axis=1)

    lam = lam_ref[...]
    lam_full = (jnp.exp(jnp.sum(lam[0:1] * lam[1:2], axis=-1, keepdims=True))
                - jnp.exp(jnp.sum(lam[2:3] * lam[3:4], axis=-1, keepdims=True)) + LAMBDA_INIT)
    slope = slope_ref[0][:, 0:1]
    first = lax.broadcasted_iota(jnp.int32, (1, width), 1) < DA_HEAD_DIM
    sg = sg_ref[...] * (1.0 - LAMBDA_INIT)

    n_blk = seq // tq
    dist = lax.broadcasted_iota(jnp.int32, (tq, tq), 0) - lax.broadcasted_iota(jnp.int32, (tq, tq), 1)
    for d in range(2 * n_blk - 1):
        bias_ref[d] = slope * jnp.abs(dist - (d - (n_blk - 1)) * tq).astype(F32)

    def body(i, carry):
        r0 = pl.multiple_of(i * tq, tq)
        qt = qs[pl.ds(r0, tq), :]
        q0 = jnp.where(first, qt, jnp.zeros_like(qt))
        q1 = jnp.where(first, jnp.zeros_like(qt), qt)
        kk = ks[...]
        bias = jnp.concatenate([bias_ref[j - i + (n_blk - 1)] for j in range(n_blk)], axis=1)
        s0 = _dot_nt(q0, kk) - bias
        s1 = _dot_nt(q1, kk) - bias
        p0 = jnp.exp(s0 - jnp.max(s0, axis=-1, keepdims=True))
        p1 = jnp.exp(s1 - jnp.max(s1, axis=-1, keepdims=True))
        pv0 = _dot(p0.astype(BF16), vs[...])
        pv1 = _dot(p1.astype(BF16), vs[...])
        w0 = 1.0 / pv0[:, width:width + 1]
        w1 = lam_full / pv1[:, width:width + 1]
        o = pv0[:, :width] * w0 - pv1[:, :width] * w1
        ms = jnp.mean(o * o, axis=-1, keepdims=True)
        o_ref[0, pl.ds(r0, tq), :] = o * lax.rsqrt(ms + NORM_EPS) * sg
        return carry

    lax.fori_loop(0, seq // tq, body, 0)


def _diff_attention(p_att3, qn_g, kn_g, lam, subln_g, tq=256):
    bsz, seq, _ = p_att3.shape
    width = 2 * DA_HEAD_DIM
    slopes = jnp.asarray([2.0 ** (-8.0 * (h + 1) / DA_HEADS) for h in range(DA_HEADS)], F32)
    slopes = jnp.broadcast_to(slopes[:, None, None], (DA_HEADS, 1, width))
    blk = lambda off: pl.BlockSpec((1, seq, width), lambda b, h: (b, 0, off + h))
    full = lambda shape: pl.BlockSpec(shape, lambda b, h: (0,) * len(shape))
    return pl.pallas_call(
        functools.partial(_attn_kernel, tq),
        grid=(bsz, DA_HEADS),
        in_specs=[blk(0), blk(DA_HEADS), blk(2 * DA_HEADS),
                  full((1, width)), full((1, width)), full((4, DA_HEAD_DIM)),
                  pl.BlockSpec((1, 1, width), lambda b, h: (h, 0, 0)),
                  full((1, width))],
        out_specs=pl.BlockSpec((1, seq, width), lambda b, h: (b, 0, h)),
        out_shape=jax.ShapeDtypeStruct((bsz, seq, DA_HEADS * width), F32),
        scratch_shapes=[pltpu.VMEM((seq, width), BF16)] * 2 + [pltpu.VMEM((seq, 2 * width), BF16),
                                                               pltpu.VMEM((2 * (seq // tq) - 1, tq, tq), F32)],
        compiler_params=_params("arbitrary", "arbitrary"),
        name="diff_attention",
    )(p_att3, p_att3, p_att3,
      jnp.tile(qn_g.reshape(1, DA_HEAD_DIM), (1, 2)), jnp.tile(kn_g.reshape(1, DA_HEAD_DIM), (1, 2)),
      lam, slopes, subln_g.reshape(1, width))


def _rwkv_chunk_kernel(n_blocks, cur_ref, prev_ref, next_ref, mu_ref, w0_ref, w2_ref, a0_ref, a2_ref, g2_ref,
                       kk_ref, ka_ref, rk_ref, hs_ref, ra_ref, y0_ref, g_ref, h_ref, bonus_ref, gate_ref):
    ch = RW_CHUNK
    w = RW_WIDTH
    pw = 2 * RW_HEAD
    blk = cur_ref.shape[1]
    n_sub = blk // ch
    block = pl.program_id(1)

    pc = cur_ref[0]
    row = lax.broadcasted_iota(jnp.int32, (blk, 1), 0)
    pv = prev_ref[0][SUBLANES - 1:SUBLANES, :] * (block > 0).astype(F32)
    nx = next_ref[0][0:1, :] * (block < n_blocks - 1).astype(F32)
    prev = jnp.where(row == 0, pv, pltpu.roll(pc, 1, 0))
    nxt = jnp.where(row == blk - 1, nx, pltpu.roll(pc, blk - 1, 0))
    mu = mu_ref[...]
    ps = pc + mu[0:1] * (prev - pc) + mu[1:2] * (nxt - pc)
    r, k, v = ps[:, 0:w], ps[:, w:2 * w], ps[:, 2 * w:3 * w]
    wd = ps[:, 3 * w:3 * w + LORA_COLS]
    ad = ps[:, 3 * w + LORA_COLS:3 * w + 2 * LORA_COLS]
    gd = ps[:, 3 * w + 2 * LORA_COLS:3 * w + 3 * LORA_COLS]

    head_sum = hs_ref[...]

    def per_head_sum(x):
        return _dot(x.astype(BF16), head_sum)

    kkr = k * kk_ref[...]
    kk = kkr / jnp.maximum(jnp.sqrt(per_head_sum(kkr * kkr)), 1e-12)
    bonus_ref[0] = per_head_sum(r * k * rk_ref[...]) * v
    gate_ref[0] = _dot(jax.nn.sigmoid(gd), g2_ref[...])

    ti = lax.broadcasted_iota(jnp.int32, (ch, ch), 0)
    tj = lax.broadcasted_iota(jnp.int32, (ch, ch), 1)
    eye = (ti == tj).astype(F32)
    lane = lax.broadcasted_iota(jnp.int32, (1, pw), 1)
    head_lanes = (lane < RW_HEAD, lane >= RW_HEAD)
    pi = lax.broadcasted_iota(jnp.int32, (pw, pw), 0)
    pj = lax.broadcasted_iota(jnp.int32, (pw, pw), 1)
    same_head = (pi < RW_HEAD) == (pj < RW_HEAD)
    eye_pair = (pi == pj).astype(F32)
    v16 = v.astype(BF16)
    tanh_wd = jnp.tanh(wd)

    probs = []
    pairs = []
    bi = lax.broadcasted_iota(jnp.int32, (blk, blk), 0)
    bj = lax.broadcasted_iota(jnp.int32, (blk, blk), 1)
    same_chunk = (bi >> (ch.bit_length() - 1)) == (bj >> (ch.bit_length() - 1))
    for d in range(2):
        z = w0_ref[d:d + 1] + _dot(tanh_wd, w2_ref[d])
        logdec = -jax.nn.sigmoid(z) * math.exp(-0.5)
        a = jax.nn.sigmoid(a0_ref[d:d + 1] + _dot(ad, a2_ref[d]))
        kd = k * (1.0 + (a - 1.0) * ka_ref[...])
        before = (tj < ti) if d == 0 else (tj > ti)
        upto = before | (tj == ti)
        upto_blk = same_chunk & ((bj <= bi) if d == 0 else (bj >= bi))
        cum = _dot(upto_blk.astype(F32), logdec, HIGHEST)
        w_in, w_ex, w_inv = jnp.exp(cum), jnp.exp(cum - logdec), jnp.exp(-cum)
        a_bar = -kk * w_ex
        b_bar = (kk * a * w_inv).astype(BF16)
        k_bar = (kd * w_inv).astype(BF16)
        r_bar = r * (w_in if d == 0 else w_ex)
        ymask = upto if d == 0 else before
        later, earlier = (ti, tj) if d == 0 else (tj, ti)
        off_masks = []
        for lvl in range(int(math.log2(ch))):
            same_pair = (ti >> (lvl + 1)) == (tj >> (lvl + 1))
            off_masks.append(same_pair & (((later >> lvl) & 1) == 1) & (((earlier >> lvl) & 1) == 0))
        for c in range(n_sub):
            rows = slice(c * ch, (c + 1) * ch)
            last = (c + 1) * ch - 1 if d == 0 else c * ch
            w_tot = w_in[last:last + 1]
            for p in range(RW_PAIRS):
                sl = slice(p * pw, (p + 1) * pw)
                pairs.append(dict(c=c, d=d, p=p, b=b_bar[rows, sl], k=k_bar[rows, sl], v=v16[rows, sl],
                                  r=r_bar[rows, sl], w_tot=w_tot[:, sl]))
                for hh in range(2):
                    mh = head_lanes[hh]
                    probs.append(dict(pair=len(pairs) - 1, before=before, ymask=ymask, off=off_masks,
                                      a=jnp.where(mh, a_bar[rows, sl], 0.0).astype(BF16),
                                      r=jnp.where(mh, r_bar[rows, sl], 0.0).astype(BF16),
                                      v=jnp.where(mh, v16[rows, sl], jnp.zeros_like(v16[rows, sl]))))

    for q in probs:
        pr = pairs[q["pair"]]
        q["ab"] = _dot_nt(q["a"], pr["b"])
        ak = _dot_nt(q["a"], pr["k"])
        q["rb"] = jnp.where(q["ymask"], _dot_nt(q["r"], pr["b"]), 0.0).astype(BF16)
        rk = _dot_nt(q["r"], pr["k"])
        q["ak"] = jnp.where(q["before"], ak, 0.0).astype(BF16)
        q["rk"] = jnp.where(q["ymask"], rk, 0.0).astype(BF16)
    for q in probs:
        q["akv"] = _dot(q["ak"], q["v"])
        q["rkv"] = _dot(q["rk"], q["v"])
        q["inv"] = eye + jnp.where(q["off"][0], q["ab"], 0.0)
    for lvl in range(1, int(math.log2(ch))):
        for q in probs:
            q["tmp"] = _dot(q["inv"].astype(BF16), jnp.where(q["off"][lvl], q["ab"], 0.0).astype(BF16))
        for q in probs:
            q["inv"] = q["inv"] + _dot(q["tmp"].astype(BF16), q["inv"].astype(BF16))
    for q in probs:
        rhs = jnp.concatenate([q["a"], q["akv"].astype(BF16)], axis=1)
        q["x"] = _dot(q["inv"].astype(BF16), rhs)
    for q in probs:
        q["z"] = _dot(q["rb"], q["x"].astype(BF16))
    for i, pr in enumerate(pairs):
        q0, q1 = probs[2 * i], probs[2 * i + 1]
        x = q0["x"] + q1["x"]
        z = q0["z"] + q1["z"]
        ra = pr["r"] + z[:, :pw]
        y0 = z[:, pw:] + q0["rkv"] + q1["rkv"]
        xb = _dot_tn(x.astype(BF16), pr["b"])
        vk = _dot_tn(pr["v"], pr["k"])
        g = (eye_pair + jnp.where(same_head, xb[:pw], 0.0)) * pr["w_tot"]
        h = jnp.where(same_head, xb[pw:] + vk, 0.0) * pr["w_tot"]
        c, d, p = pr["c"], pr["d"], pr["p"]
        sl = slice(p * pw, (p + 1) * pw)
        rows = slice(c * ch, (c + 1) * ch)
        ra_ref[0, d, rows, sl] = ra.astype(BF16)
        y0_ref[0, d, rows, sl] = y0.astype(BF16)
        g_ref[0, d, c, :, sl] = g.astype(BF16)
        h_ref[0, d, c, :, sl] = h.astype(BF16)


def _pad_lora(w2):
    keep = jnp.arange(2)[:, None, None, None] == jnp.arange(2)[None, :, None, None]
    return jnp.where(keep, w2[None], 0.0).reshape(2, -1, w2.shape[-1])


def _rwkv_chunks(p_rw3, mu, w0, w2, a0, a2, g2, k_k, k_a, r_k, chunks_per_step=2):
    bsz, seq, cols = p_rw3.shape
    ch, w, pw = RW_CHUNK, RW_WIDTH, 2 * RW_HEAD
    blk = chunks_per_step * ch
    n_chunks = seq // ch
    n_blocks = seq // blk
    sub_per_blk = blk // SUBLANES
    n_sub = seq // SUBLANES
    full = lambda shape: pl.BlockSpec(shape, lambda b, c: (0,) * len(shape))
    row = lambda x: x.reshape(1, w)
    head_sum = (jnp.arange(w)[:, None] // RW_HEAD == jnp.arange(w)[None, :] // RW_HEAD).astype(BF16)
    tok = lambda dt: jax.ShapeDtypeStruct((bsz, 2, seq, w), dt)
    mat = lambda dt: jax.ShapeDtypeStruct((bsz, 2, n_chunks, pw, w), dt)
    tok_spec = pl.BlockSpec((1, 2, blk, w), lambda b, c: (b, 0, c, 0))
    mat_spec = pl.BlockSpec((1, 2, chunks_per_step, pw, w), lambda b, c: (b, 0, c, 0, 0))
    one_spec = pl.BlockSpec((1, blk, w), lambda b, c: (b, c, 0))
    return pl.pallas_call(
        functools.partial(_rwkv_chunk_kernel, n_blocks),
        grid=(bsz, n_blocks),
        in_specs=[pl.BlockSpec((1, blk, cols), lambda b, c: (b, c, 0)),
                  pl.BlockSpec((1, SUBLANES, cols), lambda b, c: (b, jnp.maximum(c * sub_per_blk - 1, 0), 0)),
                  pl.BlockSpec((1, SUBLANES, cols),
                               lambda b, c: (b, jnp.minimum((c + 1) * sub_per_blk, n_sub - 1), 0)),
                  full((2, cols)), full((2, w)), full((2, LORA_COLS, w)), full((2, w)), full((2, LORA_COLS, w)),
                  full((LORA_COLS, w)), full((1, w)), full((1, w)), full((1, w)), full((w, w))],
        out_specs=[tok_spec, tok_spec, mat_spec, mat_spec, one_spec, one_spec],
        out_shape=[tok(BF16), tok(BF16), mat(BF16), mat(BF16),
                   jax.ShapeDtypeStruct((bsz, seq, w), F32), jax.ShapeDtypeStruct((bsz, seq, w), F32)],
        compiler_params=_params("parallel", "parallel"),
        name="rwkv7_chunks",
    )(p_rw3, p_rw3, p_rw3, mu, w0, _pad_lora(w2), a0, _pad_lora(a2), g2, row(k_k), row(k_a), row(r_k), head_sum)


def _rwkv_scan_kernel(ra0_ref, ra1_ref, y00_ref, y01_ref, g0_ref, g1_ref, h0_ref, h1_ref, o0_ref, o1_ref, st_ref):
    pw = 2 * RW_HEAD

    @pl.when(pl.program_id(1) == 0)
    def _():
        st_ref[...] = jnp.zeros_like(st_ref)

    ch = RW_CHUNK
    n_sub = g0_ref.shape[2]
    dirs = ((ra0_ref, y00_ref, g0_ref, h0_ref, o0_ref), (ra1_ref, y01_ref, g1_ref, h1_ref, o1_ref))
    states = {(d, p): st_ref[d, p] for d in range(2) for p in range(RW_PAIRS)}
    for step in range(n_sub):
        for d, (ra_ref, y0_ref, g_ref, h_ref, o_ref) in enumerate(dirs):
            c = step if d == 0 else n_sub - 1 - step
            rows = slice(c * ch, (c + 1) * ch)
            for p in range(RW_PAIRS):
                sl = slice(p * pw, (p + 1) * pw)
                s16 = states[d, p].astype(BF16)
                o_ref[0, rows, sl] = _dot_nt(ra_ref[0, 0, rows, sl], s16) + y0_ref[0, 0, rows, sl]
                states[d, p] = _dot(s16, g_ref[0, 0, c, :, sl]) + h_ref[0, 0, c, :, sl]
    for (d, p), s in states.items():
        st_ref[d, p] = s


def _rwkv_scan(ra, y0, g, h, n_sub=4):
    bsz, _, seq, w = ra.shape
    ch, pw = RW_CHUNK, 2 * RW_HEAD
    n_blk = seq // (ch * n_sub)
    blk = lambda d, c: c if d == 0 else n_blk - 1 - c
    tok = lambda d: pl.BlockSpec((1, 1, n_sub * ch, w), lambda b, c: (b, d, blk(d, c), 0))
    mat = lambda d: pl.BlockSpec((1, 1, n_sub, pw, w), lambda b, c: (b, d, blk(d, c), 0, 0))
    out = lambda d: pl.BlockSpec((1, n_sub * ch, w), lambda b, c: (b, blk(d, c), 0))
    return pl.pallas_call(
        _rwkv_scan_kernel,
        grid=(bsz, n_blk),
        in_specs=[tok(0), tok(1), tok(0), tok(1), mat(0), mat(1), mat(0), mat(1)],
        out_specs=[out(0), out(1)],
        out_shape=[jax.ShapeDtypeStruct((bsz, seq, w), F32)] * 2,
        scratch_shapes=[pltpu.VMEM((2, RW_PAIRS, pw, pw), F32)],
        compiler_params=_params("parallel", "arbitrary"),
        name="rwkv7_scan",
    )(ra, ra, y0, y0, g, g, h, h)


def _outproj_kernel(att_ref, yf_ref, yb_ref, bonus_ref, gate_ref, lng_ref, lnb_ref, hs_ref, x_ref, mod_ref,
                    wa_ref, wr_ref, g_ref, x1_ref, h2_ref):
    head_sum = hs_ref[...]

    def per_head_mean(v):
        return _dot(v.astype(BF16), head_sum) * (1.0 / RW_HEAD)

    y = yf_ref[...] + yb_ref[...]
    yc = y - per_head_mean(y)
    yn = yc * lax.rsqrt(per_head_mean(yc * yc) + GN_EPS) * lng_ref[...] + lnb_ref[...]
    rw = (yn + bonus_ref[...]) * gate_ref[...]
    acc = _dot(att_ref[...].astype(BF16), wa_ref[...]) + _dot(rw.astype(BF16), wr_ref[...])
    m = mod_ref[0]
    x1 = x_ref[...] + m[2:3] * acc
    x1_ref[...] = x1
    ms = jnp.mean(x1 * x1, axis=-1, keepdims=True)
    yo = x1 * lax.rsqrt(ms + NORM_EPS) * g_ref[...]
    h2_ref[...] = (yo * (1.0 + m[4:5]) + m[3:4]).astype(BF16)


def _out_proj(att2, y_f, y_b, bonus, gate, ln_g, ln_b, x2, mod3, w_a, w_r, g, seq, tm=512):
    t, d = x2.shape
    ka, kr = att2.shape[1], y_f.shape[1]
    per_b = seq // tm
    rows = lambda n: pl.BlockSpec((tm, n), lambda i: (i, 0))
    full = lambda shape: pl.BlockSpec(shape, lambda i: (0,) * len(shape))
    head_sum = (jnp.arange(kr)[:, None] // RW_HEAD == jnp.arange(kr)[None, :] // RW_HEAD).astype(BF16)
    return pl.pallas_call(
        _outproj_kernel,
        grid=(t // tm,),
        in_specs=[rows(ka), rows(kr), rows(kr), rows(kr), rows(kr), full((1, kr)), full((1, kr)), full((kr, kr)),
                  rows(d), pl.BlockSpec((1, 6, d), lambda i: (i // per_b, 0, 0)),
                  full((ka, d)), full((kr, d)), full((1, d))],
        out_specs=[rows(d), rows(d)],
        out_shape=[jax.ShapeDtypeStruct((t, d), F32), jax.ShapeDtypeStruct((t, d), BF16)],
        compiler_params=_params("arbitrary"),
        name="out_proj",
    )(att2, y_f, y_b, bonus, gate, ln_g.reshape(1, kr), ln_b.reshape(1, kr), head_sum, x2, mod3, w_a, w_r,
      g.reshape(1, d))


def _topk_ranks(scores, vals_refs, rank_refs, break_ties):
    nk, tm = scores[0].shape
    kiota = lax.broadcasted_iota(jnp.int32, (nk, tm), 0)

    def body(j, carry):
        out = []
        for (cur, rank), vals_ref in zip(carry, vals_refs):
            m = jnp.max(cur, axis=0, keepdims=True)
            hit = cur == m
            if break_ties:
                hit = kiota == jnp.min(jnp.where(hit, kiota, nk), axis=0, keepdims=True)
            vals_ref[pl.ds(j, 1), :] = m
            out.append((jnp.where(hit, NEG_INF, cur), None if rank is None else jnp.where(hit, j, rank)))
        return tuple(out)

    init = tuple((s, None if ref is None else jnp.full((nk, tm), PEER_TOPK, jnp.int32))
                 for s, ref in zip(scores, rank_refs))
    excess = jnp.zeros((1, tm), F32)
    for (cur, rank), rank_ref in zip(lax.fori_loop(0, PEER_TOPK, body, init), rank_refs):
        if rank_ref is not None:
            rank_ref[...] = rank
        removed = jnp.sum(jnp.where(cur == NEG_INF, 1.0, 0.0), axis=0, keepdims=True)
        excess = jnp.maximum(excess, jnp.abs(removed - PEER_TOPK))
    return jnp.max(excess)


def _pair_candidates():
    pieces = []
    for lvl in range(4):
        cnt = PEER_TOPK // (lvl + 1)
        pieces.append(("row", lvl, lvl, cnt))
        if lvl + 1 < cnt:
            pieces.append(("col", lvl, lvl + 1, cnt))
    return pieces


def _select_pairs(cand, ci, valid, sel_ref, break_ties):
    big = 4 * PEER_TOPK * PEER_TOPK

    def body(j, cur):
        m = jnp.max(cur, axis=0, keepdims=True)
        hit = cur == m
        if break_ties:
            hit = ci == jnp.min(jnp.where(hit, ci, big), axis=0, keepdims=True)
        return jnp.where(hit, NEG_INF, cur)

    cur = lax.fori_loop(0, PEER_TOPK, body, cand)
    sel = jnp.where((cur == NEG_INF) & valid, 1.0, 0.0)
    sel_ref[...] = sel
    return jnp.max(jnp.abs(jnp.sum(sel, axis=0, keepdims=True) - PEER_TOPK))


def _route_kernel(h_ref, wq_ref, keys_ref, p1_ref, cnt_ref, p2_ref, rk2_ref, q_scr, v1_ref, v2_ref, rank_ref,
                  sel_ref, key_ref, kref_ref):
    tm = h_ref.shape[0]
    nk = PEER_NKEYS
    n_grp = tm // LANES
    q = _dot(h_ref[...], wq_ref[...]).astype(BF16)
    for hp in range(2 * PEER_HEADS):
        q_scr[hp] = q[:, hp * nk:(hp + 1) * nk]

    pieces = _pair_candidates()
    big = 4 * PEER_TOPK * PEER_TOPK
    cat = lambda parts: jnp.concatenate(parts, axis=1)

    def head(h, carry):
        s1 = _dot_nt(keys_ref[2 * h], q_scr[2 * h])
        s2 = _dot_nt(keys_ref[2 * h + 1], q_scr[2 * h + 1])
        probs = [s[:, g * LANES:(g + 1) * LANES] for s in (s1, s2) for g in range(n_grp)]
        vals = [ref.at[g] for ref in (v1_ref, v2_ref) for g in range(n_grp)]
        ranks = [rank_ref.at[i] for i in range(2 * n_grp)]
        excess = 0.0
        for i in range(0, len(probs), 3):
            excess = jnp.maximum(excess, _topk_ranks(probs[i:i + 3], vals[i:i + 3], [None] * len(probs[i:i + 3]),
                                                     break_ties=False))
        key_ref[0] = s1
        key_ref[1] = s2
        kref_ref[0] = cat([v1_ref[g] for g in range(n_grp)])
        kref_ref[1] = cat([v2_ref[g] for g in range(n_grp)])

        @pl.when(excess > 0.0)
        def _():
            _topk_ranks(probs, vals, ranks, break_ties=True)
            rank_row = jnp.broadcast_to(lax.broadcasted_iota(jnp.int32, (PEER_TOPK, 1), 0).astype(F32),
                                        (PEER_TOPK, tm))
            for half in range(2):
                key_ref[half] = cat([rank_ref[half * n_grp + g] for g in range(n_grp)]).astype(F32)
                kref_ref[half] = rank_row

        v1 = cat([v1_ref[g] for g in range(n_grp)])
        v2 = cat([v2_ref[g] for g in range(n_grp)])

        cands, cis, valids = [], [], []
        for kind, fixed, lo, hi in pieces:
            rows = SUBLANES * ((hi + SUBLANES - 1) // SUBLANES)
            ridx = lax.broadcasted_iota(jnp.int32, (rows, 1), 0)
            valid = (ridx >= lo) & (ridx < hi)
            if kind == "row":
                vsum = v1[fixed:fixed + 1] + v2[0:rows]
                ci = fixed * PEER_TOPK + ridx
            else:
                vsum = v1[0:rows] + v2[fixed:fixed + 1]
                ci = ridx * PEER_TOPK + fixed
            cands.append(jnp.where(valid, vsum, NEG_INF))
            cis.append(jnp.broadcast_to(jnp.where(valid, ci, big), (rows, tm)))
            valids.append(jnp.broadcast_to(valid, (rows, tm)))
        cand = jnp.concatenate(cands, axis=0)
        ci = jnp.concatenate(cis, axis=0)
        valid = jnp.concatenate(valids, axis=0)
        off = _select_pairs(cand, ci, valid, sel_ref, break_ties=False)

        @pl.when(off > 0.0)
        def _():
            _select_pairs(cand, ci, valid, sel_ref, break_ties=True)

        sel = sel_ref[...]
        zsum = jnp.sum(sel * jnp.exp(cand - (v1[0:1] + v2[0:1])), axis=0, keepdims=True)
        aidx = lax.broadcasted_iota(jnp.int32, (PEER_TOPK, 1), 0)
        per_a = jnp.zeros((PEER_TOPK, tm), F32)
        row0 = 0
        for kind, fixed, lo, hi in pieces:
            rows = SUBLANES * ((hi + SUBLANES - 1) // SUBLANES)
            part = sel[row0:row0 + rows]
            row0 += rows
            if kind == "row":
                per_a = per_a + jnp.where(aidx == fixed, jnp.sum(part, axis=0, keepdims=True), 0.0)
            elif rows == PEER_TOPK:
                per_a = per_a + part
            else:
                per_a = per_a + jnp.concatenate([part, jnp.zeros((PEER_TOPK - rows, tm), F32)], axis=0)
        key1, key2 = key_ref[0], key_ref[1]
        kref1, kref2 = kref_ref[0], kref_ref[1]
        cnt = jnp.zeros((nk, tm), F32)
        rank2 = jnp.full((nk, tm), float(PEER_TOPK), F32)
        for a in range(PEER_TOPK):
            cnt = jnp.where(key1 == kref1[a:a + 1], per_a[a:a + 1], cnt)
            rank2 = jnp.where(key2 == kref2[a:a + 1], float(a), rank2)
        p1_ref[h] = jnp.exp(s1 - v1[0:1]) / (zsum * 2.0 ** 0.5)
        cnt_ref[h] = cnt
        p2_ref[h] = jnp.exp(s2 - v2[0:1]).astype(BF16)
        rk2_ref[h] = rank2.astype(BF16)
        return carry

    lax.fori_loop(0, PEER_HEADS, head, 0)


def _route(h2, wq, keys, tm=512):
    t, d = h2.shape
    nq = wq.shape[1]
    nk = PEER_NKEYS
    out = lambda dt: jax.ShapeDtypeStruct((PEER_HEADS, nk, t), dt)
    ospec = pl.BlockSpec((PEER_HEADS, nk, tm), lambda i: (0, 0, i))
    return pl.pallas_call(
        _route_kernel,
        grid=(t // tm,),
        in_specs=[pl.BlockSpec((tm, d), lambda i: (i, 0)),
                  pl.BlockSpec((d, nq), lambda i: (0, 0)),
                  pl.BlockSpec((2 * PEER_HEADS, nk, nk), lambda i: (0, 0, 0))],
        out_specs=[ospec] * 4,
        out_shape=[out(F32), out(F32), out(BF16), out(BF16)],
        scratch_shapes=[pltpu.VMEM((2 * PEER_HEADS, tm, nk), BF16),
                        pltpu.VMEM((tm // LANES, PEER_TOPK, LANES), F32),
                        pltpu.VMEM((tm // LANES, PEER_TOPK, LANES), F32),
                        pltpu.VMEM((2 * tm // LANES, nk, LANES), jnp.int32),
                        pltpu.VMEM((sum(SUBLANES * ((hi + SUBLANES - 1) // SUBLANES)
                                        for _, _, _, hi in _pair_candidates()), tm), F32),
                        pltpu.VMEM((2, nk, tm), F32), pltpu.VMEM((2, PEER_TOPK, tm), F32)],
        compiler_params=_params("parallel"),
        name="peer_route",
    )(h2, wq, keys)


def _expert_kernel(n_eblk, h_ref, u_ref, vt_ref, p1_ref, cnt_ref, p2_ref, rk2_ref, x1_ref, mod_ref,
                   o_ref, acc_ref, act_ref, gate_ref):
    j = pl.program_id(1)
    te = u_ref.shape[0]
    tm = h_ref.shape[0]
    nk = PEER_NKEYS
    kb_rows = MXU_WIDTH
    per_kb = kb_rows // nk
    tile = 2 * SUBLANES

    @pl.when(j == 0)
    def _():
        acc_ref[...] = jnp.zeros_like(acc_ref)

    hb = h_ref[...]
    n_kb = te // kb_rows
    second_k = 2 * MXU_WIDTH

    def activations(kb):
        rows = slice(kb * kb_rows, (kb + 1) * kb_rows)
        act_ref[rows, :] = _dot_nt(u_ref[rows, :], hb)

    def gates(kb):
        for sb in range(kb * per_kb, (kb + 1) * per_kb):
            e1 = j * (te // nk) + sb
            gate = None
            for h in range(PEER_HEADS):
                c = jnp.broadcast_to(cnt_ref[h, pl.ds(e1, 1), :], (tile, tm)).astype(BF16)
                p = jnp.broadcast_to(p1_ref[h, pl.ds(e1, 1), :], (tile, tm)).astype(BF16)
                rk2 = rk2_ref[h].reshape(nk // tile, tile, tm)
                p2 = p2_ref[h].reshape(nk // tile, tile, tm)
                term = jnp.where(rk2 < c[None], p2, jnp.zeros_like(p2)) * p[None]
                gate = term if gate is None else gate + term
            gate_ref[sb * nk:(sb + 1) * nk, :] = gate.reshape(nk, tm)

    for kb in range(n_kb):
        gates(kb)
    for kb in range(n_kb):
        activations(kb)
    acc = acc_ref[...]
    for c0 in range(0, te, second_k):
        rows = slice(c0, c0 + second_k)
        t = act_ref[rows, :]
        w = gate_ref[rows, :] * (t + t * lax.erf(t)).astype(BF16)
        acc = acc + _dot(vt_ref[:, rows], w)
    acc_ref[...] = acc

    @pl.when(j == n_eblk - 1)
    def _():
        o_ref[...] = x1_ref[...] + mod_ref[0][5:6] * acc_ref[...].T


def _experts(h2, u, vt, p1, cnt, p2, rk2, x1, mod3, seq, tm=512, te=2048):
    t, d = h2.shape
    ne = u.shape[0]
    nk = PEER_NKEYS
    n_eblk = ne // te
    per_b = seq // tm
    rspec = pl.BlockSpec((PEER_HEADS, nk, tm), lambda i, j: (0, 0, i))
    return pl.pallas_call(
        functools.partial(_expert_kernel, n_eblk),
        grid=(t // tm, n_eblk),
        in_specs=[pl.BlockSpec((tm, d), lambda i, j: (i, 0)),
                  pl.BlockSpec((te, d), lambda i, j: (j, 0)),
                  pl.BlockSpec((d, te), lambda i, j: (0, j)),
                  rspec, rspec, rspec, rspec,
                  pl.BlockSpec((tm, d), lambda i, j: (i, 0)),
                  pl.BlockSpec((1, 6, d), lambda i, j: (i // per_b, 0, 0))],
        out_specs=pl.BlockSpec((tm, d), lambda i, j: (i, 0)),
        out_shape=jax.ShapeDtypeStruct((t, d), F32),
        scratch_shapes=[pltpu.VMEM((d, tm), F32), pltpu.VMEM((te, tm), F32), pltpu.VMEM((te, tm), BF16)],
        compiler_params=_params("parallel", "arbitrary"),
        name="peer_experts",
    )(h2, u, vt, p1, cnt, p2, rk2, x1, mod3)


def kernel(x, c, ada_w, ada_b, norm1_g, w_in, da_qnorm_g, da_knorm_g, da_lambda, da_subln_g, rw_shift_mu, rw_w0,
           rw_w2, rw_a0, rw_a2, rw_g2, rw_k_k, rw_k_a, rw_r_k, rw_ln_g, rw_ln_b, w_out, norm2_g, peer_wq,
           peer_keys, peer_u, peer_v):
    bsz, seq, d = x.shape
    depth = ada_w.shape[0]
    t = bsz * seq
    for l in range(depth):
        mod3 = _ada_mod(c, ada_w[l], ada_b[l]).reshape(bsz, 6, d)
        x2 = x.reshape(t, d)
        w_l = w_in[l].astype(BF16)
        p_att, p_rw = _in_proj(x2, mod3, norm1_g[l], w_l[:, :ATT_COLS], w_l[:, ATT_COLS:], seq)
        att = _diff_attention(p_att.reshape(bsz, seq, ATT_COLS), da_qnorm_g[l], da_knorm_g[l], da_lambda[l],
                              da_subln_g[l])
        ra, y0, g, h, bonus, gate = _rwkv_chunks(p_rw.reshape(bsz, seq, RWKV_COLS), rw_shift_mu[l], rw_w0[l],
                                                 rw_w2[l], rw_a0[l], rw_a2[l], rw_g2[l], rw_k_k[l], rw_k_a[l],
                                                 rw_r_k[l].reshape(-1))
        y_f, y_b = _rwkv_scan(ra, y0, g, h)
        wo = w_out[l].astype(BF16)
        aw = att.shape[-1]
        flat = lambda a: a.reshape(t, RW_WIDTH)
        x1, h2 = _out_proj(att.reshape(t, aw), flat(y_f), flat(y_b), flat(bonus), flat(gate), rw_ln_g[l],
                           rw_ln_b[l], x2, mod3, wo[:aw], wo[aw:], norm2_g[l], seq)
        keys = peer_keys[l].reshape(2 * PEER_HEADS, PEER_NKEYS, -1).astype(BF16)
        p1, cnt, p2, rk2 = _route(h2, peer_wq[l].astype(BF16), keys)
        u_scaled = (peer_u[l] * 2.0 ** -0.5).astype(BF16)
        x = _experts(h2, u_scaled, peer_v[l].T.astype(BF16), p1, cnt, p2, rk2, x1, mod3, seq).reshape(bsz, seq, d)
    return x
```

```python
import functools
import math

import jax
import jax.numpy as jnp
from jax import lax
from jax.experimental import pallas as pl
from jax.experimental.pallas import tpu as pltpu

F32 = jnp.float32
BF16 = jnp.bfloat16
HIGHEST = lax.Precision.HIGHEST

LANES = 128
SUBLANES = 8
MXU_WIDTH = 256
VMEM_LIMIT_BYTES = 56 * 1024 * 1024

DA_HEAD_DIM = 64
DA_HEADS = 4
RW_HEAD = 64
RW_HEADS = 8
RW_PAIRS = RW_HEADS // 2
RW_WIDTH = RW_HEADS * RW_HEAD
LORA_COLS = 128
RWKV_COLS = 3 * RW_WIDTH + 3 * LORA_COLS
ATT_COLS = 3 * 2 * DA_HEAD_DIM * DA_HEADS
GN_EPS = 64e-5
NORM_EPS = 1e-6
LAMBDA_INIT = 0.8 - 0.6 * math.exp(-0.3 * 0)
PEER_HEADS = 8
PEER_NKEYS = 128
PEER_TOPK = 16
RW_CHUNK = 64
NEG_INF = float("-inf")


def _params(*sem):
    return pltpu.CompilerParams(dimension_semantics=sem, vmem_limit_bytes=VMEM_LIMIT_BYTES)


def _dot(a, b, precision=None):
    return jnp.dot(a, b, preferred_element_type=F32, precision=precision)


def _dot_nt(a, b, precision=None):
    return lax.dot_general(a, b, (((1,), (1,)), ((), ())), preferred_element_type=F32, precision=precision)


def _dot_tn(a, b, precision=None):
    return lax.dot_general(a, b, (((0,), (0,)), ((), ())), preferred_element_type=F32, precision=precision)


def _group_matrix(n, group, value):
    shift = group.bit_length() - 1
    r = lax.broadcasted_iota(jnp.int32, (n, n), 0) >> shift
    c = lax.broadcasted_iota(jnp.int32, (n, n), 1) >> shift
    return jnp.where(r == c, value, 0.0).astype(F32)


def _ada_kernel(c_ref, w_ref, b_ref, o_ref):
    c = c_ref[...]
    s = c * jax.nn.sigmoid(c)
    o_ref[...] = _dot(s, w_ref[...], HIGHEST) + b_ref[...]


def _ada_mod(c, w, b):
    bsz, d = c.shape
    n = w.shape[1]
    tn = 1024
    return pl.pallas_call(
        _ada_kernel,
        grid=(n // tn,),
        in_specs=[pl.BlockSpec((bsz, d), lambda j: (0, 0)),
                  pl.BlockSpec((d, tn), lambda j: (0, j)),
                  pl.BlockSpec((1, tn), lambda j: (0, j))],
        out_specs=pl.BlockSpec((bsz, tn), lambda j: (0, j)),
        out_shape=jax.ShapeDtypeStruct((bsz, n), F32),
        compiler_params=_params("arbitrary"),
        name="ada_mod",
    )(c, w, b.reshape(1, n))


def _inproj_kernel(x_ref, mod_ref, g_ref, wa_ref, wr_ref, oa_ref, or_ref):
    x = x_ref[...]
    ms = jnp.mean(x * x, axis=-1, keepdims=True)
    y = x * lax.rsqrt(ms + NORM_EPS) * g_ref[...]
    m = mod_ref[0]
    h = (y * (1.0 + m[1:2]) + m[0:1]).astype(BF16)
    oa_ref[...] = _dot(h, wa_ref[...])
    or_ref[...] = _dot(h, wr_ref[...])


def _in_proj(x2, mod3, g, w_att, w_rw, seq, tm=512):
    t, d = x2.shape
    na, nr = w_att.shape[1], w_rw.shape[1]
    per_b = seq // tm
    return pl.pallas_call(
        _inproj_kernel,
        grid=(t // tm,),
        in_specs=[pl.BlockSpec((tm, d), lambda i: (i, 0)),
                  pl.BlockSpec((1, 6, d), lambda i: (i // per_b, 0, 0)),
                  pl.BlockSpec((1, d), lambda i: (0, 0)),
                  pl.BlockSpec((d, na), lambda i: (0, 0)),
                  pl.BlockSpec((d, nr), lambda i: (0, 0))],
        out_specs=[pl.BlockSpec((tm, na), lambda i: (i, 0)),
                   pl.BlockSpec((tm, nr), lambda i: (i, 0))],
        out_shape=[jax.ShapeDtypeStruct((t, na), F32), jax.ShapeDtypeStruct((t, nr), F32)],
        compiler_params=_params("arbitrary"),
        name="in_proj",
    )(x2, mod3, g.reshape(1, d), w_att, w_rw)


def _attn_kernel(tq, q_ref, k_ref, v_ref, qg_ref, kg_ref, lam_ref, slope_ref, sg_ref, o_ref, qs, ks, vs, bias_ref):
    seq = q_ref.shape[1]
    width = 2 * DA_HEAD_DIM
    avg = _group_matrix(width, DA_HEAD_DIM, 1.0 / DA_HEAD_DIM)

    def qk_norm(x, g):
        ms = _dot(x * x, avg)
        return x * lax.rsqrt(ms + NORM_EPS) * g

    qs[...] = (qk_norm(q_ref[0], qg_ref[...]) * (DA_HEAD_DIM ** -0.5)).astype(BF16)
    ks[...] = qk_norm(k_ref[0], kg_ref[...]).astype(BF16)
    ones_col = (lax.broadcasted_iota(jnp.int32, (seq, width), 1) == 0).astype(BF16)
    vs[...] = jnp.concatenate([v_ref[0].astype(BF16), ones_col], axis=1)

    lam = lam_ref[...]
    lam_full = (jnp.exp(jnp.sum(lam[0:1] * lam[1:2], axis=-1, keepdims=True))
                - jnp.exp(jnp.sum(lam[2:3] * lam[3:4], axis=-1, keepdims=True)) + LAMBDA_INIT)
    slope = slope_ref[0][:, 0:1]
    first = lax.broadcasted_iota(jnp.int32, (1, width), 1) < DA_HEAD_DIM
    sg = sg_ref[...] * (1.0 - LAMBDA_INIT)

    n_blk = seq // tq
    dist = lax.broadcasted_iota(jnp.int32, (tq, tq), 0) - lax.broadcasted_iota(jnp.int32, (tq, tq), 1)
    for d in range(2 * n_blk - 1):
        bias_ref[d] = slope * jnp.abs(dist - (d - (n_blk - 1)) * tq).astype(F32)

    def body(i, carry):
        r0 = pl.multiple_of(i * tq, tq)
        qt = qs[pl.ds(r0, tq), :]
        q0 = jnp.where(first, qt, jnp.zeros_like(qt))
        q1 = jnp.where(first, jnp.zeros_like(qt), qt)
        kk = ks[...]
        bias = jnp.concatenate([bias_ref[j - i + (n_blk - 1)] for j in range(n_blk)], axis=1)
        s0 = _dot_nt(q0, kk) - bias
        s1 = _dot_nt(q1, kk) - bias
        p0 = jnp.exp(s0 - jnp.max(s0, axis=-1, keepdims=True))
        p1 = jnp.exp(s1 - jnp.max(s1, axis=-1, keepdims=True))
        pv0 = _dot(p0.astype(BF16), vs[...])
        pv1 = _dot(p1.astype(BF16), vs[...])
        w0 = 1.0 / pv0[:, width:width + 1]
        w1 = lam_full / pv1[:, width:width + 1]
        o = pv0[:, :width] * w0 - pv1[:, :width] * w1
        ms = jnp.mean(o * o, axis=-1, keepdims=True)
        o_ref[0, pl.ds(r0, tq), :] = o * lax.rsqrt(ms + NORM_EPS) * sg
        return carry

    lax.fori_loop(0, seq // tq, body, 0)


def _diff_attention(p_att3, qn_g, kn_g, lam, subln_g, tq=256):
    bsz, seq, _ = p_att3.shape
    width = 2 * DA_HEAD_DIM
    slopes = jnp.asarray([2.0 ** (-8.0 * (h + 1) / DA_HEADS) for h in range(DA_HEADS)], F32)
    slopes = jnp.broadcast_to(slopes[:, None, None], (DA_HEADS, 1, width))
    blk = lambda off: pl.BlockSpec((1, seq, width), lambda b, h: (b, 0, off + h))
    full = lambda shape: pl.BlockSpec(shape, lambda b, h: (0,) * len(shape))
    return pl.pallas_call(
        functools.partial(_attn_kernel, tq),
        grid=(bsz, DA_HEADS),
        in_specs=[blk(0), blk(DA_HEADS), blk(2 * DA_HEADS),
                  full((1, width)), full((1, width)), full((4, DA_HEAD_DIM)),
                  pl.BlockSpec((1, 1, width), lambda b, h: (h, 0, 0)),
                  full((1, width))],
        out_specs=pl.BlockSpec((1, seq, width), lambda b, h: (b, 0, h)),
        out_shape=jax.ShapeDtypeStruct((bsz, seq, DA_HEADS * width), F32),
        scratch_shapes=[pltpu.VMEM((seq, width), BF16)] * 2 + [pltpu.VMEM((seq, 2 * width), BF16),
                                                               pltpu.VMEM((2 * (seq // tq) - 1, tq, tq), F32)],
        compiler_params=_params("arbitrary", "arbitrary"),
        name="diff_attention",
    )(p_att3, p_att3, p_att3,
      jnp.tile(qn_g.reshape(1, DA_HEAD_DIM), (1, 2)), jnp.tile(kn_g.reshape(1, DA_HEAD_DIM), (1, 2)),
      lam, slopes, subln_g.reshape(1, width))


def _rwkv_chunk_kernel(n_blocks, cur_ref, prev_ref, next_ref, mu_ref, w0_ref, w2_ref, a0_ref, a2_ref, g2_ref,
                       kk_ref, ka_ref, rk_ref, hs_ref, ra_ref, y0_ref, g_ref, h_ref, bonus_ref, gate_ref):
    ch = RW_CHUNK
    w = RW_WIDTH
    pw = 2 * RW_HEAD
    blk = cur_ref.shape[1]
    n_sub = blk // ch
    block = pl.program_id(1)

    pc = cur_ref[0]
    row = lax.broadcasted_iota(jnp.int32, (blk, 1), 0)
    pv = prev_ref[0][SUBLANES - 1:SUBLANES, :] * (block > 0).astype(F32)
    nx = next_ref[0][0:1, :] * (block < n_blocks - 1).astype(F32)
    prev = jnp.where(row == 0, pv, pltpu.roll(pc, 1, 0))
    nxt = jnp.where(row == blk - 1, nx, pltpu.roll(pc, blk - 1, 0))
    mu = mu_ref[...]
    ps = pc + mu[0:1] * (prev - pc) + mu[1:2] * (nxt - pc)
    r, k, v = ps[:, 0:w], ps[:, w:2 * w], ps[:, 2 * w:3 * w]
    wd = ps[:, 3 * w:3 * w + LORA_COLS]
    ad = ps[:, 3 * w + LORA_COLS:3 * w + 2 * LORA_COLS]
    gd = ps[:, 3 * w + 2 * LORA_COLS:3 * w + 3 * LORA_COLS]

    head_sum = hs_ref[...]

    def per_head_sum(x):
        return _dot(x.astype(BF16), head_sum)

    kkr = k * kk_ref[...]
    kk = kkr / jnp.maximum(jnp.sqrt(per_head_sum(kkr * kkr)), 1e-12)
    bonus_ref[0] = per_head_sum(r * k * rk_ref[...]) * v
    gate_ref[0] = _dot(jax.nn.sigmoid(gd), g2_ref[...])

    ti = lax.broadcasted_iota(jnp.int32, (ch, ch), 0)
    tj = lax.broadcasted_iota(jnp.int32, (ch, ch), 1)
    eye = (ti == tj).astype(F32)
    lane = lax.broadcasted_iota(jnp.int32, (1, pw), 1)
    head_lanes = (lane < RW_HEAD, lane >= RW_HEAD)
    pi = lax.broadcasted_iota(jnp.int32, (pw, pw), 0)
    pj = lax.broadcasted_iota(jnp.int32, (pw, pw), 1)
    same_head = (pi < RW_HEAD) == (pj < RW_HEAD)
    eye_pair = (pi == pj).astype(F32)
    v16 = v.astype(BF16)
    tanh_wd = jnp.tanh(wd)

    probs = []
    pairs = []
    bi = lax.broadcasted_iota(jnp.int32, (blk, blk), 0)
    bj = lax.broadcasted_iota(jnp.int32, (blk, blk), 1)
    same_chunk = (bi >> (ch.bit_length() - 1)) == (bj >> (ch.bit_length() - 1))
    for d in range(2):
        z = w0_ref[d:d + 1] + _dot(tanh_wd, w2_ref[d])
        logdec = -jax.nn.sigmoid(z) * math.exp(-0.5)
        a = jax.nn.sigmoid(a0_ref[d:d + 1] + _dot(ad, a2_ref[d]))
        kd = k * (1.0 + (a - 1.0) * ka_ref[...])
        before = (tj < ti) if d == 0 else (tj > ti)
        upto = before | (tj == ti)
        upto_blk = same_chunk & ((bj <= bi) if d == 0 else (bj >= bi))
        cum = _dot(upto_blk.astype(F32), logdec, HIGHEST)
        w_in, w_ex, w_inv = jnp.exp(cum), jnp.exp(cum - logdec), jnp.exp(-cum)
        a_bar = -kk * w_ex
        b_bar = (kk * a * w_inv).astype(BF16)
        k_bar = (kd * w_inv).astype(BF16)
        r_bar = r * (w_in if d == 0 else w_ex)
        ymask = upto if d == 0 else before
        later, earlier = (ti, tj) if d == 0 else (tj, ti)
        off_masks = []
        for lvl in range(int(math.log2(ch))):
            same_pair = (ti >> (lvl + 1)) == (tj >> (lvl + 1))
            off_masks.append(same_pair & (((later >> lvl) & 1) == 1) & (((earlier >> lvl) & 1) == 0))
        for c in range(n_sub):
            rows = slice(c * ch, (c + 1) * ch)
            last = (c + 1) * ch - 1 if d == 0 else c * ch
            w_tot = w_in[last:last + 1]
            for p in range(RW_PAIRS):
                sl = slice(p * pw, (p + 1) * pw)
                pairs.append(dict(c=c, d=d, p=p, b=b_bar[rows, sl], k=k_bar[rows, sl], v=v16[rows, sl],
                                  r=r_bar[rows, sl], w_tot=w_tot[:, sl]))
                for hh in range(2):
                    mh = head_lanes[hh]
                    probs.append(dict(pair=len(pairs) - 1, before=before, ymask=ymask, off=off_masks,
                                      a=jnp.where(mh, a_bar[rows, sl], 0.0).astype(BF16),
                                      r=jnp.where(mh, r_bar[rows, sl], 0.0).astype(BF16),
                                      v=jnp.where(mh, v16[rows, sl], jnp.zeros_like(v16[rows, sl]))))

    for q in probs:
        pr = pairs[q["pair"]]
        q["ab"] = _dot_nt(q["a"], pr["b"])
        ak = _dot_nt(q["a"], pr["k"])
        q["rb"] = jnp.where(q["ymask"], _dot_nt(q["r"], pr["b"]), 0.0).astype(BF16)
        rk = _dot_nt(q["r"], pr["k"])
        q["ak"] = jnp.where(q["before"], ak, 0.0).astype(BF16)
        q["rk"] = jnp.where(q["ymask"], rk, 0.0).astype(BF16)
    for q in probs:
        q["akv"] = _dot(q["ak"], q["v"])
        q["rkv"] = _dot(q["rk"], q["v"])
        q["inv"] = eye + jnp.where(q["off"][0], q["ab"], 0.0)
    for lvl in range(1, int(math.log2(ch))):
        for q in probs:
            q["tmp"] = _dot(q["inv"].astype(BF16), jnp.where(q["off"][lvl], q["ab"], 0.0).astype(BF16))
        for q in probs:
            q["inv"] = q["inv"] + _dot(q["tmp"].astype(BF16), q["inv"].astype(BF16))
    for q in probs:
        rhs = jnp.concatenate([q["a"], q["akv"].astype(BF16)], axis=1)
        q["x"] = _dot(q["inv"].astype(BF16), rhs)
    for q in probs:
        q["z"] = _dot(q["rb"], q["x"].astype(BF16))
    for i, pr in enumerate(pairs):
        q0, q1 = probs[2 * i], probs[2 * i + 1]
        x = q0["x"] + q1["x"]
        z = q0["z"] + q1["z"]
        ra = pr["r"] + z[:, :pw]
        y0 = z[:, pw:] + q0["rkv"] + q1["rkv"]
        xb = _dot_tn(x.astype(BF16), pr["b"])
        vk = _dot_tn(pr["v"], pr["k"])
        g = (eye_pair + jnp.where(same_head, xb[:pw], 0.0)) * pr["w_tot"]
        h = jnp.where(same_head, xb[pw:] + vk, 0.0) * pr["w_tot"]
        c, d, p = pr["c"], pr["d"], pr["p"]
        sl = slice(p * pw, (p + 1) * pw)
        rows = slice(c * ch, (c + 1) * ch)
        ra_ref[0, d, rows, sl] = ra.astype(BF16)
        y0_ref[0, d, rows, sl] = y0.astype(BF16)
        g_ref[0, d, c, :, sl] = g.astype(BF16)
        h_ref[0, d, c, :, sl] = h.astype(BF16)


def _pad_lora(w2):
    keep = jnp.arange(2)[:, None, None, None] == jnp.arange(2)[None, :, None, None]
    return jnp.where(keep, w2[None], 0.0).reshape(2, -1, w2.shape[-1])


def _rwkv_chunks(p_rw3, mu, w0, w2, a0, a2, g2, k_k, k_a, r_k, chunks_per_step=2):
    bsz, seq, cols = p_rw3.shape
    ch, w, pw = RW_CHUNK, RW_WIDTH, 2 * RW_HEAD
    blk = chunks_per_step * ch
    n_chunks = seq // ch
    n_blocks = seq // blk
    sub_per_blk = blk // SUBLANES
    n_sub = seq // SUBLANES
    full = lambda shape: pl.BlockSpec(shape, lambda b, c: (0,) * len(shape))
    row = lambda x: x.reshape(1, w)
    head_sum = (jnp.arange(w)[:, None] // RW_HEAD == jnp.arange(w)[None, :] // RW_HEAD).astype(BF16)
    tok = lambda dt: jax.ShapeDtypeStruct((bsz, 2, seq, w), dt)
    mat = lambda dt: jax.ShapeDtypeStruct((bsz, 2, n_chunks, pw, w), dt)
    tok_spec = pl.BlockSpec((1, 2, blk, w), lambda b, c: (b, 0, c, 0))
    mat_spec = pl.BlockSpec((1, 2, chunks_per_step, pw, w), lambda b, c: (b, 0, c, 0, 0))
    one_spec = pl.BlockSpec((1, blk, w), lambda b, c: (b, c, 0))
    return pl.pallas_call(
        functools.partial(_rwkv_chunk_kernel, n_blocks),
        grid=(bsz, n_blocks),
        in_specs=[pl.BlockSpec((1, blk, cols), lambda b, c: (b, c, 0)),
                  pl.BlockSpec((1, SUBLANES, cols), lambda b, c: (b, jnp.maximum(c * sub_per_blk - 1, 0), 0)),
                  pl.BlockSpec((1, SUBLANES, cols),
                               lambda b, c: (b, jnp.minimum((c + 1) * sub_per_blk, n_sub - 1), 0)),
                  full((2, cols)), full((2, w)), full((2, LORA_COLS, w)), full((2, w)), full((2, LORA_COLS, w)),
                  full((LORA_COLS, w)), full((1, w)), full((1, w)), full((1, w)), full((w, w))],
        out_specs=[tok_spec, tok_spec, mat_spec, mat_spec, one_spec, one_spec],
        out_shape=[tok(BF16), tok(BF16), mat(BF16), mat(BF16),
                   jax.ShapeDtypeStruct((bsz, seq, w), F32), jax.ShapeDtypeStruct((bsz, seq, w), F32)],
        compiler_params=_params("parallel", "parallel"),
        name="rwkv7_chunks",
    )(p_rw3, p_rw3, p_rw3, mu, w0, _pad_lora(w2), a0, _pad_lora(a2), g2, row(k_k), row(k_a), row(r_k), head_sum)


def _rwkv_scan_kernel(ra0_ref, ra1_ref, y00_ref, y01_ref, g0_ref, g1_ref, h0_ref, h1_ref, o0_ref, o1_ref, st_ref):
    pw = 2 * RW_HEAD

    @pl.when(pl.program_id(1) == 0)
    def _():
        st_ref[...] = jnp.zeros_like(st_ref)

    ch = RW_CHUNK
    n_sub = g0_ref.shape[2]
    dirs = ((ra0_ref, y00_ref, g0_ref, h0_ref, o0_ref), (ra1_ref, y01_ref, g1_ref, h1_ref, o1_ref))
    states = {(d, p): st_ref[d, p] for d in range(2) for p in range(RW_PAIRS)}
    for step in range(n_sub):
        for d, (ra_ref, y0_ref, g_ref, h_ref, o_ref) in enumerate(dirs):
            c = step if d == 0 else n_sub - 1 - step
            rows = slice(c * ch, (c + 1) * ch)
            for p in range(RW_PAIRS):
                sl = slice(p * pw, (p + 1) * pw)
                s16 = states[d, p].astype(BF16)
                o_ref[0, rows, sl] = _dot_nt(ra_ref[0, 0, rows, sl], s16) + y0_ref[0, 0, rows, sl]
                states[d, p] = _dot(s16, g_ref[0, 0, c, :, sl]) + h_ref[0, 0, c, :, sl]
    for (d, p), s in states.items():
        st_ref[d, p] = s


def _rwkv_scan(ra, y0, g, h, n_sub=4):
    bsz, _, seq, w = ra.shape
    ch, pw = RW_CHUNK, 2 * RW_HEAD
    n_blk = seq // (ch * n_sub)
    blk = lambda d, c: c if d == 0 else n_blk - 1 - c
    tok = lambda d: pl.BlockSpec((1, 1, n_sub * ch, w), lambda b, c: (b, d, blk(d, c), 0))
    mat = lambda d: pl.BlockSpec((1, 1, n_sub, pw, w), lambda b, c: (b, d, blk(d, c), 0, 0))
    out = lambda d: pl.BlockSpec((1, n_sub * ch, w), lambda b, c: (b, blk(d, c), 0))
    return pl.pallas_call(
        _rwkv_scan_kernel,
        grid=(bsz, n_blk),
        in_specs=[tok(0), tok(1), tok(0), tok(1), mat(0), mat(1), mat(0), mat(1)],
        out_specs=[out(0), out(1)],
        out_shape=[jax.ShapeDtypeStruct((bsz, seq, w), F32)] * 2,
        scratch_shapes=[pltpu.VMEM((2, RW_PAIRS, pw, pw), F32)],
        compiler_params=_params("parallel", "arbitrary"),
        name="rwkv7_scan",
    )(ra, ra, y0, y0, g, g, h, h)


def _outproj_kernel(att_ref, yf_ref, yb_ref, bonus_ref, gate_ref, lng_ref, lnb_ref, hs_ref, x_ref, mod_ref,
                    wa_ref, wr_ref, g_ref, x1_ref, h2_ref):
    head_sum = hs_ref[...]

    def per_head_mean(v):
        return _dot(v.astype(BF16), head_sum) * (1.0 / RW_HEAD)

    y = yf_ref[...] + yb_ref[...]
    yc = y - per_head_mean(y)
    yn = yc * lax.rsqrt(per_head_mean(yc * yc) + GN_EPS) * lng_ref[...] + lnb_ref[...]
    rw = (yn + bonus_ref[...]) * gate_ref[...]
    acc = _dot(att_ref[...].astype(BF16), wa_ref[...]) + _dot(rw.astype(BF16), wr_ref[...])
    m = mod_ref[0]
    x1 = x_ref[...] + m[2:3] * acc
    x1_ref[...] = x1
    ms = jnp.mean(x1 * x1, axis=-1, keepdims=True)
    yo = x1 * lax.rsqrt(ms + NORM_EPS) * g_ref[...]
    h2_ref[...] = (yo * (1.0 + m[4:5]) + m[3:4]).astype(BF16)


def _out_proj(att2, y_f, y_b, bonus, gate, ln_g, ln_b, x2, mod3, w_a, w_r, g, seq, tm=512):
    t, d = x2.shape
    ka, kr = att2.shape[1], y_f.shape[1]
    per_b = seq // tm
    rows = lambda n: pl.BlockSpec((tm, n), lambda i: (i, 0))
    full = lambda shape: pl.BlockSpec(shape, lambda i: (0,) * len(shape))
    head_sum = (jnp.arange(kr)[:, None] // RW_HEAD == jnp.arange(kr)[None, :] // RW_HEAD).astype(BF16)
    return pl.pallas_call(
        _outproj_kernel,
        grid=(t // tm,),
        in_specs=[rows(ka), rows(kr), rows(kr), rows(kr), rows(kr), full((1, kr)), full((1, kr)), full((kr, kr)),
                  rows(d), pl.BlockSpec((1, 6, d), lambda i: (i // per_b, 0, 0)),
                  full((ka, d)), full((kr, d)), full((1, d))],
        out_specs=[rows(d), rows(d)],
        out_shape=[jax.ShapeDtypeStruct((t, d), F32), jax.ShapeDtypeStruct((t, d), BF16)],
        compiler_params=_params("arbitrary"),
        name="out_proj",
    )(att2, y_f, y_b, bonus, gate, ln_g.reshape(1, kr), ln_b.reshape(1, kr), head_sum, x2, mod3, w_a, w_r,
      g.reshape(1, d))


def _topk_ranks(scores, vals_refs, rank_refs):
    nk, tm = scores[0].shape
    kiota = lax.broadcasted_iota(jnp.int32, (nk, tm), 0)

    def body(j, carry):
        out = []
        for (cur, rank), vals_ref in zip(carry, vals_refs):
            m = jnp.max(cur, axis=0, keepdims=True)
            hit = kiota == jnp.min(jnp.where(cur == m, kiota, nk), axis=0, keepdims=True)
            vals_ref[pl.ds(j, 1), :] = m
            out.append((jnp.where(hit, NEG_INF, cur), jnp.where(hit, j, rank)))
        return tuple(out)

    init = tuple((s, jnp.full((nk, tm), PEER_TOPK, jnp.int32)) for s in scores)
    for (_, rank), rank_ref in zip(lax.fori_loop(0, PEER_TOPK, body, init), rank_refs):
        rank_ref[...] = rank


def _compare_exchange(rows, i, j):
    a, b = rows[i], rows[j]
    rows[i], rows[j] = jnp.maximum(a, b), jnp.minimum(a, b)


def _bitonic_merge_desc(rows):
    stride = len(rows) // 2
    while stride >= 1:
        for i in range(len(rows)):
            if not i & stride:
                _compare_exchange(rows, i, i + stride)
        stride //= 2


def _bitonic_sort_desc(rows):
    n, size = len(rows), 2
    while size <= n:
        stride = size // 2
        while stride >= 1:
            for i in range(n):
                if not i & stride:
                    lo, hi = (i, i + stride) if (size == n or not i & size) else (i + stride, i)
                    _compare_exchange(rows, lo, hi)
            stride //= 2
        size *= 2


def _top16_sorted(x):
    n = PEER_TOPK
    rows = [x[SUBLANES * r:SUBLANES * (r + 1)] for r in range(x.shape[0] // SUBLANES)]
    rows += [jnp.full_like(rows[0], NEG_INF)] * (n - len(rows))
    _bitonic_sort_desc(rows)
    shift = SUBLANES // 2
    while shift >= 1:
        other = [pltpu.roll(rows[n - 1 - r], shift, 0) for r in range(n)]
        rows = [jnp.maximum(rows[r], other[r]) for r in range(n)]
        _bitonic_merge_desc(rows)
        shift //= 2
    return rows


def _top16_values(s, vals_ref):
    n = PEER_TOPK
    rows = _top16_sorted(s)
    for r in range(n):
        vals_ref[r:r + 1, :] = rows[r][0:1]
    distinct = rows[0] > rows[1]
    for r in range(1, n - 1):
        distinct = distinct & (rows[r] > rows[r + 1])
    at_least = jnp.zeros_like(rows[0])
    for r in range(n):
        at_least = at_least + jnp.where(s[SUBLANES * r:SUBLANES * (r + 1)] >= rows[n - 1], 1.0, 0.0)
    count = jnp.sum(at_least, axis=0, keepdims=True)
    return jnp.where(distinct[0:1] & (count == n), 0.0, 1.0)


def _pair_candidates():
    pieces = []
    for lvl in range(4):
        cnt = PEER_TOPK // (lvl + 1)
        pieces.append(("row", lvl, lvl, cnt))
        if lvl + 1 < cnt:
            pieces.append(("col", lvl, lvl + 1, cnt))
    return pieces


def _select_pairs(cand, ci, valid, sel_ref):
    big = 4 * PEER_TOPK * PEER_TOPK

    def body(j, cur):
        m = jnp.max(cur, axis=0, keepdims=True)
        hit = ci == jnp.min(jnp.where(cur == m, ci, big), axis=0, keepdims=True)
        return jnp.where(hit, NEG_INF, cur)

    cur = lax.fori_loop(0, PEER_TOPK, body, cand)
    sel_ref[...] = jnp.where((cur == NEG_INF) & valid, 1.0, 0.0)


def _select_pairs_by_value(cand, sel_ref):
    worst = 0.0
    for g in range(cand.shape[1] // LANES):
        lanes = slice(g * LANES, (g + 1) * LANES)
        c = cand[:, lanes]
        kth = _top16_sorted(c)[PEER_TOPK - 1][0:1]
        sel = jnp.where(c >= kth, 1.0, 0.0)
        sel_ref[:, lanes] = sel
        worst = jnp.maximum(worst, jnp.max(jnp.abs(jnp.sum(sel, axis=0, keepdims=True) - PEER_TOPK)))
    return worst


def _route_kernel(h_ref, wq_ref, keys_ref, p1_ref, cnt_ref, p2_ref, rk2_ref, q_scr, v1_ref, v2_ref, rank_ref,
                  sel_ref, key_ref, kref_ref):
    tm = h_ref.shape[0]
    nk = PEER_NKEYS
    n_grp = tm // LANES
    q = _dot(h_ref[...], wq_ref[...]).astype(BF16)
    for hp in range(2 * PEER_HEADS):
        q_scr[hp] = q[:, hp * nk:(hp + 1) * nk]

    pieces = _pair_candidates()
    big = 4 * PEER_TOPK * PEER_TOPK
    cat = lambda parts: jnp.concatenate(parts, axis=1)

    def head(h, carry):
        s1 = _dot_nt(keys_ref[2 * h], q_scr[2 * h])
        s2 = _dot_nt(keys_ref[2 * h + 1], q_scr[2 * h + 1])
        probs = [s[:, g * LANES:(g + 1) * LANES] for s in (s1, s2) for g in range(n_grp)]
        vals = [ref.at[g] for ref in (v1_ref, v2_ref) for g in range(n_grp)]
        ranks = [rank_ref.at[i] for i in range(2 * n_grp)]
        ambiguous = jnp.max(cat([_top16_values(prob, val) for prob, val in zip(probs, vals)]))
        key_ref[0] = s1
        key_ref[1] = s2
        kref_ref[0] = cat([v1_ref[g] for g in range(n_grp)])
        kref_ref[1] = cat([v2_ref[g] for g in range(n_grp)])

        @pl.when(ambiguous > 0.0)
        def _():
            _topk_ranks(probs, vals, ranks)
            rank_row = jnp.broadcast_to(lax.broadcasted_iota(jnp.int32, (PEER_TOPK, 1), 0).astype(F32),
                                        (PEER_TOPK, tm))
            for half in range(2):
                key_ref[half] = cat([rank_ref[half * n_grp + g] for g in range(n_grp)]).astype(F32)
                kref_ref[half] = rank_row

        v1 = cat([v1_ref[g] for g in range(n_grp)])
        v2 = cat([v2_ref[g] for g in range(n_grp)])

        cands, cis, valids = [], [], []
        for kind, fixed, lo, hi in pieces:
            rows = SUBLANES * ((hi + SUBLANES - 1) // SUBLANES)
            ridx = lax.broadcasted_iota(jnp.int32, (rows, 1), 0)
            valid = (ridx >= lo) & (ridx < hi)
            if kind == "row":
                vsum = v1[fixed:fixed + 1] + v2[0:rows]
                ci = fixed * PEER_TOPK + ridx
            else:
                vsum = v1[0:rows] + v2[fixed:fixed + 1]
                ci = ridx * PEER_TOPK + fixed
            cands.append(jnp.where(valid, vsum, NEG_INF))
            cis.append(jnp.broadcast_to(jnp.where(valid, ci, big), (rows, tm)))
            valids.append(jnp.broadcast_to(valid, (rows, tm)))
        cand = jnp.concatenate(cands, axis=0)
        ci = jnp.concatenate(cis, axis=0)
        valid = jnp.concatenate(valids, axis=0)
        off = _select_pairs_by_value(cand, sel_ref)

        @pl.when(off > 0.0)
        def _():
            _select_pairs(cand, ci, valid, sel_ref)

        sel = sel_ref[...]
        zsum = jnp.sum(sel * jnp.exp(cand - (v1[0:1] + v2[0:1])), axis=0, keepdims=True)
        aidx = lax.broadcasted_iota(jnp.int32, (PEER_TOPK, 1), 0)
        per_a = jnp.zeros((PEER_TOPK, tm), F32)
        row0 = 0
        for kind, fixed, lo, hi in pieces:
            rows = SUBLANES * ((hi + SUBLANES - 1) // SUBLANES)
            part = sel[row0:row0 + rows]
            row0 += rows
            if kind == "row":
                per_a = per_a + jnp.where(aidx == fixed, jnp.sum(part, axis=0, keepdims=True), 0.0)
            elif rows == PEER_TOPK:
                per_a = per_a + part
            else:
                per_a = per_a + jnp.concatenate([part, jnp.zeros((PEER_TOPK - rows, tm), F32)], axis=0)
        key1, key2 = key_ref[0], key_ref[1]
        kref1, kref2 = kref_ref[0], kref_ref[1]
        cnt = jnp.zeros((nk, tm), F32)
        rank2 = jnp.full((nk, tm), float(PEER_TOPK), F32)
        for a in range(PEER_TOPK):
            cnt = jnp.where(key1 == kref1[a:a + 1], per_a[a:a + 1], cnt)
            rank2 = jnp.where(key2 == kref2[a:a + 1], float(a), rank2)
        p1_ref[h] = jnp.exp(s1 - v1[0:1]) / (zsum * 2.0 ** 0.5)
        cnt_ref[h] = cnt
        p2_ref[h] = jnp.exp(s2 - v2[0:1]).astype(BF16)
        rk2_ref[h] = rank2.astype(BF16)
        return carry

    lax.fori_loop(0, PEER_HEADS, head, 0)


def _route(h2, wq, keys, tm=512):
    t, d = h2.shape
    nq = wq.shape[1]
    nk = PEER_NKEYS
    out = lambda dt: jax.ShapeDtypeStruct((PEER_HEADS, nk, t), dt)
    ospec = pl.BlockSpec((PEER_HEADS, nk, tm), lambda i: (0, 0, i))
    return pl.pallas_call(
        _route_kernel,
        grid=(t // tm,),
        in_specs=[pl.BlockSpec((tm, d), lambda i: (i, 0)),
                  pl.BlockSpec((d, nq), lambda i: (0, 0)),
                  pl.BlockSpec((2 * PEER_HEADS, nk, nk), lambda i: (0, 0, 0))],
        out_specs=[ospec] * 4,
        out_shape=[out(F32), out(F32), out(BF16), out(BF16)],
        scratch_shapes=[pltpu.VMEM((2 * PEER_HEADS, tm, nk), BF16),
                        pltpu.VMEM((tm // LANES, PEER_TOPK, LANES), F32),
                        pltpu.VMEM((tm // LANES, PEER_TOPK, LANES), F32),
                        pltpu.VMEM((2 * tm // LANES, nk, LANES), jnp.int32),
                        pltpu.VMEM((sum(SUBLANES * ((hi + SUBLANES - 1) // SUBLANES)
                                        for _, _, _, hi in _pair_candidates()), tm), F32),
                        pltpu.VMEM((2, nk, tm), F32), pltpu.VMEM((2, PEER_TOPK, tm), F32)],
        compiler_params=_params("parallel"),
        name="peer_route",
    )(h2, wq, keys)


def _expert_kernel(n_eblk, h_ref, u_ref, vt_ref, p1_ref, cnt_ref, p2_ref, rk2_ref, x1_ref, mod_ref,
                   o_ref, acc_ref, act_ref, gate_ref):
    j = pl.program_id(1)
    te = u_ref.shape[0]
    tm = h_ref.shape[0]
    nk = PEER_NKEYS
    kb_rows = MXU_WIDTH
    per_kb = kb_rows // nk
    tile = 2 * SUBLANES

    @pl.when(j == 0)
    def _():
        acc_ref[...] = jnp.zeros_like(acc_ref)

    hb = h_ref[...]
    n_kb = te // kb_rows
    second_k = 2 * MXU_WIDTH

    def activations(kb):
        rows = slice(kb * kb_rows, (kb + 1) * kb_rows)
        act_ref[rows, :] = _dot_nt(u_ref[rows, :], hb)

    def gates(kb):
        for sb in range(kb * per_kb, (kb + 1) * per_kb):
            e1 = j * (te // nk) + sb
            gate = None
            for h in range(PEER_HEADS):
                c = jnp.broadcast_to(cnt_ref[h, pl.ds(e1, 1), :], (tile, tm)).astype(BF16)
                p = jnp.broadcast_to(p1_ref[h, pl.ds(e1, 1), :], (tile, tm)).astype(BF16)
                rk2 = rk2_ref[h].reshape(nk // tile, tile, tm)
                p2 = p2_ref[h].reshape(nk // tile, tile, tm)
                term = jnp.where(rk2 < c[None], p2, jnp.zeros_like(p2)) * p[None]
                gate = term if gate is None else gate + term
            gate_ref[sb * nk:(sb + 1) * nk, :] = gate.reshape(nk, tm)

    for kb in range(n_kb):
        gates(kb)
    for kb in range(n_kb):
        activations(kb)
    acc = acc_ref[...]
    for c0 in range(0, te, second_k):
        rows = slice(c0, c0 + second_k)
        t = act_ref[rows, :]
        w = gate_ref[rows, :] * (t + t * lax.erf(t)).astype(BF16)
        acc = acc + _dot(vt_ref[:, rows], w)
    acc_ref[...] = acc

    @pl.when(j == n_eblk - 1)
    def _():
        o_ref[...] = x1_ref[...] + mod_ref[0][5:6] * acc_ref[...].T


def _experts(h2, u, vt, p1, cnt, p2, rk2, x1, mod3, seq, tm=512, te=2048):
    t, d = h2.shape
    ne = u.shape[0]
    nk = PEER_NKEYS
    n_eblk = ne // te
    per_b = seq // tm
    rspec = pl.BlockSpec((PEER_HEADS, nk, tm), lambda i, j: (0, 0, i))
    return pl.pallas_call(
        functools.partial(_expert_kernel, n_eblk),
        grid=(t // tm, n_eblk),
        in_specs=[pl.BlockSpec((tm, d), lambda i, j: (i, 0)),
                  pl.BlockSpec((te, d), lambda i, j: (j, 0)),
                  pl.BlockSpec((d, te), lambda i, j: (0, j)),
                  rspec, rspec, rspec, rspec,
                  pl.BlockSpec((tm, d), lambda i, j: (i, 0)),
                  pl.BlockSpec((1, 6, d), lambda i, j: (i // per_b, 0, 0))],
        out_specs=pl.BlockSpec((tm, d), lambda i, j: (i, 0)),
        out_shape=jax.ShapeDtypeStruct((t, d), F32),
        scratch_shapes=[pltpu.VMEM((d, tm), F32), pltpu.VMEM((te, tm), F32), pltpu.VMEM((te, tm), BF16)],
        compiler_params=_params("parallel", "arbitrary"),
        name="peer_experts",
    )(h2, u, vt, p1, cnt, p2, rk2, x1, mod3)


def kernel(x, c, ada_w, ada_b, norm1_g, w_in, da_qnorm_g, da_knorm_g, da_lambda, da_subln_g, rw_shift_mu, rw_w0,
           rw_w2, rw_a0, rw_a2, rw_g2, rw_k_k, rw_k_a, rw_r_k, rw_ln_g, rw_ln_b, w_out, norm2_g, peer_wq,
           peer_keys, peer_u, peer_v):
    bsz, seq, d = x.shape
    depth = ada_w.shape[0]
    t = bsz * seq
    for l in range(depth):
        mod3 = _ada_mod(c, ada_w[l], ada_b[l]).reshape(bsz, 6, d)
        x2 = x.reshape(t, d)
        w_l = w_in[l].astype(BF16)
        p_att, p_rw = _in_proj(x2, mod3, norm1_g[l], w_l[:, :ATT_COLS], w_l[:, ATT_COLS:], seq)
        att = _diff_attention(p_att.reshape(bsz, seq, ATT_COLS), da_qnorm_g[l], da_knorm_g[l], da_lambda[l],
                              da_subln_g[l])
        ra, y0, g, h, bonus, gate = _rwkv_chunks(p_rw.reshape(bsz, seq, RWKV_COLS), rw_shift_mu[l], rw_w0[l],
                                                 rw_w2[l], rw_a0[l], rw_a2[l], rw_g2[l], rw_k_k[l], rw_k_a[l],
                                                 rw_r_k[l].reshape(-1))
        y_f, y_b = _rwkv_scan(ra, y0, g, h)
        wo = w_out[l].astype(BF16)
        aw = att.shape[-1]
        flat = lambda a: a.reshape(t, RW_WIDTH)
        x1, h2 = _out_proj(att.reshape(t, aw), flat(y_f), flat(y_b), flat(bonus), flat(gate), rw_ln_g[l],
                           rw_ln_b[l], x2, mod3, wo[:aw], wo[aw:], norm2_g[l], seq)
        keys = peer_keys[l].reshape(2 * PEER_HEADS, PEER_NKEYS, -1).astype(BF16)
        p1, cnt, p2, rk2 = _route(h2, peer_wq[l].astype(BF16), keys)
        u_scaled = (peer_u[l] * 2.0 ** -0.5).astype(BF16)
        x = _experts(h2, u_scaled, peer_v[l].T.astype(BF16), p1, cnt, p2, rk2, x1, mod3, seq).reshape(bsz, seq, d)
    return x
```

```python
import functools
import math

import jax
import jax.numpy as jnp
from jax import lax
from jax.experimental import pallas as pl
from jax.experimental.pallas import tpu as pltpu

F32 = jnp.float32
BF16 = jnp.bfloat16
HIGHEST = lax.Precision.HIGHEST

LANES = 128
SUBLANES = 8
MXU_WIDTH = 256
VMEM_LIMIT_BYTES = 56 * 1024 * 1024

DA_HEAD_DIM = 64
DA_HEADS = 4
RW_HEAD = 64
RW_HEADS = 8
RW_PAIRS = RW_HEADS // 2
RW_WIDTH = RW_HEADS * RW_HEAD
LORA_COLS = 128
RWKV_COLS = 3 * RW_WIDTH + 3 * LORA_COLS
ATT_COLS = 3 * 2 * DA_HEAD_DIM * DA_HEADS
GN_EPS = 64e-5
NORM_EPS = 1e-6
LAMBDA_INIT = 0.8 - 0.6 * math.exp(-0.3 * 0)
PEER_HEADS = 8
PEER_NKEYS = 128
PEER_TOPK = 16
RW_CHUNK = 64
NEG_INF = float("-inf")


def _params(*sem):
    return pltpu.CompilerParams(dimension_semantics=sem, vmem_limit_bytes=VMEM_LIMIT_BYTES)


def _dot(a, b, precision=None):
    return jnp.dot(a, b, preferred_element_type=F32, precision=precision)


def _dot_nt(a, b, precision=None):
    return lax.dot_general(a, b, (((1,), (1,)), ((), ())), preferred_element_type=F32, precision=precision)


def _dot_tn(a, b, precision=None):
    return lax.dot_general(a, b, (((0,), (0,)), ((), ())), preferred_element_type=F32, precision=precision)


def _group_matrix(n, group, value):
    shift = group.bit_length() - 1
    r = lax.broadcasted_iota(jnp.int32, (n, n), 0) >> shift
    c = lax.broadcasted_iota(jnp.int32, (n, n), 1) >> shift
    return jnp.where(r == c, value, 0.0).astype(F32)


def _ada_kernel(c_ref, w_ref, b_ref, o_ref):
    c = c_ref[...]
    s = c * jax.nn.sigmoid(c)
    o_ref[...] = _dot(s, w_ref[...], HIGHEST) + b_ref[...]


def _ada_mod(c, w, b):
    bsz, d = c.shape
    n = w.shape[1]
    tn = 1024
    return pl.pallas_call(
        _ada_kernel,
        grid=(n // tn,),
        in_specs=[pl.BlockSpec((bsz, d), lambda j: (0, 0)),
                  pl.BlockSpec((d, tn), lambda j: (0, j)),
                  pl.BlockSpec((1, tn), lambda j: (0, j))],
        out_specs=pl.BlockSpec((bsz, tn), lambda j: (0, j)),
        out_shape=jax.ShapeDtypeStruct((bsz, n), F32),
        compiler_params=_params("arbitrary"),
        name="ada_mod",
    )(c, w, b.reshape(1, n))


def _inproj_kernel(x_ref, mod_ref, g_ref, wa_ref, wr_ref, oa_ref, or_ref):
    x = x_ref[...]
    ms = jnp.mean(x * x, axis=-1, keepdims=True)
    y = x * lax.rsqrt(ms + NORM_EPS) * g_ref[...]
    m = mod_ref[0]
    h = (y * (1.0 + m[1:2]) + m[0:1]).astype(BF16)
    oa_ref[...] = _dot(h, wa_ref[...])
    or_ref[...] = _dot(h, wr_ref[...])


def _in_proj(x2, mod3, g, w_att, w_rw, seq, tm=512):
    t, d = x2.shape
    na, nr = w_att.shape[1], w_rw.shape[1]
    per_b = seq // tm
    return pl.pallas_call(
        _inproj_kernel,
        grid=(t // tm,),
        in_specs=[pl.BlockSpec((tm, d), lambda i: (i, 0)),
                  pl.BlockSpec((1, 6, d), lambda i: (i // per_b, 0, 0)),
                  pl.BlockSpec((1, d), lambda i: (0, 0)),
                  pl.BlockSpec((d, na), lambda i: (0, 0)),
                  pl.BlockSpec((d, nr), lambda i: (0, 0))],
        out_specs=[pl.BlockSpec((tm, na), lambda i: (i, 0)),
                   pl.BlockSpec((tm, nr), lambda i: (i, 0))],
        out_shape=[jax.ShapeDtypeStruct((t, na), F32), jax.ShapeDtypeStruct((t, nr), F32)],
        compiler_params=_params("arbitrary"),
        name="in_proj",
    )(x2, mod3, g.reshape(1, d), w_att, w_rw)


def _attn_kernel(tq, q_ref, k_ref, v_ref, qg_ref, kg_ref, lam_ref, slope_ref, sg_ref, o_ref, qs, ks, vs, bias_ref):
    seq = q_ref.shape[1]
    width = 2 * DA_HEAD_DIM
    avg = _group_matrix(width, DA_HEAD_DIM, 1.0 / DA_HEAD_DIM)

    def qk_norm(x, g):
        ms = _dot(x * x, avg)
        return x * lax.rsqrt(ms + NORM_EPS) * g

    qs[...] = (qk_norm(q_ref[0], qg_ref[...]) * (DA_HEAD_DIM ** -0.5)).astype(BF16)
    ks[...] = qk_norm(k_ref[0], kg_ref[...]).astype(BF16)
    ones_col = (lax.broadcasted_iota(jnp.int32, (seq, width), 1) == 0).astype(BF16)
    vs[...] = jnp.concatenate([v_ref[0].astype(BF16), ones_col], axis=1)

    lam = lam_ref[...]
    lam_full = (jnp.exp(jnp.sum(lam[0:1] * lam[1:2], axis=-1, keepdims=True))
                - jnp.exp(jnp.sum(lam[2:3] * lam[3:4], axis=-1, keepdims=True)) + LAMBDA_INIT)
    slope = slope_ref[0][:, 0:1]
    first = lax.broadcasted_iota(jnp.int32, (1, width), 1) < DA_HEAD_DIM
    sg = sg_ref[...] * (1.0 - LAMBDA_INIT)

    n_blk = seq // tq
    dist = lax.broadcasted_iota(jnp.int32, (tq, tq), 0) - lax.broadcasted_iota(jnp.int32, (tq, tq), 1)
    for d in range(2 * n_blk - 1):
        bias_ref[d] = slope * jnp.abs(dist - (d - (n_blk - 1)) * tq).astype(F32)

    def body(i, carry):
        r0 = pl.multiple_of(i * tq, tq)
        qt = qs[pl.ds(r0, tq), :]
        q0 = jnp.where(first, qt, jnp.zeros_like(qt))
        q1 = jnp.where(first, jnp.zeros_like(qt), qt)
        kk = ks[...]
        bias = jnp.concatenate([bias_ref[j - i + (n_blk - 1)] for j in range(n_blk)], axis=1)
        s0 = _dot_nt(q0, kk) - bias
        s1 = _dot_nt(q1, kk) - bias
        p0 = jnp.exp(s0 - jnp.max(s0, axis=-1, keepdims=True))
        p1 = jnp.exp(s1 - jnp.max(s1, axis=-1, keepdims=True))
        pv0 = _dot(p0.astype(BF16), vs[...])
        pv1 = _dot(p1.astype(BF16), vs[...])
        w0 = 1.0 / pv0[:, width:width + 1]
        w1 = lam_full / pv1[:, width:width + 1]
        o = pv0[:, :width] * w0 - pv1[:, :width] * w1
        ms = jnp.mean(o * o, axis=-1, keepdims=True)
        o_ref[0, pl.ds(r0, tq), :] = o * lax.rsqrt(ms + NORM_EPS) * sg
        return carry

    lax.fori_loop(0, seq // tq, body, 0)


def _diff_attention(p_att3, qn_g, kn_g, lam, subln_g, tq=256):
    bsz, seq, _ = p_att3.shape
    width = 2 * DA_HEAD_DIM
    slopes = jnp.asarray([2.0 ** (-8.0 * (h + 1) / DA_HEADS) for h in range(DA_HEADS)], F32)
    slopes = jnp.broadcast_to(slopes[:, None, None], (DA_HEADS, 1, width))
    blk = lambda off: pl.BlockSpec((1, seq, width), lambda b, h: (b, 0, off + h))
    full = lambda shape: pl.BlockSpec(shape, lambda b, h: (0,) * len(shape))
    return pl.pallas_call(
        functools.partial(_attn_kernel, tq),
        grid=(bsz, DA_HEADS),
        in_specs=[blk(0), blk(DA_HEADS), blk(2 * DA_HEADS),
                  full((1, width)), full((1, width)), full((4, DA_HEAD_DIM)),
                  pl.BlockSpec((1, 1, width), lambda b, h: (h, 0, 0)),
                  full((1, width))],
        out_specs=pl.BlockSpec((1, seq, width), lambda b, h: (b, 0, h)),
        out_shape=jax.ShapeDtypeStruct((bsz, seq, DA_HEADS * width), F32),
        scratch_shapes=[pltpu.VMEM((seq, width), BF16)] * 2 + [pltpu.VMEM((seq, 2 * width), BF16),
                                                               pltpu.VMEM((2 * (seq // tq) - 1, tq, tq), F32)],
        compiler_params=_params("arbitrary", "arbitrary"),
        name="diff_attention",
    )(p_att3, p_att3, p_att3,
      jnp.tile(qn_g.reshape(1, DA_HEAD_DIM), (1, 2)), jnp.tile(kn_g.reshape(1, DA_HEAD_DIM), (1, 2)),
      lam, slopes, subln_g.reshape(1, width))


def _rwkv_chunk_kernel(n_blocks, cur_ref, prev_ref, next_ref, mu_ref, w0_ref, w2_ref, a0_ref, a2_ref, g2_ref,
                       kk_ref, ka_ref, rk_ref, hs_ref, ra_ref, y0_ref, g_ref, h_ref, bonus_ref, gate_ref):
    ch = RW_CHUNK
    w = RW_WIDTH
    pw = 2 * RW_HEAD
    blk = cur_ref.shape[1]
    n_sub = blk // ch
    block = pl.program_id(1)

    pc = cur_ref[0]
    row = lax.broadcasted_iota(jnp.int32, (blk, 1), 0)
    pv = prev_ref[0][SUBLANES - 1:SUBLANES, :] * (block > 0).astype(F32)
    nx = next_ref[0][0:1, :] * (block < n_blocks - 1).astype(F32)
    prev = jnp.where(row == 0, pv, pltpu.roll(pc, 1, 0))
    nxt = jnp.where(row == blk - 1, nx, pltpu.roll(pc, blk - 1, 0))
    mu = mu_ref[...]
    ps = pc + mu[0:1] * (prev - pc) + mu[1:2] * (nxt - pc)
    r, k, v = ps[:, 0:w], ps[:, w:2 * w], ps[:, 2 * w:3 * w]
    wd = ps[:, 3 * w:3 * w + LORA_COLS]
    ad = ps[:, 3 * w + LORA_COLS:3 * w + 2 * LORA_COLS]
    gd = ps[:, 3 * w + 2 * LORA_COLS:3 * w + 3 * LORA_COLS]

    head_sum = hs_ref[...]

    def per_head_sum(x):
        return _dot(x.astype(BF16), head_sum)

    kkr = k * kk_ref[...]
    kk = kkr / jnp.maximum(jnp.sqrt(per_head_sum(kkr * kkr)), 1e-12)
    bonus_ref[0] = per_head_sum(r * k * rk_ref[...]) * v
    gate_ref[0] = _dot(jax.nn.sigmoid(gd), g2_ref[...])

    ti = lax.broadcasted_iota(jnp.int32, (ch, ch), 0)
    tj = lax.broadcasted_iota(jnp.int32, (ch, ch), 1)
    eye = (ti == tj).astype(F32)
    lane = lax.broadcasted_iota(jnp.int32, (1, pw), 1)
    head_lanes = (lane < RW_HEAD, lane >= RW_HEAD)
    pi = lax.broadcasted_iota(jnp.int32, (pw, pw), 0)
    pj = lax.broadcasted_iota(jnp.int32, (pw, pw), 1)
    same_head = (pi < RW_HEAD) == (pj < RW_HEAD)
    eye_pair = (pi == pj).astype(F32)
    v16 = v.astype(BF16)
    tanh_wd = jnp.tanh(wd)

    probs = []
    pairs = []
    bi = lax.broadcasted_iota(jnp.int32, (blk, blk), 0)
    bj = lax.broadcasted_iota(jnp.int32, (blk, blk), 1)
    same_chunk = (bi >> (ch.bit_length() - 1)) == (bj >> (ch.bit_length() - 1))
    for d in range(2):
        z = w0_ref[d:d + 1] + _dot(tanh_wd, w2_ref[d])
        logdec = -jax.nn.sigmoid(z) * math.exp(-0.5)
        a = jax.nn.sigmoid(a0_ref[d:d + 1] + _dot(ad, a2_ref[d]))
        kd = k * (1.0 + (a - 1.0) * ka_ref[...])
        before = (tj < ti) if d == 0 else (tj > ti)
        upto = before | (tj == ti)
        upto_blk = same_chunk & ((bj <= bi) if d == 0 else (bj >= bi))
        cum = _dot(upto_blk.astype(F32), logdec, HIGHEST)
        w_in, w_ex, w_inv = jnp.exp(cum), jnp.exp(cum - logdec), jnp.exp(-cum)
        a_bar = -kk * w_ex
        b_bar = (kk * a * w_inv).astype(BF16)
        k_bar = (kd * w_inv).astype(BF16)
        r_bar = r * (w_in if d == 0 else w_ex)
        ymask = upto if d == 0 else before
        later, earlier = (ti, tj) if d == 0 else (tj, ti)
        off_masks = []
        for lvl in range(int(math.log2(ch))):
            same_pair = (ti >> (lvl + 1)) == (tj >> (lvl + 1))
            off_masks.append(same_pair & (((later >> lvl) & 1) == 1) & (((earlier >> lvl) & 1) == 0))
        for c in range(n_sub):
            rows = slice(c * ch, (c + 1) * ch)
            last = (c + 1) * ch - 1 if d == 0 else c * ch
            w_tot = w_in[last:last + 1]
            for p in range(RW_PAIRS):
                sl = slice(p * pw, (p + 1) * pw)
                pairs.append(dict(c=c, d=d, p=p, b=b_bar[rows, sl], k=k_bar[rows, sl], v=v16[rows, sl],
                                  r=r_bar[rows, sl], w_tot=w_tot[:, sl]))
                for hh in range(2):
                    mh = head_lanes[hh]
                    probs.append(dict(pair=len(pairs) - 1, before=before, ymask=ymask, off=off_masks,
                                      a=jnp.where(mh, a_bar[rows, sl], 0.0).astype(BF16),
                                      r=jnp.where(mh, r_bar[rows, sl], 0.0).astype(BF16),
                                      v=jnp.where(mh, v16[rows, sl], jnp.zeros_like(v16[rows, sl]))))

    for q in probs:
        pr = pairs[q["pair"]]
        q["ab"] = _dot_nt(q["a"], pr["b"])
        ak = _dot_nt(q["a"], pr["k"])
        q["rb"] = jnp.where(q["ymask"], _dot_nt(q["r"], pr["b"]), 0.0).astype(BF16)
        rk = _dot_nt(q["r"], pr["k"])
        q["ak"] = jnp.where(q["before"], ak, 0.0).astype(BF16)
        q["rk"] = jnp.where(q["ymask"], rk, 0.0).astype(BF16)
    for q in probs:
        q["akv"] = _dot(q["ak"], q["v"])
        q["rkv"] = _dot(q["rk"], q["v"])
        q["inv"] = eye + jnp.where(q["off"][0], q["ab"], 0.0)
    for lvl in range(1, int(math.log2(ch))):
        for q in probs:
            q["tmp"] = _dot(q["inv"].astype(BF16), jnp.where(q["off"][lvl], q["ab"], 0.0).astype(BF16))
        for q in probs:
            q["inv"] = q["inv"] + _dot(q["tmp"].astype(BF16), q["inv"].astype(BF16))
    for q in probs:
        rhs = jnp.concatenate([q["a"], q["akv"].astype(BF16)], axis=1)
        q["x"] = _dot(q["inv"].astype(BF16), rhs)
    for q in probs:
        q["z"] = _dot(q["rb"], q["x"].astype(BF16))
    for i, pr in enumerate(pairs):
        q0, q1 = probs[2 * i], probs[2 * i + 1]
        x = q0["x"] + q1["x"]
        z = q0["z"] + q1["z"]
        ra = pr["r"] + z[:, :pw]
        y0 = z[:, pw:] + q0["rkv"] + q1["rkv"]
        xb = _dot_tn(x.astype(BF16), pr["b"])
        vk = _dot_tn(pr["v"], pr["k"])
        g = (eye_pair + jnp.where(same_head, xb[:pw], 0.0)) * pr["w_tot"]
        h = jnp.where(same_head, xb[pw:] + vk, 0.0) * pr["w_tot"]
        c, d, p = pr["c"], pr["d"], pr["p"]
        sl = slice(p * pw, (p + 1) * pw)
        rows = slice(c * ch, (c + 1) * ch)
        ra_ref[0, d, rows, sl] = ra.astype(BF16)
        y0_ref[0, d, rows, sl] = y0.astype(BF16)
        g_ref[0, d, c, :, sl] = g.astype(BF16)
        h_ref[0, d, c, :, sl] = h.astype(BF16)


def _pad_lora(w2):
    keep = jnp.arange(2)[:, None, None, None] == jnp.arange(2)[None, :, None, None]
    return jnp.where(keep, w2[None], 0.0).reshape(2, -1, w2.shape[-1])


def _rwkv_chunks(p_rw3, mu, w0, w2, a0, a2, g2, k_k, k_a, r_k, chunks_per_step=2):
    bsz, seq, cols = p_rw3.shape
    ch, w, pw = RW_CHUNK, RW_WIDTH, 2 * RW_HEAD
    blk = chunks_per_step * ch
    n_chunks = seq // ch
    n_blocks = seq // blk
    sub_per_blk = blk // SUBLANES
    n_sub = seq // SUBLANES
    full = lambda shape: pl.BlockSpec(shape, lambda b, c: (0,) * len(shape))
    row = lambda x: x.reshape(1, w)
    head_sum = (jnp.arange(w)[:, None] // RW_HEAD == jnp.arange(w)[None, :] // RW_HEAD).astype(BF16)
    tok = lambda dt: jax.ShapeDtypeStruct((bsz, 2, seq, w), dt)
    mat = lambda dt: jax.ShapeDtypeStruct((bsz, 2, n_chunks, pw, w), dt)
    tok_spec = pl.BlockSpec((1, 2, blk, w), lambda b, c: (b, 0, c, 0))
    mat_spec = pl.BlockSpec((1, 2, chunks_per_step, pw, w), lambda b, c: (b, 0, c, 0, 0))
    one_spec = pl.BlockSpec((1, blk, w), lambda b, c: (b, c, 0))
    return pl.pallas_call(
        functools.partial(_rwkv_chunk_kernel, n_blocks),
        grid=(bsz, n_blocks),
        in_specs=[pl.BlockSpec((1, blk, cols), lambda b, c: (b, c, 0)),
                  pl.BlockSpec((1, SUBLANES, cols), lambda b, c: (b, jnp.maximum(c * sub_per_blk - 1, 0), 0)),
                  pl.BlockSpec((1, SUBLANES, cols),
                               lambda b, c: (b, jnp.minimum((c + 1) * sub_per_blk, n_sub - 1), 0)),
                  full((2, cols)), full((2, w)), full((2, LORA_COLS, w)), full((2, w)), full((2, LORA_COLS, w)),
                  full((LORA_COLS, w)), full((1, w)), full((1, w)), full((1, w)), full((w, w))],
        out_specs=[tok_spec, tok_spec, mat_spec, mat_spec, one_spec, one_spec],
        out_shape=[tok(BF16), tok(BF16), mat(BF16), mat(BF16),
                   jax.ShapeDtypeStruct((bsz, seq, w), F32), jax.ShapeDtypeStruct((bsz, seq, w), F32)],
        compiler_params=_params("parallel", "parallel"),
        name="rwkv7_chunks",
    )(p_rw3, p_rw3, p_rw3, mu, w0, _pad_lora(w2), a0, _pad_lora(a2), g2, row(k_k), row(k_a), row(r_k), head_sum)


def _rwkv_scan_kernel(ra0_ref, ra1_ref, y00_ref, y01_ref, g0_ref, g1_ref, h0_ref, h1_ref, o0_ref, o1_ref, st_ref):
    pw = 2 * RW_HEAD

    @pl.when(pl.program_id(1) == 0)
    def _():
        st_ref[...] = jnp.zeros_like(st_ref)

    ch = RW_CHUNK
    n_sub = g0_ref.shape[2]
    dirs = ((ra0_ref, y00_ref, g0_ref, h0_ref, o0_ref), (ra1_ref, y01_ref, g1_ref, h1_ref, o1_ref))
    states = {(d, p): st_ref[d, p] for d in range(2) for p in range(RW_PAIRS)}
    for step in range(n_sub):
        for d, (ra_ref, y0_ref, g_ref, h_ref, o_ref) in enumerate(dirs):
            c = step if d == 0 else n_sub - 1 - step
            rows = slice(c * ch, (c + 1) * ch)
            for p in range(RW_PAIRS):
                sl = slice(p * pw, (p + 1) * pw)
                s16 = states[d, p].astype(BF16)
                o_ref[0, rows, sl] = _dot_nt(ra_ref[0, 0, rows, sl], s16) + y0_ref[0, 0, rows, sl]
                states[d, p] = _dot(s16, g_ref[0, 0, c, :, sl]) + h_ref[0, 0, c, :, sl]
    for (d, p), s in states.items():
        st_ref[d, p] = s


def _rwkv_scan(ra, y0, g, h, n_sub=4):
    bsz, _, seq, w = ra.shape
    ch, pw = RW_CHUNK, 2 * RW_HEAD
    n_blk = seq // (ch * n_sub)
    blk = lambda d, c: c if d == 0 else n_blk - 1 - c
    tok = lambda d: pl.BlockSpec((1, 1, n_sub * ch, w), lambda b, c: (b, d, blk(d, c), 0))
    mat = lambda d: pl.BlockSpec((1, 1, n_sub, pw, w), lambda b, c: (b, d, blk(d, c), 0, 0))
    out = lambda d: pl.BlockSpec((1, n_sub * ch, w), lambda b, c: (b, blk(d, c), 0))
    return pl.pallas_call(
        _rwkv_scan_kernel,
        grid=(bsz, n_blk),
        in_specs=[tok(0), tok(1), tok(0), tok(1), mat(0), mat(1), mat(0), mat(1)],
        out_specs=[out(0), out(1)],
        out_shape=[jax.ShapeDtypeStruct((bsz, seq, w), F32)] * 2,
        scratch_shapes=[pltpu.VMEM((2, RW_PAIRS, pw, pw), F32)],
        compiler_params=_params("parallel", "arbitrary"),
        name="rwkv7_scan",
    )(ra, ra, y0, y0, g, g, h, h)


def _outproj_kernel(att_ref, yf_ref, yb_ref, bonus_ref, gate_ref, lng_ref, lnb_ref, hs_ref, x_ref, mod_ref,
                    wa_ref, wr_ref, g_ref, x1_ref, h2_ref, h2t_ref):
    head_sum = hs_ref[...]

    def per_head_mean(v):
        return _dot(v.astype(BF16), head_sum) * (1.0 / RW_HEAD)

    y = yf_ref[...] + yb_ref[...]
    yc = y - per_head_mean(y)
    yn = yc * lax.rsqrt(per_head_mean(yc * yc) + GN_EPS) * lng_ref[...] + lnb_ref[...]
    rw = (yn + bonus_ref[...]) * gate_ref[...]
    acc = _dot(att_ref[...].astype(BF16), wa_ref[...]) + _dot(rw.astype(BF16), wr_ref[...])
    m = mod_ref[0]
    x1 = x_ref[...] + m[2:3] * acc
    x1_ref[...] = x1
    ms = jnp.mean(x1 * x1, axis=-1, keepdims=True)
    yo = x1 * lax.rsqrt(ms + NORM_EPS) * g_ref[...]
    h2 = yo * (1.0 + m[4:5]) + m[3:4]
    h2_ref[...] = h2.astype(BF16)
    h2t_ref[...] = h2.T.astype(BF16)


def _out_proj(att2, y_f, y_b, bonus, gate, ln_g, ln_b, x2, mod3, w_a, w_r, g, seq, tm=512):
    t, d = x2.shape
    ka, kr = att2.shape[1], y_f.shape[1]
    per_b = seq // tm
    rows = lambda n: pl.BlockSpec((tm, n), lambda i: (i, 0))
    full = lambda shape: pl.BlockSpec(shape, lambda i: (0,) * len(shape))
    head_sum = (jnp.arange(kr)[:, None] // RW_HEAD == jnp.arange(kr)[None, :] // RW_HEAD).astype(BF16)
    return pl.pallas_call(
        _outproj_kernel,
        grid=(t // tm,),
        in_specs=[rows(ka), rows(kr), rows(kr), rows(kr), rows(kr), full((1, kr)), full((1, kr)), full((kr, kr)),
                  rows(d), pl.BlockSpec((1, 6, d), lambda i: (i // per_b, 0, 0)),
                  full((ka, d)), full((kr, d)), full((1, d))],
        out_specs=[rows(d), rows(d), pl.BlockSpec((d, tm), lambda i: (0, i))],
        out_shape=[jax.ShapeDtypeStruct((t, d), F32), jax.ShapeDtypeStruct((t, d), BF16),
                   jax.ShapeDtypeStruct((d, t), BF16)],
        compiler_params=_params("arbitrary"),
        name="out_proj",
    )(att2, y_f, y_b, bonus, gate, ln_g.reshape(1, kr), ln_b.reshape(1, kr), head_sum, x2, mod3, w_a, w_r,
      g.reshape(1, d))


def _topk_ranks(scores, vals_refs, rank_refs):
    nk, tm = scores[0].shape
    kiota = lax.broadcasted_iota(jnp.int32, (nk, tm), 0)

    def body(j, carry):
        out = []
        for (cur, rank), vals_ref in zip(carry, vals_refs):
            m = jnp.max(cur, axis=0, keepdims=True)
            hit = kiota == jnp.min(jnp.where(cur == m, kiota, nk), axis=0, keepdims=True)
            vals_ref[pl.ds(j, 1), :] = m
            out.append((jnp.where(hit, NEG_INF, cur), jnp.where(hit, j, rank)))
        return tuple(out)

    init = tuple((s, jnp.full((nk, tm), PEER_TOPK, jnp.int32)) for s in scores)
    for (_, rank), rank_ref in zip(lax.fori_loop(0, PEER_TOPK, body, init), rank_refs):
        rank_ref[...] = rank


def _compare_exchange(rows, i, j):
    a, b = rows[i], rows[j]
    rows[i], rows[j] = jnp.maximum(a, b), jnp.minimum(a, b)


def _bitonic_merge_desc(rows):
    stride = len(rows) // 2
    while stride >= 1:
        for i in range(len(rows)):
            if not i & stride:
                _compare_exchange(rows, i, i + stride)
        stride //= 2


def _bitonic_sort_desc(rows):
    n, size = len(rows), 2
    while size <= n:
        stride = size // 2
        while stride >= 1:
            for i in range(n):
                if not i & stride:
                    lo, hi = (i, i + stride) if (size == n or not i & size) else (i + stride, i)
                    _compare_exchange(rows, lo, hi)
            stride //= 2
        size *= 2


def _top16_sorted(x):
    n = PEER_TOPK
    rows = [x[SUBLANES * r:SUBLANES * (r + 1)] for r in range(x.shape[0] // SUBLANES)]
    rows += [jnp.full_like(rows[0], NEG_INF)] * (n - len(rows))
    _bitonic_sort_desc(rows)
    shift = SUBLANES // 2
    while shift >= 1:
        other = [pltpu.roll(rows[n - 1 - r], shift, 0) for r in range(n)]
        rows = [jnp.maximum(rows[r], other[r]) for r in range(n)]
        _bitonic_merge_desc(rows)
        shift //= 2
    return rows


def _top16_values(s, vals_ref):
    n = PEER_TOPK
    rows = _top16_sorted(s)
    for r in range(n):
        vals_ref[r:r + 1, :] = rows[r][0:1]
    distinct = rows[0] > rows[1]
    for r in range(1, n - 1):
        distinct = distinct & (rows[r] > rows[r + 1])
    at_least = jnp.zeros_like(rows[0])
    for r in range(n):
        at_least = at_least + jnp.where(s[SUBLANES * r:SUBLANES * (r + 1)] >= rows[n - 1], 1.0, 0.0)
    count = jnp.sum(at_least, axis=0, keepdims=True)
    return jnp.where(distinct[0:1] & (count == n), 0.0, 1.0)


def _pair_candidates():
    pieces = []
    for lvl in range(4):
        cnt = PEER_TOPK // (lvl + 1)
        pieces.append(("row", lvl, lvl, cnt))
        if lvl + 1 < cnt:
            pieces.append(("col", lvl, lvl + 1, cnt))
    return pieces


def _select_pairs(cand, ci, valid, sel_ref):
    big = 4 * PEER_TOPK * PEER_TOPK

    def body(j, cur):
        m = jnp.max(cur, axis=0, keepdims=True)
        hit = ci == jnp.min(jnp.where(cur == m, ci, big), axis=0, keepdims=True)
        return jnp.where(hit, NEG_INF, cur)

    cur = lax.fori_loop(0, PEER_TOPK, body, cand)
    sel_ref[...] = jnp.where((cur == NEG_INF) & valid, 1.0, 0.0)


def _select_pairs_by_value(cand, sel_ref):
    worst = 0.0
    for g in range(cand.shape[1] // LANES):
        lanes = slice(g * LANES, (g + 1) * LANES)
        c = cand[:, lanes]
        kth = _top16_sorted(c)[PEER_TOPK - 1][0:1]
        sel = jnp.where(c >= kth, 1.0, 0.0)
        sel_ref[:, lanes] = sel
        worst = jnp.maximum(worst, jnp.max(jnp.abs(jnp.sum(sel, axis=0, keepdims=True) - PEER_TOPK)))
    return worst


def _route_kernel(h_ref, wq_ref, keys_ref, p1_ref, cnt_ref, p2_ref, rk2_ref, q_scr, v1_ref, v2_ref, rank_ref,
                  sel_ref, key_ref, kref_ref):
    tm = h_ref.shape[0]
    nk = PEER_NKEYS
    n_grp = tm // LANES
    q = _dot(h_ref[...], wq_ref[...]).astype(BF16)
    for hp in range(2 * PEER_HEADS):
        q_scr[hp] = q[:, hp * nk:(hp + 1) * nk]

    pieces = _pair_candidates()
    big = 4 * PEER_TOPK * PEER_TOPK
    cat = lambda parts: jnp.concatenate(parts, axis=1)

    def head(h, carry):
        s1 = _dot_nt(keys_ref[2 * h], q_scr[2 * h])
        s2 = _dot_nt(keys_ref[2 * h + 1], q_scr[2 * h + 1])
        probs = [s[:, g * LANES:(g + 1) * LANES] for s in (s1, s2) for g in range(n_grp)]
        vals = [ref.at[g] for ref in (v1_ref, v2_ref) for g in range(n_grp)]
        ranks = [rank_ref.at[i] for i in range(2 * n_grp)]
        ambiguous = jnp.max(cat([_top16_values(prob, val) for prob, val in zip(probs, vals)]))
        key_ref[0] = s1
        key_ref[1] = s2
        kref_ref[0] = cat([v1_ref[g] for g in range(n_grp)])
        kref_ref[1] = cat([v2_ref[g] for g in range(n_grp)])

        @pl.when(ambiguous > 0.0)
        def _():
            _topk_ranks(probs, vals, ranks)
            rank_row = jnp.broadcast_to(lax.broadcasted_iota(jnp.int32, (PEER_TOPK, 1), 0).astype(F32),
                                        (PEER_TOPK, tm))
            for half in range(2):
                key_ref[half] = cat([rank_ref[half * n_grp + g] for g in range(n_grp)]).astype(F32)
                kref_ref[half] = rank_row

        v1 = cat([v1_ref[g] for g in range(n_grp)])
        v2 = cat([v2_ref[g] for g in range(n_grp)])

        cands, cis, valids = [], [], []
        for kind, fixed, lo, hi in pieces:
            rows = SUBLANES * ((hi + SUBLANES - 1) // SUBLANES)
            ridx = lax.broadcasted_iota(jnp.int32, (rows, 1), 0)
            valid = (ridx >= lo) & (ridx < hi)
            if kind == "row":
                vsum = v1[fixed:fixed + 1] + v2[0:rows]
                ci = fixed * PEER_TOPK + ridx
            else:
                vsum = v1[0:rows] + v2[fixed:fixed + 1]
                ci = ridx * PEER_TOPK + fixed
            cands.append(jnp.where(valid, vsum, NEG_INF))
            cis.append(jnp.broadcast_to(jnp.where(valid, ci, big), (rows, tm)))
            valids.append(jnp.broadcast_to(valid, (rows, tm)))
        cand = jnp.concatenate(cands, axis=0)
        ci = jnp.concatenate(cis, axis=0)
        valid = jnp.concatenate(valids, axis=0)
        off = _select_pairs_by_value(cand, sel_ref)

        @pl.when(off > 0.0)
        def _():
            _select_pairs(cand, ci, valid, sel_ref)

        sel = sel_ref[...]
        zsum = jnp.sum(sel * jnp.exp(cand - (v1[0:1] + v2[0:1])), axis=0, keepdims=True)
        aidx = lax.broadcasted_iota(jnp.int32, (PEER_TOPK, 1), 0)
        per_a = jnp.zeros((PEER_TOPK, tm), F32)
        row0 = 0
        for kind, fixed, lo, hi in pieces:
            rows = SUBLANES * ((hi + SUBLANES - 1) // SUBLANES)
            part = sel[row0:row0 + rows]
            row0 += rows
            if kind == "row":
                per_a = per_a + jnp.where(aidx == fixed, jnp.sum(part, axis=0, keepdims=True), 0.0)
            elif rows == PEER_TOPK:
                per_a = per_a + part
            else:
                per_a = per_a + jnp.concatenate([part, jnp.zeros((PEER_TOPK - rows, tm), F32)], axis=0)
        key1, key2 = key_ref[0], key_ref[1]
        kref1, kref2 = kref_ref[0], kref_ref[1]
        cnt = jnp.zeros((nk, tm), F32)
        rank2 = jnp.full((nk, tm), float(PEER_TOPK), F32)
        for a in range(PEER_TOPK):
            cnt = jnp.where(key1 == kref1[a:a + 1], per_a[a:a + 1], cnt)
            rank2 = jnp.where(key2 == kref2[a:a + 1], float(a), rank2)
        p1_ref[h] = jnp.exp(s1 - v1[0:1]) / (zsum * 2.0 ** 0.5)
        cnt_ref[h] = cnt
        p2_ref[h] = jnp.exp(s2 - v2[0:1]).astype(BF16)
        rk2_ref[h] = rank2.astype(BF16)
        return carry

    lax.fori_loop(0, PEER_HEADS, head, 0)


def _route(h2, wq, keys, tm=512):
    t, d = h2.shape
    nq = wq.shape[1]
    nk = PEER_NKEYS
    out = lambda dt: jax.ShapeDtypeStruct((PEER_HEADS, nk, t), dt)
    ospec = pl.BlockSpec((PEER_HEADS, nk, tm), lambda i: (0, 0, i))
    return pl.pallas_call(
        _route_kernel,
        grid=(t // tm,),
        in_specs=[pl.BlockSpec((tm, d), lambda i: (i, 0)),
                  pl.BlockSpec((d, nq), lambda i: (0, 0)),
                  pl.BlockSpec((2 * PEER_HEADS, nk, nk), lambda i: (0, 0, 0))],
        out_specs=[ospec] * 4,
        out_shape=[out(F32), out(F32), out(BF16), out(BF16)],
        scratch_shapes=[pltpu.VMEM((2 * PEER_HEADS, tm, nk), BF16),
                        pltpu.VMEM((tm // LANES, PEER_TOPK, LANES), F32),
                        pltpu.VMEM((tm // LANES, PEER_TOPK, LANES), F32),
                        pltpu.VMEM((2 * tm // LANES, nk, LANES), jnp.int32),
                        pltpu.VMEM((sum(SUBLANES * ((hi + SUBLANES - 1) // SUBLANES)
                                        for _, _, _, hi in _pair_candidates()), tm), F32),
                        pltpu.VMEM((2, nk, tm), F32), pltpu.VMEM((2, PEER_TOPK, tm), F32)],
        compiler_params=_params("parallel"),
        name="peer_route",
    )(h2, wq, keys)


def _expert_kernel(n_eblk, h_ref, u_ref, vt_ref, p1_ref, cnt_ref, p2_ref, rk2_ref, x1_ref, mod_ref,
                   o_ref, acc_ref, act_ref, gate_ref):
    j = pl.program_id(1)
    te = u_ref.shape[0]
    tm = h_ref.shape[1]
    nk = PEER_NKEYS
    kb_rows = MXU_WIDTH
    per_kb = kb_rows // nk
    tile = 2 * SUBLANES

    @pl.when(j == 0)
    def _():
        acc_ref[...] = jnp.zeros_like(acc_ref)

    hb = h_ref[...]
    n_kb = te // kb_rows
    second_k = 2 * MXU_WIDTH

    first_rows = 2 * MXU_WIDTH

    def gates(kb):
        for sb in range(kb * per_kb, (kb + 1) * per_kb):
            e1 = j * (te // nk) + sb
            gate = None
            for h in range(PEER_HEADS):
                c = jnp.broadcast_to(cnt_ref[h, pl.ds(e1, 1), :], (tile, tm)).astype(BF16)
                p = jnp.broadcast_to(p1_ref[h, pl.ds(e1, 1), :], (tile, tm)).astype(BF16)
                rk2 = rk2_ref[h].reshape(nk // tile, tile, tm)
                p2 = p2_ref[h].reshape(nk // tile, tile, tm)
                term = jnp.where(rk2 < c[None], p2, jnp.zeros_like(p2)) * p[None]
                gate = term if gate is None else gate + term
            gate_ref[sb * nk:(sb + 1) * nk, :] = gate.reshape(nk, tm)

    for kb in range(n_kb):
        gates(kb)
    for r0 in range(0, te, first_rows):
        act_ref[r0:r0 + first_rows, :] = _dot(u_ref[r0:r0 + first_rows, :], hb)
    acc = acc_ref[...]
    for c0 in range(0, te, second_k):
        rows = slice(c0, c0 + second_k)
        t = act_ref[rows, :]
        w = gate_ref[rows, :] * (t + t * lax.erf(t)).astype(BF16)
        acc = acc + _dot_tn(vt_ref[rows, :], w)
    acc_ref[...] = acc

    @pl.when(j == n_eblk - 1)
    def _():
        o_ref[...] = x1_ref[...] + mod_ref[0][5:6] * acc_ref[...].T


def _experts(h2t, u, vt, p1, cnt, p2, rk2, x1, mod3, seq, tm=512, te=2048):
    d, t = h2t.shape
    ne = u.shape[0]
    nk = PEER_NKEYS
    n_eblk = ne // te
    per_b = seq // tm
    rspec = pl.BlockSpec((PEER_HEADS, nk, tm), lambda i, j: (0, 0, i))
    return pl.pallas_call(
        functools.partial(_expert_kernel, n_eblk),
        grid=(t // tm, n_eblk),
        in_specs=[pl.BlockSpec((d, tm), lambda i, j: (0, i)),
                  pl.BlockSpec((te, d), lambda i, j: (j, 0)),
                  pl.BlockSpec((te, d), lambda i, j: (j, 0)),
                  rspec, rspec, rspec, rspec,
                  pl.BlockSpec((tm, d), lambda i, j: (i, 0)),
                  pl.BlockSpec((1, 6, d), lambda i, j: (i // per_b, 0, 0))],
        out_specs=pl.BlockSpec((tm, d), lambda i, j: (i, 0)),
        out_shape=jax.ShapeDtypeStruct((t, d), F32),
        scratch_shapes=[pltpu.VMEM((d, tm), F32), pltpu.VMEM((te, tm), F32), pltpu.VMEM((te, tm), BF16)],
        compiler_params=_params("parallel", "arbitrary"),
        name="peer_experts",
    )(h2t, u, vt, p1, cnt, p2, rk2, x1, mod3)


def kernel(x, c, ada_w, ada_b, norm1_g, w_in, da_qnorm_g, da_knorm_g, da_lambda, da_subln_g, rw_shift_mu, rw_w0,
           rw_w2, rw_a0, rw_a2, rw_g2, rw_k_k, rw_k_a, rw_r_k, rw_ln_g, rw_ln_b, w_out, norm2_g, peer_wq,
           peer_keys, peer_u, peer_v):
    bsz, seq, d = x.shape
    depth = ada_w.shape[0]
    t = bsz * seq
    for l in range(depth):
        mod3 = _ada_mod(c, ada_w[l], ada_b[l]).reshape(bsz, 6, d)
        x2 = x.reshape(t, d)
        w_l = w_in[l].astype(BF16)
        p_att, p_rw = _in_proj(x2, mod3, norm1_g[l], w_l[:, :ATT_COLS], w_l[:, ATT_COLS:], seq)
        att = _diff_attention(p_att.reshape(bsz, seq, ATT_COLS), da_qnorm_g[l], da_knorm_g[l], da_lambda[l],
                              da_subln_g[l])
        ra, y0, g, h, bonus, gate = _rwkv_chunks(p_rw.reshape(bsz, seq, RWKV_COLS), rw_shift_mu[l], rw_w0[l],
                                                 rw_w2[l], rw_a0[l], rw_a2[l], rw_g2[l], rw_k_k[l], rw_k_a[l],
                                                 rw_r_k[l].reshape(-1))
        y_f, y_b = _rwkv_scan(ra, y0, g, h)
        wo = w_out[l].astype(BF16)
        aw = att.shape[-1]
        flat = lambda a: a.reshape(t, RW_WIDTH)
        x1, h2, h2t = _out_proj(att.reshape(t, aw), flat(y_f), flat(y_b), flat(bonus), flat(gate), rw_ln_g[l],
                                rw_ln_b[l], x2, mod3, wo[:aw], wo[aw:], norm2_g[l], seq)
        keys = peer_keys[l].reshape(2 * PEER_HEADS, PEER_NKEYS, -1).astype(BF16)
        p1, cnt, p2, rk2 = _route(h2, peer_wq[l].astype(BF16), keys)
        u_scaled = (peer_u[l] * 2.0 ** -0.5).astype(BF16)
        x = _experts(h2t, u_scaled, peer_v[l].astype(BF16), p1, cnt, p2, rk2, x1, mod3, seq).reshape(bsz, seq, d)
    return x
```

```python
import functools
import math

import jax
import jax.numpy as jnp
from jax import lax
from jax.experimental import pallas as pl
from jax.experimental.pallas import tpu as pltpu

F32 = jnp.float32
BF16 = jnp.bfloat16
HIGHEST = lax.Precision.HIGHEST

LANES = 128
SUBLANES = 8
MXU_WIDTH = 256
VMEM_LIMIT_BYTES = 56 * 1024 * 1024

DA_HEAD_DIM = 64
DA_HEADS = 4
RW_HEAD = 64
RW_HEADS = 8
RW_PAIRS = RW_HEADS // 2
RW_WIDTH = RW_HEADS * RW_HEAD
LORA_COLS = 128
RWKV_COLS = 3 * RW_WIDTH + 3 * LORA_COLS
ATT_COLS = 3 * 2 * DA_HEAD_DIM * DA_HEADS
GN_EPS = 64e-5
NORM_EPS = 1e-6
LAMBDA_INIT = 0.8 - 0.6 * math.exp(-0.3 * 0)
PEER_HEADS = 8
PEER_NKEYS = 128
PEER_TOPK = 16
RW_CHUNK = 64
NEG_INF = float("-inf")


def _params(*sem):
    return pltpu.CompilerParams(dimension_semantics=sem, vmem_limit_bytes=VMEM_LIMIT_BYTES)


def _dot(a, b, precision=None):
    return jnp.dot(a, b, preferred_element_type=F32, precision=precision)


def _dot_nt(a, b, precision=None):
    return lax.dot_general(a, b, (((1,), (1,)), ((), ())), preferred_element_type=F32, precision=precision)


def _dot_tn(a, b, precision=None):
    return lax.dot_general(a, b, (((0,), (0,)), ((), ())), preferred_element_type=F32, precision=precision)


def _group_matrix(n, group, value):
    shift = group.bit_length() - 1
    r = lax.broadcasted_iota(jnp.int32, (n, n), 0) >> shift
    c = lax.broadcasted_iota(jnp.int32, (n, n), 1) >> shift
    return jnp.where(r == c, value, 0.0).astype(F32)


def _ada_kernel(c_ref, w_ref, b_ref, o_ref):
    c = c_ref[...]
    s = c * jax.nn.sigmoid(c)
    o_ref[...] = _dot(s, w_ref[...], HIGHEST) + b_ref[...]


def _ada_mod(c, w, b):
    bsz, d = c.shape
    n = w.shape[1]
    tn = 1024
    return pl.pallas_call(
        _ada_kernel,
        grid=(n // tn,),
        in_specs=[pl.BlockSpec((bsz, d), lambda j: (0, 0)),
                  pl.BlockSpec((d, tn), lambda j: (0, j)),
                  pl.BlockSpec((1, tn), lambda j: (0, j))],
        out_specs=pl.BlockSpec((bsz, tn), lambda j: (0, j)),
        out_shape=jax.ShapeDtypeStruct((bsz, n), F32),
        compiler_params=_params("arbitrary"),
        name="ada_mod",
    )(c, w, b.reshape(1, n))


def _inproj_kernel(x_ref, mod_ref, g_ref, wa_ref, wr_ref, oa_ref, or_ref):
    x = x_ref[...]
    ms = jnp.mean(x * x, axis=-1, keepdims=True)
    y = x * lax.rsqrt(ms + NORM_EPS) * g_ref[...]
    m = mod_ref[0]
    h = (y * (1.0 + m[1:2]) + m[0:1]).astype(BF16)
    oa_ref[...] = _dot(h, wa_ref[...])
    or_ref[...] = _dot(h, wr_ref[...])


def _in_proj(x2, mod3, g, w_att, w_rw, seq, tm=512):
    t, d = x2.shape
    na, nr = w_att.shape[1], w_rw.shape[1]
    per_b = seq // tm
    return pl.pallas_call(
        _inproj_kernel,
        grid=(t // tm,),
        in_specs=[pl.BlockSpec((tm, d), lambda i: (i, 0)),
                  pl.BlockSpec((1, 6, d), lambda i: (i // per_b, 0, 0)),
                  pl.BlockSpec((1, d), lambda i: (0, 0)),
                  pl.BlockSpec((d, na), lambda i: (0, 0)),
                  pl.BlockSpec((d, nr), lambda i: (0, 0))],
        out_specs=[pl.BlockSpec((tm, na), lambda i: (i, 0)),
                   pl.BlockSpec((tm, nr), lambda i: (i, 0))],
        out_shape=[jax.ShapeDtypeStruct((t, na), F32), jax.ShapeDtypeStruct((t, nr), F32)],
        compiler_params=_params("arbitrary"),
        name="in_proj",
    )(x2, mod3, g.reshape(1, d), w_att, w_rw)


def _attn_kernel(tq, q_ref, k_ref, v_ref, qg_ref, kg_ref, lam_ref, slope_ref, sg_ref, o_ref, qs, ks, vs, bias_ref):
    seq = q_ref.shape[1]
    width = 2 * DA_HEAD_DIM
    avg = _group_matrix(width, DA_HEAD_DIM, 1.0 / DA_HEAD_DIM)

    def qk_norm(x, g):
        ms = _dot(x * x, avg)
        return x * lax.rsqrt(ms + NORM_EPS) * g

    qs[...] = (qk_norm(q_ref[0], qg_ref[...]) * (DA_HEAD_DIM ** -0.5)).astype(BF16)
    ks[...] = qk_norm(k_ref[0], kg_ref[...]).astype(BF16)
    ones_col = (lax.broadcasted_iota(jnp.int32, (seq, width), 1) == 0).astype(BF16)
    vs[...] = jnp.concatenate([v_ref[0].astype(BF16), ones_col], axis=1)

    lam = lam_ref[...]
    lam_full = (jnp.exp(jnp.sum(lam[0:1] * lam[1:2], axis=-1, keepdims=True))
                - jnp.exp(jnp.sum(lam[2:3] * lam[3:4], axis=-1, keepdims=True)) + LAMBDA_INIT)
    slope = slope_ref[0][:, 0:1]
    first = lax.broadcasted_iota(jnp.int32, (1, width), 1) < DA_HEAD_DIM
    sg = sg_ref[...] * (1.0 - LAMBDA_INIT)

    n_blk = seq // tq
    dist = lax.broadcasted_iota(jnp.int32, (tq, tq), 0) - lax.broadcasted_iota(jnp.int32, (tq, tq), 1)
    for d in range(2 * n_blk - 1):
        bias_ref[d] = slope * jnp.abs(dist - (d - (n_blk - 1)) * tq).astype(F32)

    def body(i, carry):
        r0 = pl.multiple_of(i * tq, tq)
        qt = qs[pl.ds(r0, tq), :]
        q0 = jnp.where(first, qt, jnp.zeros_like(qt))
        q1 = jnp.where(first, jnp.zeros_like(qt), qt)
        kk = ks[...]
        bias = jnp.concatenate([bias_ref[j - i + (n_blk - 1)] for j in range(n_blk)], axis=1)
        s0 = _dot_nt(q0, kk) - bias
        s1 = _dot_nt(q1, kk) - bias
        p0 = jnp.exp(s0 - jnp.max(s0, axis=-1, keepdims=True))
        p1 = jnp.exp(s1 - jnp.max(s1, axis=-1, keepdims=True))
        pv0 = _dot(p0.astype(BF16), vs[...])
        pv1 = _dot(p1.astype(BF16), vs[...])
        w0 = 1.0 / pv0[:, width:width + 1]
        w1 = lam_full / pv1[:, width:width + 1]
        o = pv0[:, :width] * w0 - pv1[:, :width] * w1
        ms = jnp.mean(o * o, axis=-1, keepdims=True)
        o_ref[0, pl.ds(r0, tq), :] = o * lax.rsqrt(ms + NORM_EPS) * sg
        return carry

    lax.fori_loop(0, seq // tq, body, 0)


def _diff_attention(p_att3, qn_g, kn_g, lam, subln_g, tq=256):
    bsz, seq, _ = p_att3.shape
    width = 2 * DA_HEAD_DIM
    slopes = jnp.asarray([2.0 ** (-8.0 * (h + 1) / DA_HEADS) for h in range(DA_HEADS)], F32)
    slopes = jnp.broadcast_to(slopes[:, None, None], (DA_HEADS, 1, width))
    blk = lambda off: pl.BlockSpec((1, seq, width), lambda b, h: (b, 0, off + h))
    full = lambda shape: pl.BlockSpec(shape, lambda b, h: (0,) * len(shape))
    return pl.pallas_call(
        functools.partial(_attn_kernel, tq),
        grid=(bsz, DA_HEADS),
        in_specs=[blk(0), blk(DA_HEADS), blk(2 * DA_HEADS),
                  full((1, width)), full((1, width)), full((4, DA_HEAD_DIM)),
                  pl.BlockSpec((1, 1, width), lambda b, h: (h, 0, 0)),
                  full((1, width))],
        out_specs=pl.BlockSpec((1, seq, width), lambda b, h: (b, 0, h)),
        out_shape=jax.ShapeDtypeStruct((bsz, seq, DA_HEADS * width), F32),
        scratch_shapes=[pltpu.VMEM((seq, width), BF16)] * 2 + [pltpu.VMEM((seq, 2 * width), BF16),
                                                               pltpu.VMEM((2 * (seq // tq) - 1, tq, tq), F32)],
        compiler_params=_params("arbitrary", "arbitrary"),
        name="diff_attention",
    )(p_att3, p_att3, p_att3,
      jnp.tile(qn_g.reshape(1, DA_HEAD_DIM), (1, 2)), jnp.tile(kn_g.reshape(1, DA_HEAD_DIM), (1, 2)),
      lam, slopes, subln_g.reshape(1, width))


def _rwkv_chunk_kernel(n_blocks, cur_ref, prev_ref, next_ref, mu_ref, w0_ref, w2_ref, a0_ref, a2_ref, g2_ref,
                       kk_ref, ka_ref, rk_ref, hs_ref, ra_ref, y0_ref, g_ref, h_ref, bonus_ref, gate_ref):
    ch = RW_CHUNK
    w = RW_WIDTH
    pw = 2 * RW_HEAD
    blk = cur_ref.shape[1]
    n_sub = blk // ch
    block = pl.program_id(1)

    pc = cur_ref[0]
    row = lax.broadcasted_iota(jnp.int32, (blk, 1), 0)
    pv = prev_ref[0][SUBLANES - 1:SUBLANES, :] * (block > 0).astype(F32)
    nx = next_ref[0][0:1, :] * (block < n_blocks - 1).astype(F32)
    prev = jnp.where(row == 0, pv, pltpu.roll(pc, 1, 0))
    nxt = jnp.where(row == blk - 1, nx, pltpu.roll(pc, blk - 1, 0))
    mu = mu_ref[...]
    ps = pc + mu[0:1] * (prev - pc) + mu[1:2] * (nxt - pc)
    r, k, v = ps[:, 0:w], ps[:, w:2 * w], ps[:, 2 * w:3 * w]
    wd = ps[:, 3 * w:3 * w + LORA_COLS]
    ad = ps[:, 3 * w + LORA_COLS:3 * w + 2 * LORA_COLS]
    gd = ps[:, 3 * w + 2 * LORA_COLS:3 * w + 3 * LORA_COLS]

    head_sum = hs_ref[...]

    def per_head_sum(x):
        return _dot(x.astype(BF16), head_sum)

    kkr = k * kk_ref[...]
    kk = kkr / jnp.maximum(jnp.sqrt(per_head_sum(kkr * kkr)), 1e-12)
    bonus_ref[0] = per_head_sum(r * k * rk_ref[...]) * v
    gate_ref[0] = _dot(jax.nn.sigmoid(gd), g2_ref[...])

    ti = lax.broadcasted_iota(jnp.int32, (ch, ch), 0)
    tj = lax.broadcasted_iota(jnp.int32, (ch, ch), 1)
    eye = (ti == tj).astype(F32)
    lane = lax.broadcasted_iota(jnp.int32, (1, pw), 1)
    head_lanes = (lane < RW_HEAD, lane >= RW_HEAD)
    pi = lax.broadcasted_iota(jnp.int32, (pw, pw), 0)
    pj = lax.broadcasted_iota(jnp.int32, (pw, pw), 1)
    same_head = (pi < RW_HEAD) == (pj < RW_HEAD)
    eye_pair = (pi == pj).astype(F32)
    v16 = v.astype(BF16)
    tanh_wd = jnp.tanh(wd)

    probs = []
    pairs = []
    bi = lax.broadcasted_iota(jnp.int32, (blk, blk), 0)
    bj = lax.broadcasted_iota(jnp.int32, (blk, blk), 1)
    same_chunk = (bi >> (ch.bit_length() - 1)) == (bj >> (ch.bit_length() - 1))
    for d in range(2):
        z = w0_ref[d:d + 1] + _dot(tanh_wd, w2_ref[d])
        logdec = -jax.nn.sigmoid(z) * math.exp(-0.5)
        a = jax.nn.sigmoid(a0_ref[d:d + 1] + _dot(ad, a2_ref[d]))
        kd = k * (1.0 + (a - 1.0) * ka_ref[...])
        before = (tj < ti) if d == 0 else (tj > ti)
        upto = before | (tj == ti)
        upto_blk = same_chunk & ((bj <= bi) if d == 0 else (bj >= bi))
        cum = _dot(upto_blk.astype(F32), logdec, HIGHEST)
        w_in, w_ex, w_inv = jnp.exp(cum), jnp.exp(cum - logdec), jnp.exp(-cum)
        a_bar = -kk * w_ex
        b_bar = (kk * a * w_inv).astype(BF16)
        k_bar = (kd * w_inv).astype(BF16)
        r_bar = r * (w_in if d == 0 else w_ex)
        ymask = upto if d == 0 else before
        later, earlier = (ti, tj) if d == 0 else (tj, ti)
        off_masks = []
        for lvl in range(int(math.log2(ch))):
            same_pair = (ti >> (lvl + 1)) == (tj >> (lvl + 1))
            off_masks.append(same_pair & (((later >> lvl) & 1) == 1) & (((earlier >> lvl) & 1) == 0))
        for c in range(n_sub):
            rows = slice(c * ch, (c + 1) * ch)
            last = (c + 1) * ch - 1 if d == 0 else c * ch
            w_tot = w_in[last:last + 1]
            for p in range(RW_PAIRS):
                sl = slice(p * pw, (p + 1) * pw)
                pairs.append(dict(c=c, d=d, p=p, b=b_bar[rows, sl], k=k_bar[rows, sl], v=v16[rows, sl],
                                  r=r_bar[rows, sl], w_tot=w_tot[:, sl]))
                for hh in range(2):
                    mh = head_lanes[hh]
                    probs.append(dict(pair=len(pairs) - 1, before=before, ymask=ymask, off=off_masks,
                                      a=jnp.where(mh, a_bar[rows, sl], 0.0).astype(BF16),
                                      r=jnp.where(mh, r_bar[rows, sl], 0.0).astype(BF16),
                                      v=jnp.where(mh, v16[rows, sl], jnp.zeros_like(v16[rows, sl]))))

    for q in probs:
        pr = pairs[q["pair"]]
        q["ab"] = _dot_nt(q["a"], pr["b"])
        ak = _dot_nt(q["a"], pr["k"])
        q["rb"] = jnp.where(q["ymask"], _dot_nt(q["r"], pr["b"]), 0.0).astype(BF16)
        rk = _dot_nt(q["r"], pr["k"])
        q["ak"] = jnp.where(q["before"], ak, 0.0).astype(BF16)
        q["rk"] = jnp.where(q["ymask"], rk, 0.0).astype(BF16)
    for q in probs:
        q["akv"] = _dot(q["ak"], q["v"])
        q["rkv"] = _dot(q["rk"], q["v"])
        q["inv"] = eye + jnp.where(q["off"][0], q["ab"], 0.0)
    for lvl in range(1, int(math.log2(ch))):
        for q in probs:
            q["tmp"] = _dot(q["inv"].astype(BF16), jnp.where(q["off"][lvl], q["ab"], 0.0).astype(BF16))
        for q in probs:
            q["inv"] = q["inv"] + _dot(q["tmp"].astype(BF16), q["inv"].astype(BF16))
    for q in probs:
        rhs = jnp.concatenate([q["a"], q["akv"].astype(BF16)], axis=1)
        q["x"] = _dot(q["inv"].astype(BF16), rhs)
    for q in probs:
        q["z"] = _dot(q["rb"], q["x"].astype(BF16))
    for i, pr in enumerate(pairs):
        q0, q1 = probs[2 * i], probs[2 * i + 1]
        x = q0["x"] + q1["x"]
        z = q0["z"] + q1["z"]
        ra = pr["r"] + z[:, :pw]
        y0 = z[:, pw:] + q0["rkv"] + q1["rkv"]
        xb = _dot_tn(x.astype(BF16), pr["b"])
        vk = _dot_tn(pr["v"], pr["k"])
        g = (eye_pair + jnp.where(same_head, xb[:pw], 0.0)) * pr["w_tot"]
        h = jnp.where(same_head, xb[pw:] + vk, 0.0) * pr["w_tot"]
        c, d, p = pr["c"], pr["d"], pr["p"]
        sl = slice(p * pw, (p + 1) * pw)
        rows = slice(c * ch, (c + 1) * ch)
        ra_ref[0, d, rows, sl] = ra.astype(BF16)
        y0_ref[0, d, rows, sl] = y0.astype(BF16)
        g_ref[0, d, c, :, sl] = g.astype(BF16)
        h_ref[0, d, c, :, sl] = h.astype(BF16)


def _pad_lora(w2):
    keep = jnp.arange(2)[:, None, None, None] == jnp.arange(2)[None, :, None, None]
    return jnp.where(keep, w2[None], 0.0).reshape(2, -1, w2.shape[-1])


def _rwkv_chunks(p_rw3, mu, w0, w2, a0, a2, g2, k_k, k_a, r_k, chunks_per_step=2):
    bsz, seq, cols = p_rw3.shape
    ch, w, pw = RW_CHUNK, RW_WIDTH, 2 * RW_HEAD
    blk = chunks_per_step * ch
    n_chunks = seq // ch
    n_blocks = seq // blk
    sub_per_blk = blk // SUBLANES
    n_sub = seq // SUBLANES
    full = lambda shape: pl.BlockSpec(shape, lambda b, c: (0,) * len(shape))
    row = lambda x: x.reshape(1, w)
    head_sum = (jnp.arange(w)[:, None] // RW_HEAD == jnp.arange(w)[None, :] // RW_HEAD).astype(BF16)
    tok = lambda dt: jax.ShapeDtypeStruct((bsz, 2, seq, w), dt)
    mat = lambda dt: jax.ShapeDtypeStruct((bsz, 2, n_chunks, pw, w), dt)
    tok_spec = pl.BlockSpec((1, 2, blk, w), lambda b, c: (b, 0, c, 0))
    mat_spec = pl.BlockSpec((1, 2, chunks_per_step, pw, w), lambda b, c: (b, 0, c, 0, 0))
    one_spec = pl.BlockSpec((1, blk, w), lambda b, c: (b, c, 0))
    return pl.pallas_call(
        functools.partial(_rwkv_chunk_kernel, n_blocks),
        grid=(bsz, n_blocks),
        in_specs=[pl.BlockSpec((1, blk, cols), lambda b, c: (b, c, 0)),
                  pl.BlockSpec((1, SUBLANES, cols), lambda b, c: (b, jnp.maximum(c * sub_per_blk - 1, 0), 0)),
                  pl.BlockSpec((1, SUBLANES, cols),
                               lambda b, c: (b, jnp.minimum((c + 1) * sub_per_blk, n_sub - 1), 0)),
                  full((2, cols)), full((2, w)), full((2, LORA_COLS, w)), full((2, w)), full((2, LORA_COLS, w)),
                  full((LORA_COLS, w)), full((1, w)), full((1, w)), full((1, w)), full((w, w))],
        out_specs=[tok_spec, tok_spec, mat_spec, mat_spec, one_spec, one_spec],
        out_shape=[tok(BF16), tok(BF16), mat(BF16), mat(BF16),
                   jax.ShapeDtypeStruct((bsz, seq, w), F32), jax.ShapeDtypeStruct((bsz, seq, w), F32)],
        compiler_params=_params("parallel", "parallel"),
        name="rwkv7_chunks",
    )(p_rw3, p_rw3, p_rw3, mu, w0, _pad_lora(w2), a0, _pad_lora(a2), g2, row(k_k), row(k_a), row(r_k), head_sum)


def _rwkv_scan_kernel(ra0_ref, ra1_ref, y00_ref, y01_ref, g0_ref, g1_ref, h0_ref, h1_ref, o0_ref, o1_ref, st_ref):
    pw = 2 * RW_HEAD

    @pl.when(pl.program_id(1) == 0)
    def _():
        st_ref[...] = jnp.zeros_like(st_ref)

    ch = RW_CHUNK
    n_sub = g0_ref.shape[2]
    dirs = ((ra0_ref, y00_ref, g0_ref, h0_ref, o0_ref), (ra1_ref, y01_ref, g1_ref, h1_ref, o1_ref))
    states = {(d, p): st_ref[d, p] for d in range(2) for p in range(RW_PAIRS)}
    for step in range(n_sub):
        for d, (ra_ref, y0_ref, g_ref, h_ref, o_ref) in enumerate(dirs):
            c = step if d == 0 else n_sub - 1 - step
            rows = slice(c * ch, (c + 1) * ch)
            for p in range(RW_PAIRS):
                sl = slice(p * pw, (p + 1) * pw)
                s16 = states[d, p].astype(BF16)
                o_ref[0, rows, sl] = _dot_nt(ra_ref[0, 0, rows, sl], s16) + y0_ref[0, 0, rows, sl]
                states[d, p] = _dot(s16, g_ref[0, 0, c, :, sl]) + h_ref[0, 0, c, :, sl]
    for (d, p), s in states.items():
        st_ref[d, p] = s


def _rwkv_scan(ra, y0, g, h, n_sub=4):
    bsz, _, seq, w = ra.shape
    ch, pw = RW_CHUNK, 2 * RW_HEAD
    n_blk = seq // (ch * n_sub)
    blk = lambda d, c: c if d == 0 else n_blk - 1 - c
    tok = lambda d: pl.BlockSpec((1, 1, n_sub * ch, w), lambda b, c: (b, d, blk(d, c), 0))
    mat = lambda d: pl.BlockSpec((1, 1, n_sub, pw, w), lambda b, c: (b, d, blk(d, c), 0, 0))
    out = lambda d: pl.BlockSpec((1, n_sub * ch, w), lambda b, c: (b, blk(d, c), 0))
    return pl.pallas_call(
        _rwkv_scan_kernel,
        grid=(bsz, n_blk),
        in_specs=[tok(0), tok(1), tok(0), tok(1), mat(0), mat(1), mat(0), mat(1)],
        out_specs=[out(0), out(1)],
        out_shape=[jax.ShapeDtypeStruct((bsz, seq, w), F32)] * 2,
        scratch_shapes=[pltpu.VMEM((2, RW_PAIRS, pw, pw), F32)],
        compiler_params=_params("parallel", "arbitrary"),
        name="rwkv7_scan",
    )(ra, ra, y0, y0, g, g, h, h)


def _outproj_kernel(att_ref, yf_ref, yb_ref, bonus_ref, gate_ref, lng_ref, lnb_ref, hs_ref, x_ref, mod_ref,
                    wa_ref, wr_ref, g_ref, x1_ref, h2_ref, h2t_ref):
    head_sum = hs_ref[...]

    def per_head_mean(v):
        return _dot(v.astype(BF16), head_sum) * (1.0 / RW_HEAD)

    y = yf_ref[...] + yb_ref[...]
    yc = y - per_head_mean(y)
    yn = yc * lax.rsqrt(per_head_mean(yc * yc) + GN_EPS) * lng_ref[...] + lnb_ref[...]
    rw = (yn + bonus_ref[...]) * gate_ref[...]
    acc = _dot(att_ref[...].astype(BF16), wa_ref[...]) + _dot(rw.astype(BF16), wr_ref[...])
    m = mod_ref[0]
    x1 = x_ref[...] + m[2:3] * acc
    x1_ref[...] = x1
    ms = jnp.mean(x1 * x1, axis=-1, keepdims=True)
    yo = x1 * lax.rsqrt(ms + NORM_EPS) * g_ref[...]
    h2 = yo * (1.0 + m[4:5]) + m[3:4]
    h2_ref[...] = h2.astype(BF16)
    h2t_ref[...] = h2.T.astype(BF16)


def _out_proj(att2, y_f, y_b, bonus, gate, ln_g, ln_b, x2, mod3, w_a, w_r, g, seq, tm=512):
    t, d = x2.shape
    ka, kr = att2.shape[1], y_f.shape[1]
    per_b = seq // tm
    rows = lambda n: pl.BlockSpec((tm, n), lambda i: (i, 0))
    full = lambda shape: pl.BlockSpec(shape, lambda i: (0,) * len(shape))
    head_sum = (jnp.arange(kr)[:, None] // RW_HEAD == jnp.arange(kr)[None, :] // RW_HEAD).astype(BF16)
    return pl.pallas_call(
        _outproj_kernel,
        grid=(t // tm,),
        in_specs=[rows(ka), rows(kr), rows(kr), rows(kr), rows(kr), full((1, kr)), full((1, kr)), full((kr, kr)),
                  rows(d), pl.BlockSpec((1, 6, d), lambda i: (i // per_b, 0, 0)),
                  full((ka, d)), full((kr, d)), full((1, d))],
        out_specs=[rows(d), rows(d), pl.BlockSpec((d, tm), lambda i: (0, i))],
        out_shape=[jax.ShapeDtypeStruct((t, d), F32), jax.ShapeDtypeStruct((t, d), BF16),
                   jax.ShapeDtypeStruct((d, t), BF16)],
        compiler_params=_params("arbitrary"),
        name="out_proj",
    )(att2, y_f, y_b, bonus, gate, ln_g.reshape(1, kr), ln_b.reshape(1, kr), head_sum, x2, mod3, w_a, w_r,
      g.reshape(1, d))


def _topk_ranks(scores, vals_refs, rank_refs):
    nk, tm = scores[0].shape
    kiota = lax.broadcasted_iota(jnp.int32, (nk, tm), 0)

    def body(j, carry):
        out = []
        for (cur, rank), vals_ref in zip(carry, vals_refs):
            m = jnp.max(cur, axis=0, keepdims=True)
            hit = kiota == jnp.min(jnp.where(cur == m, kiota, nk), axis=0, keepdims=True)
            vals_ref[pl.ds(j, 1), :] = m
            out.append((jnp.where(hit, NEG_INF, cur), jnp.where(hit, j, rank)))
        return tuple(out)

    init = tuple((s, jnp.full((nk, tm), PEER_TOPK, jnp.int32)) for s in scores)
    for (_, rank), rank_ref in zip(lax.fori_loop(0, PEER_TOPK, body, init), rank_refs):
        rank_ref[...] = rank


def _compare_exchange(rows, i, j):
    a, b = rows[i], rows[j]
    rows[i], rows[j] = jnp.maximum(a, b), jnp.minimum(a, b)


def _bitonic_merge_desc(rows):
    stride = len(rows) // 2
    while stride >= 1:
        for i in range(len(rows)):
            if not i & stride:
                _compare_exchange(rows, i, i + stride)
        stride //= 2


def _bitonic_sort_desc(rows):
    n, size = len(rows), 2
    while size <= n:
        stride = size // 2
        while stride >= 1:
            for i in range(n):
                if not i & stride:
                    lo, hi = (i, i + stride) if (size == n or not i & size) else (i + stride, i)
                    _compare_exchange(rows, lo, hi)
            stride //= 2
        size *= 2


def _top16_sorted(x):
    n = PEER_TOPK
    rows = [x[SUBLANES * r:SUBLANES * (r + 1)] for r in range(x.shape[0] // SUBLANES)]
    rows += [jnp.full_like(rows[0], NEG_INF)] * (n - len(rows))
    _bitonic_sort_desc(rows)
    shift = SUBLANES // 2
    while shift >= 1:
        other = [pltpu.roll(rows[n - 1 - r], shift, 0) for r in range(n)]
        rows = [jnp.maximum(rows[r], other[r]) for r in range(n)]
        _bitonic_merge_desc(rows)
        shift //= 2
    return rows


def _top16_values(s, vals_ref):
    n = PEER_TOPK
    rows = _top16_sorted(s)
    for r in range(n):
        vals_ref[r:r + 1, :] = rows[r][0:1]
    distinct = rows[0] > rows[1]
    for r in range(1, n - 1):
        distinct = distinct & (rows[r] > rows[r + 1])
    at_least = jnp.zeros_like(rows[0])
    for r in range(n):
        at_least = at_least + jnp.where(s[SUBLANES * r:SUBLANES * (r + 1)] >= rows[n - 1], 1.0, 0.0)
    count = jnp.sum(at_least, axis=0, keepdims=True)
    return jnp.where(distinct[0:1] & (count == n), 0.0, 1.0)


def _pair_candidates():
    pieces = []
    for lvl in range(4):
        cnt = PEER_TOPK // (lvl + 1)
        pieces.append(("row", lvl, lvl, cnt))
        if lvl + 1 < cnt:
            pieces.append(("col", lvl, lvl + 1, cnt))
    return pieces


def _select_pairs(cand, ci, valid, sel_ref):
    big = 4 * PEER_TOPK * PEER_TOPK

    def body(j, cur):
        m = jnp.max(cur, axis=0, keepdims=True)
        hit = ci == jnp.min(jnp.where(cur == m, ci, big), axis=0, keepdims=True)
        return jnp.where(hit, NEG_INF, cur)

    cur = lax.fori_loop(0, PEER_TOPK, body, cand)
    sel_ref[...] = jnp.where((cur == NEG_INF) & valid, 1.0, 0.0)


def _select_pairs_by_value(cand, sel_ref):
    worst = 0.0
    for g in range(cand.shape[1] // LANES):
        lanes = slice(g * LANES, (g + 1) * LANES)
        c = cand[:, lanes]
        kth = _top16_sorted(c)[PEER_TOPK - 1][0:1]
        sel = jnp.where(c >= kth, 1.0, 0.0)
        sel_ref[:, lanes] = sel
        worst = jnp.maximum(worst, jnp.max(jnp.abs(jnp.sum(sel, axis=0, keepdims=True) - PEER_TOPK)))
    return worst


def _route_kernel(h_ref, wq_ref, keys_ref, p1_ref, cnt_ref, p2_ref, rk2_ref, q_scr, v1_ref, v2_ref, rank_ref,
                  sel_ref, key_ref, kref_ref):
    tm = h_ref.shape[0]
    nk = PEER_NKEYS
    n_grp = tm // LANES
    q = _dot(h_ref[...], wq_ref[...]).astype(BF16)
    for hp in range(2 * PEER_HEADS):
        q_scr[hp] = q[:, hp * nk:(hp + 1) * nk]

    pieces = _pair_candidates()
    big = 4 * PEER_TOPK * PEER_TOPK
    cat = lambda parts: jnp.concatenate(parts, axis=1)

    def head(h, carry):
        s1 = _dot_nt(keys_ref[2 * h], q_scr[2 * h])
        s2 = _dot_nt(keys_ref[2 * h + 1], q_scr[2 * h + 1])
        probs = [s[:, g * LANES:(g + 1) * LANES] for s in (s1, s2) for g in range(n_grp)]
        vals = [ref.at[g] for ref in (v1_ref, v2_ref) for g in range(n_grp)]
        ranks = [rank_ref.at[i] for i in range(2 * n_grp)]
        ambiguous = jnp.max(cat([_top16_values(prob, val) for prob, val in zip(probs, vals)]))
        key_ref[0] = s1
        key_ref[1] = s2
        kref_ref[0] = cat([v1_ref[g] for g in range(n_grp)])
        kref_ref[1] = cat([v2_ref[g] for g in range(n_grp)])

        @pl.when(ambiguous > 0.0)
        def _():
            _topk_ranks(probs, vals, ranks)
            rank_row = jnp.broadcast_to(lax.broadcasted_iota(jnp.int32, (PEER_TOPK, 1), 0).astype(F32),
                                        (PEER_TOPK, tm))
            for half in range(2):
                key_ref[half] = cat([rank_ref[half * n_grp + g] for g in range(n_grp)]).astype(F32)
                kref_ref[half] = rank_row

        v1 = cat([v1_ref[g] for g in range(n_grp)])
        v2 = cat([v2_ref[g] for g in range(n_grp)])

        cands, cis, valids = [], [], []
        for kind, fixed, lo, hi in pieces:
            rows = SUBLANES * ((hi + SUBLANES - 1) // SUBLANES)
            ridx = lax.broadcasted_iota(jnp.int32, (rows, 1), 0)
            valid = (ridx >= lo) & (ridx < hi)
            if kind == "row":
                vsum = v1[fixed:fixed + 1] + v2[0:rows]
                ci = fixed * PEER_TOPK + ridx
            else:
                vsum = v1[0:rows] + v2[fixed:fixed + 1]
                ci = ridx * PEER_TOPK + fixed
            cands.append(jnp.where(valid, vsum, NEG_INF))
            cis.append(jnp.broadcast_to(jnp.where(valid, ci, big), (rows, tm)))
            valids.append(jnp.broadcast_to(valid, (rows, tm)))
        cand = jnp.concatenate(cands, axis=0)
        ci = jnp.concatenate(cis, axis=0)
        valid = jnp.concatenate(valids, axis=0)
        off = _select_pairs_by_value(cand, sel_ref)

        @pl.when(off > 0.0)
        def _():
            _select_pairs(cand, ci, valid, sel_ref)

        sel = sel_ref[...]
        zsum = jnp.sum(sel * jnp.exp(cand - (v1[0:1] + v2[0:1])), axis=0, keepdims=True)
        aidx = lax.broadcasted_iota(jnp.int32, (PEER_TOPK, 1), 0)
        per_a = jnp.zeros((PEER_TOPK, tm), F32)
        row0 = 0
        for kind, fixed, lo, hi in pieces:
            rows = SUBLANES * ((hi + SUBLANES - 1) // SUBLANES)
            part = sel[row0:row0 + rows]
            row0 += rows
            if kind == "row":
                per_a = per_a + jnp.where(aidx == fixed, jnp.sum(part, axis=0, keepdims=True), 0.0)
            elif rows == PEER_TOPK:
                per_a = per_a + part
            else:
                per_a = per_a + jnp.concatenate([part, jnp.zeros((PEER_TOPK - rows, tm), F32)], axis=0)
        key1, key2 = key_ref[0], key_ref[1]
        kref1, kref2 = kref_ref[0], kref_ref[1]
        cnt = jnp.zeros((nk, tm), F32)
        rank2 = jnp.full((nk, tm), float(PEER_TOPK), F32)
        for a in range(PEER_TOPK):
            cnt = jnp.where(key1 == kref1[a:a + 1], per_a[a:a + 1], cnt)
            rank2 = jnp.where(key2 == kref2[a:a + 1], float(a), rank2)
        p1_ref[h] = jnp.exp(s1 - v1[0:1]) / (zsum * 2.0 ** 0.5)
        cnt_ref[h] = cnt
        p2_ref[h] = jnp.exp(s2 - v2[0:1]).astype(BF16)
        rk2_ref[h] = rank2.astype(BF16)
        return carry

    lax.fori_loop(0, PEER_HEADS, head, 0)


def _route(h2, wq, keys, tm=512):
    t, d = h2.shape
    nq = wq.shape[1]
    nk = PEER_NKEYS
    out = lambda dt: jax.ShapeDtypeStruct((PEER_HEADS, nk, t), dt)
    ospec = pl.BlockSpec((PEER_HEADS, nk, tm), lambda i: (0, 0, i))
    return pl.pallas_call(
        _route_kernel,
        grid=(t // tm,),
        in_specs=[pl.BlockSpec((tm, d), lambda i: (i, 0)),
                  pl.BlockSpec((d, nq), lambda i: (0, 0)),
                  pl.BlockSpec((2 * PEER_HEADS, nk, nk), lambda i: (0, 0, 0))],
        out_specs=[ospec] * 4,
        out_shape=[out(F32), out(F32), out(BF16), out(BF16)],
        scratch_shapes=[pltpu.VMEM((2 * PEER_HEADS, tm, nk), BF16),
                        pltpu.VMEM((tm // LANES, PEER_TOPK, LANES), F32),
                        pltpu.VMEM((tm // LANES, PEER_TOPK, LANES), F32),
                        pltpu.VMEM((2 * tm // LANES, nk, LANES), jnp.int32),
                        pltpu.VMEM((sum(SUBLANES * ((hi + SUBLANES - 1) // SUBLANES)
                                        for _, _, _, hi in _pair_candidates()), tm), F32),
                        pltpu.VMEM((2, nk, tm), F32), pltpu.VMEM((2, PEER_TOPK, tm), F32)],
        compiler_params=_params("parallel"),
        name="peer_route",
    )(h2, wq, keys)


def _expert_kernel(n_eblk, h_ref, u_ref, vt_ref, p1_ref, cnt_ref, p2_ref, rk2_ref, x1_ref, mod_ref,
                   o_ref, acc_ref, act_ref, gate_ref):
    j = pl.program_id(1)
    te = u_ref.shape[0]
    tm = h_ref.shape[1]
    nk = PEER_NKEYS
    kb_rows = MXU_WIDTH
    per_kb = kb_rows // nk
    tile = 2 * SUBLANES

    @pl.when(j == 0)
    def _():
        acc_ref[...] = jnp.zeros_like(acc_ref)

    hb = h_ref[...]
    n_kb = te // kb_rows
    second_k = 2 * MXU_WIDTH

    first_rows = MXU_WIDTH

    def gates(kb):
        for sb in range(kb * per_kb, (kb + 1) * per_kb):
            e1 = j * (te // nk) + sb
            gate = None
            for h in range(PEER_HEADS):
                c = jnp.broadcast_to(cnt_ref[h, pl.ds(e1, 1), :], (tile, tm)).astype(BF16)
                p = jnp.broadcast_to(p1_ref[h, pl.ds(e1, 1), :], (tile, tm)).astype(BF16)
                rk2 = rk2_ref[h].reshape(nk // tile, tile, tm)
                p2 = p2_ref[h].reshape(nk // tile, tile, tm)
                term = jnp.where(rk2 < c[None], p2, jnp.zeros_like(p2)) * p[None]
                gate = term if gate is None else gate + term
            gate_ref[sb * nk:(sb + 1) * nk, :] = gate.reshape(nk, tm)

    for kb in range(n_kb):
        gates(kb)
    for r0 in range(0, te, first_rows):
        act_ref[r0:r0 + first_rows, :] = _dot(u_ref[r0:r0 + first_rows, :], hb)
    acc = acc_ref[...]
    for c0 in range(0, te, second_k):
        rows = slice(c0, c0 + second_k)
        t = act_ref[rows, :]
        w = gate_ref[rows, :] * (t + t * lax.erf(t)).astype(BF16)
        acc = acc + _dot(vt_ref[:, rows], w)
    acc_ref[...] = acc

    @pl.when(j == n_eblk - 1)
    def _():
        o_ref[...] = x1_ref[...] + mod_ref[0][5:6] * acc_ref[...].T


def _experts(h2t, u, vt, p1, cnt, p2, rk2, x1, mod3, seq, tm=512, te=2048):
    d, t = h2t.shape
    ne = u.shape[0]
    nk = PEER_NKEYS
    n_eblk = ne // te
    per_b = seq // tm
    rspec = pl.BlockSpec((PEER_HEADS, nk, tm), lambda i, j: (0, 0, i))
    return pl.pallas_call(
        functools.partial(_expert_kernel, n_eblk),
        grid=(t // tm, n_eblk),
        in_specs=[pl.BlockSpec((d, tm), lambda i, j: (0, i)),
                  pl.BlockSpec((te, d), lambda i, j: (j, 0)),
                  pl.BlockSpec((d, te), lambda i, j: (0, j)),
                  rspec, rspec, rspec, rspec,
                  pl.BlockSpec((tm, d), lambda i, j: (i, 0)),
                  pl.BlockSpec((1, 6, d), lambda i, j: (i // per_b, 0, 0))],
        out_specs=pl.BlockSpec((tm, d), lambda i, j: (i, 0)),
        out_shape=jax.ShapeDtypeStruct((t, d), F32),
        scratch_shapes=[pltpu.VMEM((d, tm), F32), pltpu.VMEM((te, tm), F32), pltpu.VMEM((te, tm), BF16)],
        compiler_params=_params("parallel", "arbitrary"),
        name="peer_experts",
    )(h2t, u, vt, p1, cnt, p2, rk2, x1, mod3)


def kernel(x, c, ada_w, ada_b, norm1_g, w_in, da_qnorm_g, da_knorm_g, da_lambda, da_subln_g, rw_shift_mu, rw_w0,
           rw_w2, rw_a0, rw_a2, rw_g2, rw_k_k, rw_k_a, rw_r_k, rw_ln_g, rw_ln_b, w_out, norm2_g, peer_wq,
           peer_keys, peer_u, peer_v):
    bsz, seq, d = x.shape
    depth = ada_w.shape[0]
    t = bsz * seq
    for l in range(depth):
        mod3 = _ada_mod(c, ada_w[l], ada_b[l]).reshape(bsz, 6, d)
        x2 = x.reshape(t, d)
        w_l = w_in[l].astype(BF16)
        p_att, p_rw = _in_proj(x2, mod3, norm1_g[l], w_l[:, :ATT_COLS], w_l[:, ATT_COLS:], seq)
        att = _diff_attention(p_att.reshape(bsz, seq, ATT_COLS), da_qnorm_g[l], da_knorm_g[l], da_lambda[l],
                              da_subln_g[l])
        ra, y0, g, h, bonus, gate = _rwkv_chunks(p_rw.reshape(bsz, seq, RWKV_COLS), rw_shift_mu[l], rw_w0[l],
                                                 rw_w2[l], rw_a0[l], rw_a2[l], rw_g2[l], rw_k_k[l], rw_k_a[l],
                                                 rw_r_k[l].reshape(-1))
        y_f, y_b = _rwkv_scan(ra, y0, g, h)
        wo = w_out[l].astype(BF16)
        aw = att.shape[-1]
        flat = lambda a: a.reshape(t, RW_WIDTH)
        x1, h2, h2t = _out_proj(att.reshape(t, aw), flat(y_f), flat(y_b), flat(bonus), flat(gate), rw_ln_g[l],
                                rw_ln_b[l], x2, mod3, wo[:aw], wo[aw:], norm2_g[l], seq)
        keys = peer_keys[l].reshape(2 * PEER_HEADS, PEER_NKEYS, -1).astype(BF16)
        p1, cnt, p2, rk2 = _route(h2, peer_wq[l].astype(BF16), keys)
        u_scaled = (peer_u[l] * 2.0 ** -0.5).astype(BF16)
        x = _experts(h2t, u_scaled, peer_v[l].T.astype(BF16), p1, cnt, p2, rk2, x1, mod3, seq).reshape(bsz, seq, d)
    return x
```

```python
import functools
import math

import jax
import jax.numpy as jnp
from jax import lax
from jax.experimental import pallas as pl
from jax.experimental.pallas import tpu as pltpu

F32 = jnp.float32
BF16 = jnp.bfloat16
HIGHEST = lax.Precision.HIGHEST

LANES = 128
SUBLANES = 8
MXU_WIDTH = 256
VMEM_LIMIT_BYTES = 56 * 1024 * 1024

DA_HEAD_DIM = 64
DA_HEADS = 4
RW_HEAD = 64
RW_HEADS = 8
RW_PAIRS = RW_HEADS // 2
RW_WIDTH = RW_HEADS * RW_HEAD
LORA_COLS = 128
RWKV_COLS = 3 * RW_WIDTH + 3 * LORA_COLS
ATT_COLS = 3 * 2 * DA_HEAD_DIM * DA_HEADS
GN_EPS = 64e-5
NORM_EPS = 1e-6
LAMBDA_INIT = 0.8 - 0.6 * math.exp(-0.3 * 0)
PEER_HEADS = 8
PEER_NKEYS = 128
PEER_TOPK = 16
RW_CHUNK = 64
NEG_INF = float("-inf")


def _params(*sem):
    return pltpu.CompilerParams(dimension_semantics=sem, vmem_limit_bytes=VMEM_LIMIT_BYTES)


def _dot(a, b, precision=None):
    return jnp.dot(a, b, preferred_element_type=F32, precision=precision)


def _dot_nt(a, b, precision=None):
    return lax.dot_general(a, b, (((1,), (1,)), ((), ())), preferred_element_type=F32, precision=precision)


def _dot_tn(a, b, precision=None):
    return lax.dot_general(a, b, (((0,), (0,)), ((), ())), preferred_element_type=F32, precision=precision)


def _group_matrix(n, group, value):
    shift = group.bit_length() - 1
    r = lax.broadcasted_iota(jnp.int32, (n, n), 0) >> shift
    c = lax.broadcasted_iota(jnp.int32, (n, n), 1) >> shift
    return jnp.where(r == c, value, 0.0).astype(F32)


def _ada_kernel(c_ref, w_ref, b_ref, o_ref):
    c = c_ref[...]
    s = c * jax.nn.sigmoid(c)
    o_ref[...] = _dot(s, w_ref[...], HIGHEST) + b_ref[...]


def _ada_mod(c, w, b):
    bsz, d = c.shape
    n = w.shape[1]
    tn = 1024
    return pl.pallas_call(
        _ada_kernel,
        grid=(n // tn,),
        in_specs=[pl.BlockSpec((bsz, d), lambda j: (0, 0)),
                  pl.BlockSpec((d, tn), lambda j: (0, j)),
                  pl.BlockSpec((1, tn), lambda j: (0, j))],
        out_specs=pl.BlockSpec((bsz, tn), lambda j: (0, j)),
        out_shape=jax.ShapeDtypeStruct((bsz, n), F32),
        compiler_params=_params("arbitrary"),
        name="ada_mod",
    )(c, w, b.reshape(1, n))


def _inproj_kernel(x_ref, mod_ref, g_ref, wa_ref, wr_ref, oa_ref, or_ref):
    x = x_ref[...]
    ms = jnp.mean(x * x, axis=-1, keepdims=True)
    y = x * lax.rsqrt(ms + NORM_EPS) * g_ref[...]
    m = mod_ref[0]
    h = (y * (1.0 + m[1:2]) + m[0:1]).astype(BF16)
    oa_ref[...] = _dot(h, wa_ref[...])
    or_ref[...] = _dot(h, wr_ref[...])


def _in_proj(x2, mod3, g, w_att, w_rw, seq, tm=512):
    t, d = x2.shape
    na, nr = w_att.shape[1], w_rw.shape[1]
    per_b = seq // tm
    return pl.pallas_call(
        _inproj_kernel,
        grid=(t // tm,),
        in_specs=[pl.BlockSpec((tm, d), lambda i: (i, 0)),
                  pl.BlockSpec((1, 6, d), lambda i: (i // per_b, 0, 0)),
                  pl.BlockSpec((1, d), lambda i: (0, 0)),
                  pl.BlockSpec((d, na), lambda i: (0, 0)),
                  pl.BlockSpec((d, nr), lambda i: (0, 0))],
        out_specs=[pl.BlockSpec((tm, na), lambda i: (i, 0)),
                   pl.BlockSpec((tm, nr), lambda i: (i, 0))],
        out_shape=[jax.ShapeDtypeStruct((t, na), F32), jax.ShapeDtypeStruct((t, nr), F32)],
        compiler_params=_params("arbitrary"),
        name="in_proj",
    )(x2, mod3, g.reshape(1, d), w_att, w_rw)


def _attn_kernel(tq, q_ref, k_ref, v_ref, qg_ref, kg_ref, lam_ref, slope_ref, sg_ref, o_ref, qs, ks, vs, bias_ref):
    seq = q_ref.shape[1]
    width = 2 * DA_HEAD_DIM
    avg = _group_matrix(width, DA_HEAD_DIM, 1.0 / DA_HEAD_DIM)

    def qk_norm(x, g):
        ms = _dot(x * x, avg)
        return x * lax.rsqrt(ms + NORM_EPS) * g

    qs[...] = (qk_norm(q_ref[0], qg_ref[...]) * (DA_HEAD_DIM ** -0.5)).astype(BF16)
    ks[...] = qk_norm(k_ref[0], kg_ref[...]).astype(BF16)
    ones_col = (lax.broadcasted_iota(jnp.int32, (seq, width), 1) == 0).astype(BF16)
    vs[...] = jnp.concatenate([v_ref[0].astype(BF16), ones_col], axis=1)

    lam = lam_ref[...]
    lam_full = (jnp.exp(jnp.sum(lam[0:1] * lam[1:2], axis=-1, keepdims=True))
                - jnp.exp(jnp.sum(lam[2:3] * lam[3:4], axis=-1, keepdims=True)) + LAMBDA_INIT)
    slope = slope_ref[0][:, 0:1]
    first = lax.broadcasted_iota(jnp.int32, (1, width), 1) < DA_HEAD_DIM
    sg = sg_ref[...] * (1.0 - LAMBDA_INIT)

    n_blk = seq // tq
    dist = lax.broadcasted_iota(jnp.int32, (tq, tq), 0) - lax.broadcasted_iota(jnp.int32, (tq, tq), 1)
    for d in range(2 * n_blk - 1):
        bias_ref[d] = slope * jnp.abs(dist - (d - (n_blk - 1)) * tq).astype(F32)

    def body(i, carry):
        r0 = pl.multiple_of(i * tq, tq)
        qt = qs[pl.ds(r0, tq), :]
        q0 = jnp.where(first, qt, jnp.zeros_like(qt))
        q1 = jnp.where(first, jnp.zeros_like(qt), qt)
        kk = ks[...]
        bias = jnp.concatenate([bias_ref[j - i + (n_blk - 1)] for j in range(n_blk)], axis=1)
        s0 = _dot_nt(q0, kk) - bias
        s1 = _dot_nt(q1, kk) - bias
        p0 = jnp.exp(s0 - jnp.max(s0, axis=-1, keepdims=True))
        p1 = jnp.exp(s1 - jnp.max(s1, axis=-1, keepdims=True))
        pv0 = _dot(p0.astype(BF16), vs[...])
        pv1 = _dot(p1.astype(BF16), vs[...])
        w0 = 1.0 / pv0[:, width:width + 1]
        w1 = lam_full / pv1[:, width:width + 1]
        o = pv0[:, :width] * w0 - pv1[:, :width] * w1
        ms = jnp.mean(o * o, axis=-1, keepdims=True)
        o_ref[0, pl.ds(r0, tq), :] = o * lax.rsqrt(ms + NORM_EPS) * sg
        return carry

    lax.fori_loop(0, seq // tq, body, 0)


def _diff_attention(p_att3, qn_g, kn_g, lam, subln_g, tq=256):
    bsz, seq, _ = p_att3.shape
    width = 2 * DA_HEAD_DIM
    slopes = jnp.asarray([2.0 ** (-8.0 * (h + 1) / DA_HEADS) for h in range(DA_HEADS)], F32)
    slopes = jnp.broadcast_to(slopes[:, None, None], (DA_HEADS, 1, width))
    blk = lambda off: pl.BlockSpec((1, seq, width), lambda b, h: (b, 0, off + h))
    full = lambda shape: pl.BlockSpec(shape, lambda b, h: (0,) * len(shape))
    return pl.pallas_call(
        functools.partial(_attn_kernel, tq),
        grid=(bsz, DA_HEADS),
        in_specs=[blk(0), blk(DA_HEADS), blk(2 * DA_HEADS),
                  full((1, width)), full((1, width)), full((4, DA_HEAD_DIM)),
                  pl.BlockSpec((1, 1, width), lambda b, h: (h, 0, 0)),
                  full((1, width))],
        out_specs=pl.BlockSpec((1, seq, width), lambda b, h: (b, 0, h)),
        out_shape=jax.ShapeDtypeStruct((bsz, seq, DA_HEADS * width), F32),
        scratch_shapes=[pltpu.VMEM((seq, width), BF16)] * 2 + [pltpu.VMEM((seq, 2 * width), BF16),
                                                               pltpu.VMEM((2 * (seq // tq) - 1, tq, tq), F32)],
        compiler_params=_params("arbitrary", "arbitrary"),
        name="diff_attention",
    )(p_att3, p_att3, p_att3,
      jnp.tile(qn_g.reshape(1, DA_HEAD_DIM), (1, 2)), jnp.tile(kn_g.reshape(1, DA_HEAD_DIM), (1, 2)),
      lam, slopes, subln_g.reshape(1, width))


def _rwkv_chunk_kernel(n_blocks, cur_ref, prev_ref, next_ref, mu_ref, w0_ref, w2_ref, a0_ref, a2_ref, g2_ref,
                       kk_ref, ka_ref, rk_ref, hs_ref, ra_ref, y0_ref, g_ref, h_ref, bonus_ref, gate_ref):
    ch = RW_CHUNK
    w = RW_WIDTH
    pw = 2 * RW_HEAD
    blk = cur_ref.shape[1]
    n_sub = blk // ch
    block = pl.program_id(1)

    pc = cur_ref[0]
    row = lax.broadcasted_iota(jnp.int32, (blk, 1), 0)
    pv = prev_ref[0][SUBLANES - 1:SUBLANES, :] * (block > 0).astype(F32)
    nx = next_ref[0][0:1, :] * (block < n_blocks - 1).astype(F32)
    prev = jnp.where(row == 0, pv, pltpu.roll(pc, 1, 0))
    nxt = jnp.where(row == blk - 1, nx, pltpu.roll(pc, blk - 1, 0))
    mu = mu_ref[...]
    ps = pc + mu[0:1] * (prev - pc) + mu[1:2] * (nxt - pc)
    r, k, v = ps[:, 0:w], ps[:, w:2 * w], ps[:, 2 * w:3 * w]
    wd = ps[:, 3 * w:3 * w + LORA_COLS]
    ad = ps[:, 3 * w + LORA_COLS:3 * w + 2 * LORA_COLS]
    gd = ps[:, 3 * w + 2 * LORA_COLS:3 * w + 3 * LORA_COLS]

    head_sum = hs_ref[...]

    def per_head_sum(x):
        return _dot(x.astype(BF16), head_sum)

    kkr = k * kk_ref[...]
    kk = kkr / jnp.maximum(jnp.sqrt(per_head_sum(kkr * kkr)), 1e-12)
    bonus_ref[0] = per_head_sum(r * k * rk_ref[...]) * v
    gate_ref[0] = _dot(jax.nn.sigmoid(gd), g2_ref[...])

    ti = lax.broadcasted_iota(jnp.int32, (ch, ch), 0)
    tj = lax.broadcasted_iota(jnp.int32, (ch, ch), 1)
    eye = (ti == tj).astype(F32)
    lane = lax.broadcasted_iota(jnp.int32, (1, pw), 1)
    head_lanes = (lane < RW_HEAD, lane >= RW_HEAD)
    pi = lax.broadcasted_iota(jnp.int32, (pw, pw), 0)
    pj = lax.broadcasted_iota(jnp.int32, (pw, pw), 1)
    same_head = (pi < RW_HEAD) == (pj < RW_HEAD)
    eye_pair = (pi == pj).astype(F32)
    v16 = v.astype(BF16)
    tanh_wd = jnp.tanh(wd)

    probs = []
    pairs = []
    bi = lax.broadcasted_iota(jnp.int32, (blk, blk), 0)
    bj = lax.broadcasted_iota(jnp.int32, (blk, blk), 1)
    same_chunk = (bi >> (ch.bit_length() - 1)) == (bj >> (ch.bit_length() - 1))
    for d in range(2):
        z = w0_ref[d:d + 1] + _dot(tanh_wd, w2_ref[d])
        logdec = -jax.nn.sigmoid(z) * math.exp(-0.5)
        a = jax.nn.sigmoid(a0_ref[d:d + 1] + _dot(ad, a2_ref[d]))
        kd = k * (1.0 + (a - 1.0) * ka_ref[...])
        before = (tj < ti) if d == 0 else (tj > ti)
        upto = before | (tj == ti)
        upto_blk = same_chunk & ((bj <= bi) if d == 0 else (bj >= bi))
        cum = _dot(upto_blk.astype(F32), logdec, HIGHEST)
        w_in, w_ex, w_inv = jnp.exp(cum), jnp.exp(cum - logdec), jnp.exp(-cum)
        a_bar = -kk * w_ex
        b_bar = (kk * a * w_inv).astype(BF16)
        k_bar = (kd * w_inv).astype(BF16)
        r_bar = r * (w_in if d == 0 else w_ex)
        ymask = upto if d == 0 else before
        later, earlier = (ti, tj) if d == 0 else (tj, ti)
        off_masks = []
        for lvl in range(int(math.log2(ch))):
            same_pair = (ti >> (lvl + 1)) == (tj >> (lvl + 1))
            off_masks.append(same_pair & (((later >> lvl) & 1) == 1) & (((earlier >> lvl) & 1) == 0))
        for c in range(n_sub):
            rows = slice(c * ch, (c + 1) * ch)
            last = (c + 1) * ch - 1 if d == 0 else c * ch
            w_tot = w_in[last:last + 1]
            for p in range(RW_PAIRS):
                sl = slice(p * pw, (p + 1) * pw)
                pairs.append(dict(c=c, d=d, p=p, b=b_bar[rows, sl], k=k_bar[rows, sl], v=v16[rows, sl],
                                  r=r_bar[rows, sl], w_tot=w_tot[:, sl]))
                for hh in range(2):
                    mh = head_lanes[hh]
                    probs.append(dict(pair=len(pairs) - 1, before=before, ymask=ymask, off=off_masks,
                                      a=jnp.where(mh, a_bar[rows, sl], 0.0).astype(BF16),
                                      r=jnp.where(mh, r_bar[rows, sl], 0.0).astype(BF16),
                                      v=jnp.where(mh, v16[rows, sl], jnp.zeros_like(v16[rows, sl]))))

    for q in probs:
        pr = pairs[q["pair"]]
        q["ab"] = _dot_nt(q["a"], pr["b"])
        ak = _dot_nt(q["a"], pr["k"])
        q["rb"] = jnp.where(q["ymask"], _dot_nt(q["r"], pr["b"]), 0.0).astype(BF16)
        rk = _dot_nt(q["r"], pr["k"])
        q["ak"] = jnp.where(q["before"], ak, 0.0).astype(BF16)
        q["rk"] = jnp.where(q["ymask"], rk, 0.0).astype(BF16)
    for q in probs:
        q["akv"] = _dot(q["ak"], q["v"])
        q["rkv"] = _dot(q["rk"], q["v"])
        q["inv"] = eye + jnp.where(q["off"][0], q["ab"], 0.0)
    for lvl in range(1, int(math.log2(ch))):
        for q in probs:
            q["tmp"] = _dot(q["inv"].astype(BF16), jnp.where(q["off"][lvl], q["ab"], 0.0).astype(BF16))
        for q in probs:
            q["inv"] = q["inv"] + _dot(q["tmp"].astype(BF16), q["inv"].astype(BF16))
    for q in probs:
        rhs = jnp.concatenate([q["a"], q["akv"].astype(BF16)], axis=1)
        q["x"] = _dot(q["inv"].astype(BF16), rhs)
    for q in probs:
        q["z"] = _dot(q["rb"], q["x"].astype(BF16))
    for i, pr in enumerate(pairs):
        q0, q1 = probs[2 * i], probs[2 * i + 1]
        x = q0["x"] + q1["x"]
        z = q0["z"] + q1["z"]
        ra = pr["r"] + z[:, :pw]
        y0 = z[:, pw:] + q0["rkv"] + q1["rkv"]
        xb = _dot_tn(x.astype(BF16), pr["b"])
        vk = _dot_tn(pr["v"], pr["k"])
        g = (eye_pair + jnp.where(same_head, xb[:pw], 0.0)) * pr["w_tot"]
        h = jnp.where(same_head, xb[pw:] + vk, 0.0) * pr["w_tot"]
        c, d, p = pr["c"], pr["d"], pr["p"]
        sl = slice(p * pw, (p + 1) * pw)
        rows = slice(c * ch, (c + 1) * ch)
        ra_ref[0, d, rows, sl] = ra.astype(BF16)
        y0_ref[0, d, rows, sl] = y0.astype(BF16)
        g_ref[0, d, c, :, sl] = g.astype(BF16)
        h_ref[0, d, c, :, sl] = h.astype(BF16)


def _pad_lora(w2):
    keep = jnp.arange(2)[:, None, None, None] == jnp.arange(2)[None, :, None, None]
    return jnp.where(keep, w2[None], 0.0).reshape(2, -1, w2.shape[-1])


def _rwkv_chunks(p_rw3, mu, w0, w2, a0, a2, g2, k_k, k_a, r_k, chunks_per_step=2):
    bsz, seq, cols = p_rw3.shape
    ch, w, pw = RW_CHUNK, RW_WIDTH, 2 * RW_HEAD
    blk = chunks_per_step * ch
    n_chunks = seq // ch
    n_blocks = seq // blk
    sub_per_blk = blk // SUBLANES
    n_sub = seq // SUBLANES
    full = lambda shape: pl.BlockSpec(shape, lambda b, c: (0,) * len(shape))
    row = lambda x: x.reshape(1, w)
    head_sum = (jnp.arange(w)[:, None] // RW_HEAD == jnp.arange(w)[None, :] // RW_HEAD).astype(BF16)
    tok = lambda dt: jax.ShapeDtypeStruct((bsz, 2, seq, w), dt)
    mat = lambda dt: jax.ShapeDtypeStruct((bsz, 2, n_chunks, pw, w), dt)
    tok_spec = pl.BlockSpec((1, 2, blk, w), lambda b, c: (b, 0, c, 0))
    mat_spec = pl.BlockSpec((1, 2, chunks_per_step, pw, w), lambda b, c: (b, 0, c, 0, 0))
    one_spec = pl.BlockSpec((1, blk, w), lambda b, c: (b, c, 0))
    return pl.pallas_call(
        functools.partial(_rwkv_chunk_kernel, n_blocks),
        grid=(bsz, n_blocks),
        in_specs=[pl.BlockSpec((1, blk, cols), lambda b, c: (b, c, 0)),
                  pl.BlockSpec((1, SUBLANES, cols), lambda b, c: (b, jnp.maximum(c * sub_per_blk - 1, 0), 0)),
                  pl.BlockSpec((1, SUBLANES, cols),
                               lambda b, c: (b, jnp.minimum((c + 1) * sub_per_blk, n_sub - 1), 0)),
                  full((2, cols)), full((2, w)), full((2, LORA_COLS, w)), full((2, w)), full((2, LORA_COLS, w)),
                  full((LORA_COLS, w)), full((1, w)), full((1, w)), full((1, w)), full((w, w))],
        out_specs=[tok_spec, tok_spec, mat_spec, mat_spec, one_spec, one_spec],
        out_shape=[tok(BF16), tok(BF16), mat(BF16), mat(BF16),
                   jax.ShapeDtypeStruct((bsz, seq, w), F32), jax.ShapeDtypeStruct((bsz, seq, w), F32)],
        compiler_params=_params("parallel", "parallel"),
        name="rwkv7_chunks",
    )(p_rw3, p_rw3, p_rw3, mu, w0, _pad_lora(w2), a0, _pad_lora(a2), g2, row(k_k), row(k_a), row(r_k), head_sum)


def _rwkv_scan_kernel(ra0_ref, ra1_ref, y00_ref, y01_ref, g0_ref, g1_ref, h0_ref, h1_ref, o0_ref, o1_ref, st_ref):
    pw = 2 * RW_HEAD

    @pl.when(pl.program_id(1) == 0)
    def _():
        st_ref[...] = jnp.zeros_like(st_ref)

    ch = RW_CHUNK
    n_sub = g0_ref.shape[2]
    dirs = ((ra0_ref, y00_ref, g0_ref, h0_ref, o0_ref), (ra1_ref, y01_ref, g1_ref, h1_ref, o1_ref))
    states = {(d, p): st_ref[d, p] for d in range(2) for p in range(RW_PAIRS)}
    for step in range(n_sub):
        for d, (ra_ref, y0_ref, g_ref, h_ref, o_ref) in enumerate(dirs):
            c = step if d == 0 else n_sub - 1 - step
            rows = slice(c * ch, (c + 1) * ch)
            for p in range(RW_PAIRS):
                sl = slice(p * pw, (p + 1) * pw)
                s16 = states[d, p].astype(BF16)
                o_ref[0, rows, sl] = _dot_nt(ra_ref[0, 0, rows, sl], s16) + y0_ref[0, 0, rows, sl]
                states[d, p] = _dot(s16, g_ref[0, 0, c, :, sl]) + h_ref[0, 0, c, :, sl]
    for (d, p), s in states.items():
        st_ref[d, p] = s


def _rwkv_scan(ra, y0, g, h, n_sub=8):
    bsz, _, seq, w = ra.shape
    ch, pw = RW_CHUNK, 2 * RW_HEAD
    n_blk = seq // (ch * n_sub)
    blk = lambda d, c: c if d == 0 else n_blk - 1 - c
    tok = lambda d: pl.BlockSpec((1, 1, n_sub * ch, w), lambda b, c: (b, d, blk(d, c), 0))
    mat = lambda d: pl.BlockSpec((1, 1, n_sub, pw, w), lambda b, c: (b, d, blk(d, c), 0, 0))
    out = lambda d: pl.BlockSpec((1, n_sub * ch, w), lambda b, c: (b, blk(d, c), 0))
    return pl.pallas_call(
        _rwkv_scan_kernel,
        grid=(bsz, n_blk),
        in_specs=[tok(0), tok(1), tok(0), tok(1), mat(0), mat(1), mat(0), mat(1)],
        out_specs=[out(0), out(1)],
        out_shape=[jax.ShapeDtypeStruct((bsz, seq, w), F32)] * 2,
        scratch_shapes=[pltpu.VMEM((2, RW_PAIRS, pw, pw), F32)],
        compiler_params=_params("parallel", "arbitrary"),
        name="rwkv7_scan",
    )(ra, ra, y0, y0, g, g, h, h)


def _outproj_kernel(att_ref, yf_ref, yb_ref, bonus_ref, gate_ref, lng_ref, lnb_ref, hs_ref, x_ref, mod_ref,
                    wa_ref, wr_ref, g_ref, x1_ref, h2_ref, h2t_ref):
    head_sum = hs_ref[...]

    def per_head_mean(v):
        return _dot(v.astype(BF16), head_sum) * (1.0 / RW_HEAD)

    y = yf_ref[...] + yb_ref[...]
    yc = y - per_head_mean(y)
    yn = yc * lax.rsqrt(per_head_mean(yc * yc) + GN_EPS) * lng_ref[...] + lnb_ref[...]
    rw = (yn + bonus_ref[...]) * gate_ref[...]
    acc = _dot(att_ref[...].astype(BF16), wa_ref[...]) + _dot(rw.astype(BF16), wr_ref[...])
    m = mod_ref[0]
    x1 = x_ref[...] + m[2:3] * acc
    x1_ref[...] = x1
    ms = jnp.mean(x1 * x1, axis=-1, keepdims=True)
    yo = x1 * lax.rsqrt(ms + NORM_EPS) * g_ref[...]
    h2 = yo * (1.0 + m[4:5]) + m[3:4]
    h2_ref[...] = h2.astype(BF16)
    h2t_ref[...] = h2.T.astype(BF16)


def _out_proj(att2, y_f, y_b, bonus, gate, ln_g, ln_b, x2, mod3, w_a, w_r, g, seq, tm=512):
    t, d = x2.shape
    ka, kr = att2.shape[1], y_f.shape[1]
    per_b = seq // tm
    rows = lambda n: pl.BlockSpec((tm, n), lambda i: (i, 0))
    full = lambda shape: pl.BlockSpec(shape, lambda i: (0,) * len(shape))
    head_sum = (jnp.arange(kr)[:, None] // RW_HEAD == jnp.arange(kr)[None, :] // RW_HEAD).astype(BF16)
    return pl.pallas_call(
        _outproj_kernel,
        grid=(t // tm,),
        in_specs=[rows(ka), rows(kr), rows(kr), rows(kr), rows(kr), full((1, kr)), full((1, kr)), full((kr, kr)),
                  rows(d), pl.BlockSpec((1, 6, d), lambda i: (i // per_b, 0, 0)),
                  full((ka, d)), full((kr, d)), full((1, d))],
        out_specs=[rows(d), rows(d), pl.BlockSpec((d, tm), lambda i: (0, i))],
        out_shape=[jax.ShapeDtypeStruct((t, d), F32), jax.ShapeDtypeStruct((t, d), BF16),
                   jax.ShapeDtypeStruct((d, t), BF16)],
        compiler_params=_params("arbitrary"),
        name="out_proj",
    )(att2, y_f, y_b, bonus, gate, ln_g.reshape(1, kr), ln_b.reshape(1, kr), head_sum, x2, mod3, w_a, w_r,
      g.reshape(1, d))


def _topk_ranks(scores, vals_refs, rank_refs):
    nk, tm = scores[0].shape
    kiota = lax.broadcasted_iota(jnp.int32, (nk, tm), 0)

    def body(j, carry):
        out = []
        for (cur, rank), vals_ref in zip(carry, vals_refs):
            m = jnp.max(cur, axis=0, keepdims=True)
            hit = kiota == jnp.min(jnp.where(cur == m, kiota, nk), axis=0, keepdims=True)
            vals_ref[pl.ds(j, 1), :] = m
            out.append((jnp.where(hit, NEG_INF, cur), jnp.where(hit, j, rank)))
        return tuple(out)

    init = tuple((s, jnp.full((nk, tm), PEER_TOPK, jnp.int32)) for s in scores)
    for (_, rank), rank_ref in zip(lax.fori_loop(0, PEER_TOPK, body, init), rank_refs):
        rank_ref[...] = rank


def _compare_exchange(rows, i, j):
    a, b = rows[i], rows[j]
    rows[i], rows[j] = jnp.maximum(a, b), jnp.minimum(a, b)


def _bitonic_merge_desc(rows):
    stride = len(rows) // 2
    while stride >= 1:
        for i in range(len(rows)):
            if not i & stride:
                _compare_exchange(rows, i, i + stride)
        stride //= 2


def _bitonic_sort_desc(rows):
    n, size = len(rows), 2
    while size <= n:
        stride = size // 2
        while stride >= 1:
            for i in range(n):
                if not i & stride:
                    lo, hi = (i, i + stride) if (size == n or not i & size) else (i + stride, i)
                    _compare_exchange(rows, lo, hi)
            stride //= 2
        size *= 2


def _top16_sorted(x):
    n = PEER_TOPK
    rows = [x[SUBLANES * r:SUBLANES * (r + 1)] for r in range(x.shape[0] // SUBLANES)]
    rows += [jnp.full_like(rows[0], NEG_INF)] * (n - len(rows))
    _bitonic_sort_desc(rows)
    shift = SUBLANES // 2
    while shift >= 1:
        other = [pltpu.roll(rows[n - 1 - r], shift, 0) for r in range(n)]
        rows = [jnp.maximum(rows[r], other[r]) for r in range(n)]
        _bitonic_merge_desc(rows)
        shift //= 2
    return rows


def _top16_values(s, vals_ref):
    n = PEER_TOPK
    rows = _top16_sorted(s)
    for r in range(n):
        vals_ref[r:r + 1, :] = rows[r][0:1]
    distinct = rows[0] > rows[1]
    for r in range(1, n - 1):
        distinct = distinct & (rows[r] > rows[r + 1])
    at_least = jnp.zeros_like(rows[0])
    for r in range(n):
        at_least = at_least + jnp.where(s[SUBLANES * r:SUBLANES * (r + 1)] >= rows[n - 1], 1.0, 0.0)
    count = jnp.sum(at_least, axis=0, keepdims=True)
    return jnp.where(distinct[0:1] & (count == n), 0.0, 1.0)


def _pair_candidates():
    pieces = []
    for lvl in range(4):
        cnt = PEER_TOPK // (lvl + 1)
        pieces.append(("row", lvl, lvl, cnt))
        if lvl + 1 < cnt:
            pieces.append(("col", lvl, lvl + 1, cnt))
    return pieces


def _select_pairs(cand, ci, valid, sel_ref):
    big = 4 * PEER_TOPK * PEER_TOPK

    def body(j, cur):
        m = jnp.max(cur, axis=0, keepdims=True)
        hit = ci == jnp.min(jnp.where(cur == m, ci, big), axis=0, keepdims=True)
        return jnp.where(hit, NEG_INF, cur)

    cur = lax.fori_loop(0, PEER_TOPK, body, cand)
    sel_ref[...] = jnp.where((cur == NEG_INF) & valid, 1.0, 0.0)


def _select_pairs_by_value(cand, sel_ref):
    worst = 0.0
    for g in range(cand.shape[1] // LANES):
        lanes = slice(g * LANES, (g + 1) * LANES)
        c = cand[:, lanes]
        kth = _top16_sorted(c)[PEER_TOPK - 1][0:1]
        sel = jnp.where(c >= kth, 1.0, 0.0)
        sel_ref[:, lanes] = sel
        worst = jnp.maximum(worst, jnp.max(jnp.abs(jnp.sum(sel, axis=0, keepdims=True) - PEER_TOPK)))
    return worst


def _route_kernel(h_ref, wq_ref, keys_ref, p1_ref, cnt_ref, p2_ref, rk2_ref, q_scr, v1_ref, v2_ref, rank_ref,
                  sel_ref, key_ref, kref_ref):
    tm = h_ref.shape[0]
    nk = PEER_NKEYS
    n_grp = tm // LANES
    q = _dot(h_ref[...], wq_ref[...]).astype(BF16)
    for hp in range(2 * PEER_HEADS):
        q_scr[hp] = q[:, hp * nk:(hp + 1) * nk]

    pieces = _pair_candidates()
    big = 4 * PEER_TOPK * PEER_TOPK
    cat = lambda parts: jnp.concatenate(parts, axis=1)

    def head(h, carry):
        s1 = _dot_nt(keys_ref[2 * h], q_scr[2 * h])
        s2 = _dot_nt(keys_ref[2 * h + 1], q_scr[2 * h + 1])
        probs = [s[:, g * LANES:(g + 1) * LANES] for s in (s1, s2) for g in range(n_grp)]
        vals = [ref.at[g] for ref in (v1_ref, v2_ref) for g in range(n_grp)]
        ranks = [rank_ref.at[i] for i in range(2 * n_grp)]
        ambiguous = jnp.max(cat([_top16_values(prob, val) for prob, val in zip(probs, vals)]))
        key_ref[0] = s1
        key_ref[1] = s2
        kref_ref[0] = cat([v1_ref[g] for g in range(n_grp)])
        kref_ref[1] = cat([v2_ref[g] for g in range(n_grp)])

        @pl.when(ambiguous > 0.0)
        def _():
            _topk_ranks(probs, vals, ranks)
            rank_row = jnp.broadcast_to(lax.broadcasted_iota(jnp.int32, (PEER_TOPK, 1), 0).astype(F32),
                                        (PEER_TOPK, tm))
            for half in range(2):
                key_ref[half] = cat([rank_ref[half * n_grp + g] for g in range(n_grp)]).astype(F32)
                kref_ref[half] = rank_row

        v1 = cat([v1_ref[g] for g in range(n_grp)])
        v2 = cat([v2_ref[g] for g in range(n_grp)])

        cands, cis, valids = [], [], []
        for kind, fixed, lo, hi in pieces:
            rows = SUBLANES * ((hi + SUBLANES - 1) // SUBLANES)
            ridx = lax.broadcasted_iota(jnp.int32, (rows, 1), 0)
            valid = (ridx >= lo) & (ridx < hi)
            if kind == "row":
                vsum = v1[fixed:fixed + 1] + v2[0:rows]
                ci = fixed * PEER_TOPK + ridx
            else:
                vsum = v1[0:rows] + v2[fixed:fixed + 1]
                ci = ridx * PEER_TOPK + fixed
            cands.append(jnp.where(valid, vsum, NEG_INF))
            cis.append(jnp.broadcast_to(jnp.where(valid, ci, big), (rows, tm)))
            valids.append(jnp.broadcast_to(valid, (rows, tm)))
        cand = jnp.concatenate(cands, axis=0)
        ci = jnp.concatenate(cis, axis=0)
        valid = jnp.concatenate(valids, axis=0)
        off = _select_pairs_by_value(cand, sel_ref)

        @pl.when(off > 0.0)
        def _():
            _select_pairs(cand, ci, valid, sel_ref)

        sel = sel_ref[...]
        zsum = jnp.sum(sel * jnp.exp(cand - (v1[0:1] + v2[0:1])), axis=0, keepdims=True)
        aidx = lax.broadcasted_iota(jnp.int32, (PEER_TOPK, 1), 0)
        per_a = jnp.zeros((PEER_TOPK, tm), F32)
        row0 = 0
        for kind, fixed, lo, hi in pieces:
            rows = SUBLANES * ((hi + SUBLANES - 1) // SUBLANES)
            part = sel[row0:row0 + rows]
            row0 += rows
            if kind == "row":
                per_a = per_a + jnp.where(aidx == fixed, jnp.sum(part, axis=0, keepdims=True), 0.0)
            elif rows == PEER_TOPK:
                per_a = per_a + part
            else:
                per_a = per_a + jnp.concatenate([part, jnp.zeros((PEER_TOPK - rows, tm), F32)], axis=0)
        key1, key2 = key_ref[0], key_ref[1]
        kref1, kref2 = kref_ref[0], kref_ref[1]
        cnt = jnp.zeros((nk, tm), F32)
        rank2 = jnp.full((nk, tm), float(PEER_TOPK), F32)
        for a in range(PEER_TOPK):
            cnt = jnp.where(key1 == kref1[a:a + 1], per_a[a:a + 1], cnt)
            rank2 = jnp.where(key2 == kref2[a:a + 1], float(a), rank2)
        p1_ref[h] = jnp.exp(s1 - v1[0:1]) / (zsum * 2.0 ** 0.5)
        cnt_ref[h] = cnt
        p2_ref[h] = jnp.exp(s2 - v2[0:1]).astype(BF16)
        rk2_ref[h] = rank2.astype(BF16)
        return carry

    lax.fori_loop(0, PEER_HEADS, head, 0)


def _route(h2, wq, keys, tm=1024):
    t, d = h2.shape
    nq = wq.shape[1]
    nk = PEER_NKEYS
    out = lambda dt: jax.ShapeDtypeStruct((PEER_HEADS, nk, t), dt)
    ospec = pl.BlockSpec((PEER_HEADS, nk, tm), lambda i: (0, 0, i))
    return pl.pallas_call(
        _route_kernel,
        grid=(t // tm,),
        in_specs=[pl.BlockSpec((tm, d), lambda i: (i, 0)),
                  pl.BlockSpec((d, nq), lambda i: (0, 0)),
                  pl.BlockSpec((2 * PEER_HEADS, nk, nk), lambda i: (0, 0, 0))],
        out_specs=[ospec] * 4,
        out_shape=[out(F32), out(F32), out(BF16), out(BF16)],
        scratch_shapes=[pltpu.VMEM((2 * PEER_HEADS, tm, nk), BF16),
                        pltpu.VMEM((tm // LANES, PEER_TOPK, LANES), F32),
                        pltpu.VMEM((tm // LANES, PEER_TOPK, LANES), F32),
                        pltpu.VMEM((2 * tm // LANES, nk, LANES), jnp.int32),
                        pltpu.VMEM((sum(SUBLANES * ((hi + SUBLANES - 1) // SUBLANES)
                                        for _, _, _, hi in _pair_candidates()), tm), F32),
                        pltpu.VMEM((2, nk, tm), F32), pltpu.VMEM((2, PEER_TOPK, tm), F32)],
        compiler_params=_params("parallel"),
        name="peer_route",
    )(h2, wq, keys)


def _expert_kernel(n_eblk, h_ref, u_ref, vt_ref, p1_ref, cnt_ref, p2_ref, rk2_ref, x1_ref, mod_ref,
                   o_ref, acc_ref, act_ref, gate_ref):
    j = pl.program_id(1)
    te = u_ref.shape[0]
    tm = h_ref.shape[1]
    nk = PEER_NKEYS
    kb_rows = MXU_WIDTH
    per_kb = kb_rows // nk
    tile = 2 * SUBLANES

    @pl.when(j == 0)
    def _():
        acc_ref[...] = jnp.zeros_like(acc_ref)

    hb = h_ref[...]
    n_kb = te // kb_rows
    second_k = 2 * MXU_WIDTH

    first_rows = MXU_WIDTH

    def gates(kb):
        for sb in range(kb * per_kb, (kb + 1) * per_kb):
            e1 = j * (te // nk) + sb
            gate = None
            for h in range(PEER_HEADS):
                c = jnp.broadcast_to(cnt_ref[h, pl.ds(e1, 1), :], (tile, tm)).astype(BF16)
                p = jnp.broadcast_to(p1_ref[h, pl.ds(e1, 1), :], (tile, tm)).astype(BF16)
                rk2 = rk2_ref[h].reshape(nk // tile, tile, tm)
                p2 = p2_ref[h].reshape(nk // tile, tile, tm)
                term = jnp.where(rk2 < c[None], p2, jnp.zeros_like(p2)) * p[None]
                gate = term if gate is None else gate + term
            gate_ref[sb * nk:(sb + 1) * nk, :] = gate.reshape(nk, tm)

    for kb in range(n_kb):
        gates(kb)
    for r0 in range(0, te, first_rows):
        act_ref[r0:r0 + first_rows, :] = _dot(u_ref[r0:r0 + first_rows, :], hb)
    acc = acc_ref[...]
    for c0 in range(0, te, second_k):
        rows = slice(c0, c0 + second_k)
        t = act_ref[rows, :]
        w = gate_ref[rows, :] * (t + t * lax.erf(t)).astype(BF16)
        acc = acc + _dot(vt_ref[:, rows], w)
    acc_ref[...] = acc

    @pl.when(j == n_eblk - 1)
    def _():
        o_ref[...] = x1_ref[...] + mod_ref[0][5:6] * acc_ref[...].T


def _experts(h2t, u, vt, p1, cnt, p2, rk2, x1, mod3, seq, tm=512, te=2048):
    d, t = h2t.shape
    ne = u.shape[0]
    nk = PEER_NKEYS
    n_eblk = ne // te
    per_b = seq // tm
    rspec = pl.BlockSpec((PEER_HEADS, nk, tm), lambda i, j: (0, 0, i))
    return pl.pallas_call(
        functools.partial(_expert_kernel, n_eblk),
        grid=(t // tm, n_eblk),
        in_specs=[pl.BlockSpec((d, tm), lambda i, j: (0, i)),
                  pl.BlockSpec((te, d), lambda i, j: (j, 0)),
                  pl.BlockSpec((d, te), lambda i, j: (0, j)),
                  rspec, rspec, rspec, rspec,
                  pl.BlockSpec((tm, d), lambda i, j: (i, 0)),
                  pl.BlockSpec((1, 6, d), lambda i, j: (i // per_b, 0, 0))],
        out_specs=pl.BlockSpec((tm, d), lambda i, j: (i, 0)),
        out_shape=jax.ShapeDtypeStruct((t, d), F32),
        scratch_shapes=[pltpu.VMEM((d, tm), F32), pltpu.VMEM((te, tm), F32), pltpu.VMEM((te, tm), BF16)],
        compiler_params=_params("parallel", "arbitrary"),
        name="peer_experts",
    )(h2t, u, vt, p1, cnt, p2, rk2, x1, mod3)


def kernel(x, c, ada_w, ada_b, norm1_g, w_in, da_qnorm_g, da_knorm_g, da_lambda, da_subln_g, rw_shift_mu, rw_w0,
           rw_w2, rw_a0, rw_a2, rw_g2, rw_k_k, rw_k_a, rw_r_k, rw_ln_g, rw_ln_b, w_out, norm2_g, peer_wq,
           peer_keys, peer_u, peer_v):
    bsz, seq, d = x.shape
    depth = ada_w.shape[0]
    t = bsz * seq
    for l in range(depth):
        mod3 = _ada_mod(c, ada_w[l], ada_b[l]).reshape(bsz, 6, d)
        x2 = x.reshape(t, d)
        w_l = w_in[l].astype(BF16)
        p_att, p_rw = _in_proj(x2, mod3, norm1_g[l], w_l[:, :ATT_COLS], w_l[:, ATT_COLS:], seq)
        att = _diff_attention(p_att.reshape(bsz, seq, ATT_COLS), da_qnorm_g[l], da_knorm_g[l], da_lambda[l],
                              da_subln_g[l])
        ra, y0, g, h, bonus, gate = _rwkv_chunks(p_rw.reshape(bsz, seq, RWKV_COLS), rw_shift_mu[l], rw_w0[l],
                                                 rw_w2[l], rw_a0[l], rw_a2[l], rw_g2[l], rw_k_k[l], rw_k_a[l],
                                                 rw_r_k[l].reshape(-1))
        y_f, y_b = _rwkv_scan(ra, y0, g, h)
        wo = w_out[l].astype(BF16)
        aw = att.shape[-1]
        flat = lambda a: a.reshape(t, RW_WIDTH)
        x1, h2, h2t = _out_proj(att.reshape(t, aw), flat(y_f), flat(y_b), flat(bonus), flat(gate), rw_ln_g[l],
                                rw_ln_b[l], x2, mod3, wo[:aw], wo[aw:], norm2_g[l], seq)
        keys = peer_keys[l].reshape(2 * PEER_HEADS, PEER_NKEYS, -1).astype(BF16)
        p1, cnt, p2, rk2 = _route(h2, peer_wq[l].astype(BF16), keys)
        u_scaled = (peer_u[l] * 2.0 ** -0.5).astype(BF16)
        x = _experts(h2t, u_scaled, peer_v[l].T.astype(BF16), p1, cnt, p2, rk2, x1, mod3, seq).reshape(bsz, seq, d)
    return x
```

```python
import functools
import math

import jax
import jax.numpy as jnp
from jax import lax
from jax.experimental import pallas as pl
from jax.experimental.pallas import tpu as pltpu

F32 = jnp.float32
BF16 = jnp.bfloat16
HIGHEST = lax.Precision.HIGHEST

LANES = 128
SUBLANES = 8
MXU_WIDTH = 256
VMEM_LIMIT_BYTES = 56 * 1024 * 1024

DA_HEAD_DIM = 64
DA_HEADS = 4
RW_HEAD = 64
RW_HEADS = 8
RW_PAIRS = RW_HEADS // 2
RW_WIDTH = RW_HEADS * RW_HEAD
LORA_COLS = 128
RWKV_COLS = 3 * RW_WIDTH + 3 * LORA_COLS
ATT_COLS = 3 * 2 * DA_HEAD_DIM * DA_HEADS
GN_EPS = 64e-5
NORM_EPS = 1e-6
LAMBDA_INIT = 0.8 - 0.6 * math.exp(-0.3 * 0)
PEER_HEADS = 8
PEER_NKEYS = 128
PEER_TOPK = 16
RW_CHUNK = 64
NEG_INF = float("-inf")


def _params(*sem):
    return pltpu.CompilerParams(dimension_semantics=sem, vmem_limit_bytes=VMEM_LIMIT_BYTES)


def _dot(a, b, precision=None):
    return jnp.dot(a, b, preferred_element_type=F32, precision=precision)


def _dot_nt(a, b, precision=None):
    return lax.dot_general(a, b, (((1,), (1,)), ((), ())), preferred_element_type=F32, precision=precision)


def _dot_tn(a, b, precision=None):
    return lax.dot_general(a, b, (((0,), (0,)), ((), ())), preferred_element_type=F32, precision=precision)


def _group_matrix(n, group, value):
    shift = group.bit_length() - 1
    r = lax.broadcasted_iota(jnp.int32, (n, n), 0) >> shift
    c = lax.broadcasted_iota(jnp.int32, (n, n), 1) >> shift
    return jnp.where(r == c, value, 0.0).astype(F32)


def _ada_kernel(c_ref, w_ref, b_ref, o_ref):
    c = c_ref[...]
    s = c * jax.nn.sigmoid(c)
    o_ref[...] = _dot(s, w_ref[...], HIGHEST) + b_ref[...]


def _ada_mod(c, w, b):
    bsz, d = c.shape
    n = w.shape[1]
    tn = 1024
    return pl.pallas_call(
        _ada_kernel,
        grid=(n // tn,),
        in_specs=[pl.BlockSpec((bsz, d), lambda j: (0, 0)),
                  pl.BlockSpec((d, tn), lambda j: (0, j)),
                  pl.BlockSpec((1, tn), lambda j: (0, j))],
        out_specs=pl.BlockSpec((bsz, tn), lambda j: (0, j)),
        out_shape=jax.ShapeDtypeStruct((bsz, n), F32),
        compiler_params=_params("arbitrary"),
        name="ada_mod",
    )(c, w, b.reshape(1, n))


def _inproj_kernel(x_ref, mod_ref, g_ref, wa_ref, wr_ref, oa_ref, or_ref):
    x = x_ref[...]
    ms = jnp.mean(x * x, axis=-1, keepdims=True)
    y = x * lax.rsqrt(ms + NORM_EPS) * g_ref[...]
    m = mod_ref[0]
    h = (y * (1.0 + m[1:2]) + m[0:1]).astype(BF16)
    oa_ref[...] = _dot(h, wa_ref[...])
    or_ref[...] = _dot(h, wr_ref[...])


def _in_proj(x2, mod3, g, w_att, w_rw, seq, tm=512):
    t, d = x2.shape
    na, nr = w_att.shape[1], w_rw.shape[1]
    per_b = seq // tm
    return pl.pallas_call(
        _inproj_kernel,
        grid=(t // tm,),
        in_specs=[pl.BlockSpec((tm, d), lambda i: (i, 0)),
                  pl.BlockSpec((1, 6, d), lambda i: (i // per_b, 0, 0)),
                  pl.BlockSpec((1, d), lambda i: (0, 0)),
                  pl.BlockSpec((d, na), lambda i: (0, 0)),
                  pl.BlockSpec((d, nr), lambda i: (0, 0))],
        out_specs=[pl.BlockSpec((tm, na), lambda i: (i, 0)),
                   pl.BlockSpec((tm, nr), lambda i: (i, 0))],
        out_shape=[jax.ShapeDtypeStruct((t, na), F32), jax.ShapeDtypeStruct((t, nr), F32)],
        compiler_params=_params("arbitrary"),
        name="in_proj",
    )(x2, mod3, g.reshape(1, d), w_att, w_rw)


def _attn_kernel(tq, q_ref, k_ref, v_ref, qg_ref, kg_ref, lam_ref, slope_ref, sg_ref, o_ref, qs, ks, vs, bias_ref):
    seq = q_ref.shape[1]
    width = 2 * DA_HEAD_DIM
    avg = _group_matrix(width, DA_HEAD_DIM, 1.0 / DA_HEAD_DIM)

    def qk_norm(x, g):
        ms = _dot(x * x, avg)
        return x * lax.rsqrt(ms + NORM_EPS) * g

    qs[...] = (qk_norm(q_ref[0], qg_ref[...]) * (DA_HEAD_DIM ** -0.5)).astype(BF16)
    ks[...] = qk_norm(k_ref[0], kg_ref[...]).astype(BF16)
    ones_col = (lax.broadcasted_iota(jnp.int32, (seq, width), 1) == 0).astype(BF16)
    vs[...] = jnp.concatenate([v_ref[0].astype(BF16), ones_col], axis=1)

    lam = lam_ref[...]
    lam_full = (jnp.exp(jnp.sum(lam[0:1] * lam[1:2], axis=-1, keepdims=True))
                - jnp.exp(jnp.sum(lam[2:3] * lam[3:4], axis=-1, keepdims=True)) + LAMBDA_INIT)
    slope = slope_ref[0][:, 0:1]
    first = lax.broadcasted_iota(jnp.int32, (1, width), 1) < DA_HEAD_DIM
    sg = sg_ref[...] * (1.0 - LAMBDA_INIT)

    n_blk = seq // tq
    dist = lax.broadcasted_iota(jnp.int32, (tq, tq), 0) - lax.broadcasted_iota(jnp.int32, (tq, tq), 1)
    for d in range(2 * n_blk - 1):
        bias_ref[d] = slope * jnp.abs(dist - (d - (n_blk - 1)) * tq).astype(F32)

    def body(i, carry):
        r0 = pl.multiple_of(i * tq, tq)
        qt = qs[pl.ds(r0, tq), :]
        q0 = jnp.where(first, qt, jnp.zeros_like(qt))
        q1 = jnp.where(first, jnp.zeros_like(qt), qt)
        kk = ks[...]
        bias = jnp.concatenate([bias_ref[j - i + (n_blk - 1)] for j in range(n_blk)], axis=1)
        s0 = _dot_nt(q0, kk) - bias
        s1 = _dot_nt(q1, kk) - bias
        p0 = jnp.exp(s0 - jnp.max(s0, axis=-1, keepdims=True))
        p1 = jnp.exp(s1 - jnp.max(s1, axis=-1, keepdims=True))
        pv0 = _dot(p0.astype(BF16), vs[...])
        pv1 = _dot(p1.astype(BF16), vs[...])
        w0 = 1.0 / pv0[:, width:width + 1]
        w1 = lam_full / pv1[:, width:width + 1]
        o = pv0[:, :width] * w0 - pv1[:, :width] * w1
        ms = jnp.mean(o * o, axis=-1, keepdims=True)
        o_ref[0, pl.ds(r0, tq), :] = (o * lax.rsqrt(ms + NORM_EPS) * sg).astype(BF16)
        return carry

    lax.fori_loop(0, seq // tq, body, 0)


def _diff_attention(p_att3, qn_g, kn_g, lam, subln_g, tq=256):
    bsz, seq, _ = p_att3.shape
    width = 2 * DA_HEAD_DIM
    slopes = jnp.asarray([2.0 ** (-8.0 * (h + 1) / DA_HEADS) for h in range(DA_HEADS)], F32)
    slopes = jnp.broadcast_to(slopes[:, None, None], (DA_HEADS, 1, width))
    blk = lambda off: pl.BlockSpec((1, seq, width), lambda b, h: (b, 0, off + h))
    full = lambda shape: pl.BlockSpec(shape, lambda b, h: (0,) * len(shape))
    return pl.pallas_call(
        functools.partial(_attn_kernel, tq),
        grid=(bsz, DA_HEADS),
        in_specs=[blk(0), blk(DA_HEADS), blk(2 * DA_HEADS),
                  full((1, width)), full((1, width)), full((4, DA_HEAD_DIM)),
                  pl.BlockSpec((1, 1, width), lambda b, h: (h, 0, 0)),
                  full((1, width))],
        out_specs=pl.BlockSpec((1, seq, width), lambda b, h: (b, 0, h)),
        out_shape=jax.ShapeDtypeStruct((bsz, seq, DA_HEADS * width), BF16),
        scratch_shapes=[pltpu.VMEM((seq, width), BF16)] * 2 + [pltpu.VMEM((seq, 2 * width), BF16),
                                                               pltpu.VMEM((2 * (seq // tq) - 1, tq, tq), F32)],
        compiler_params=_params("arbitrary", "arbitrary"),
        name="diff_attention",
    )(p_att3, p_att3, p_att3,
      jnp.tile(qn_g.reshape(1, DA_HEAD_DIM), (1, 2)), jnp.tile(kn_g.reshape(1, DA_HEAD_DIM), (1, 2)),
      lam, slopes, subln_g.reshape(1, width))


def _rwkv_chunk_kernel(n_blocks, cur_ref, prev_ref, next_ref, mu_ref, w0_ref, w2_ref, a0_ref, a2_ref, g2_ref,
                       kk_ref, ka_ref, rk_ref, hs_ref, ra_ref, y0_ref, g_ref, h_ref, bonus_ref, gate_ref):
    ch = RW_CHUNK
    w = RW_WIDTH
    pw = 2 * RW_HEAD
    blk = cur_ref.shape[1]
    n_sub = blk // ch
    block = pl.program_id(1)

    pc = cur_ref[0]
    row = lax.broadcasted_iota(jnp.int32, (blk, 1), 0)
    pv = prev_ref[0][SUBLANES - 1:SUBLANES, :] * (block > 0).astype(F32)
    nx = next_ref[0][0:1, :] * (block < n_blocks - 1).astype(F32)
    prev = jnp.where(row == 0, pv, pltpu.roll(pc, 1, 0))
    nxt = jnp.where(row == blk - 1, nx, pltpu.roll(pc, blk - 1, 0))
    mu = mu_ref[...]
    ps = pc + mu[0:1] * (prev - pc) + mu[1:2] * (nxt - pc)
    r, k, v = ps[:, 0:w], ps[:, w:2 * w], ps[:, 2 * w:3 * w]
    wd = ps[:, 3 * w:3 * w + LORA_COLS]
    ad = ps[:, 3 * w + LORA_COLS:3 * w + 2 * LORA_COLS]
    gd = ps[:, 3 * w + 2 * LORA_COLS:3 * w + 3 * LORA_COLS]

    head_sum = hs_ref[...]

    def per_head_sum(x):
        return _dot(x.astype(BF16), head_sum)

    kkr = k * kk_ref[...]
    kk = kkr / jnp.maximum(jnp.sqrt(per_head_sum(kkr * kkr)), 1e-12)
    bonus_ref[0] = (per_head_sum(r * k * rk_ref[...]) * v).astype(BF16)
    gate_ref[0] = _dot(jax.nn.sigmoid(gd), g2_ref[...]).astype(BF16)

    ti = lax.broadcasted_iota(jnp.int32, (ch, ch), 0)
    tj = lax.broadcasted_iota(jnp.int32, (ch, ch), 1)
    eye = (ti == tj).astype(F32)
    lane = lax.broadcasted_iota(jnp.int32, (1, pw), 1)
    head_lanes = (lane < RW_HEAD, lane >= RW_HEAD)
    pi = lax.broadcasted_iota(jnp.int32, (pw, pw), 0)
    pj = lax.broadcasted_iota(jnp.int32, (pw, pw), 1)
    same_head = (pi < RW_HEAD) == (pj < RW_HEAD)
    eye_pair = (pi == pj).astype(F32)
    v16 = v.astype(BF16)
    tanh_wd = jnp.tanh(wd)

    probs = []
    pairs = []
    bi = lax.broadcasted_iota(jnp.int32, (blk, blk), 0)
    bj = lax.broadcasted_iota(jnp.int32, (blk, blk), 1)
    same_chunk = (bi >> (ch.bit_length() - 1)) == (bj >> (ch.bit_length() - 1))
    for d in range(2):
        z = w0_ref[d:d + 1] + _dot(tanh_wd, w2_ref[d])
        logdec = -jax.nn.sigmoid(z) * math.exp(-0.5)
        a = jax.nn.sigmoid(a0_ref[d:d + 1] + _dot(ad, a2_ref[d]))
        kd = k * (1.0 + (a - 1.0) * ka_ref[...])
        before = (tj < ti) if d == 0 else (tj > ti)
        upto = before | (tj == ti)
        upto_blk = same_chunk & ((bj <= bi) if d == 0 else (bj >= bi))
        cum = _dot(upto_blk.astype(F32), logdec, HIGHEST)
        w_in, w_ex, w_inv = jnp.exp(cum), jnp.exp(cum - logdec), jnp.exp(-cum)
        a_bar = -kk * w_ex
        b_bar = (kk * a * w_inv).astype(BF16)
        k_bar = (kd * w_inv).astype(BF16)
        r_bar = r * (w_in if d == 0 else w_ex)
        ymask = upto if d == 0 else before
        later, earlier = (ti, tj) if d == 0 else (tj, ti)
        off_masks = []
        for lvl in range(int(math.log2(ch))):
            same_pair = (ti >> (lvl + 1)) == (tj >> (lvl + 1))
            off_masks.append(same_pair & (((later >> lvl) & 1) == 1) & (((earlier >> lvl) & 1) == 0))
        for c in range(n_sub):
            rows = slice(c * ch, (c + 1) * ch)
            last = (c + 1) * ch - 1 if d == 0 else c * ch
            w_tot = w_in[last:last + 1]
            for p in range(RW_PAIRS):
                sl = slice(p * pw, (p + 1) * pw)
                pairs.append(dict(c=c, d=d, p=p, b=b_bar[rows, sl], k=k_bar[rows, sl], v=v16[rows, sl],
                                  r=r_bar[rows, sl], w_tot=w_tot[:, sl]))
                for hh in range(2):
                    mh = head_lanes[hh]
                    probs.append(dict(pair=len(pairs) - 1, before=before, ymask=ymask, off=off_masks,
                                      a=jnp.where(mh, a_bar[rows, sl], 0.0).astype(BF16),
                                      r=jnp.where(mh, r_bar[rows, sl], 0.0).astype(BF16),
                                      v=jnp.where(mh, v16[rows, sl], jnp.zeros_like(v16[rows, sl]))))

    for q in probs:
        pr = pairs[q["pair"]]
        q["ab"] = _dot_nt(q["a"], pr["b"])
        ak = _dot_nt(q["a"], pr["k"])
        q["rb"] = jnp.where(q["ymask"], _dot_nt(q["r"], pr["b"]), 0.0).astype(BF16)
        rk = _dot_nt(q["r"], pr["k"])
        q["ak"] = jnp.where(q["before"], ak, 0.0).astype(BF16)
        q["rk"] = jnp.where(q["ymask"], rk, 0.0).astype(BF16)
    for q in probs:
        q["akv"] = _dot(q["ak"], q["v"])
        q["rkv"] = _dot(q["rk"], q["v"])
        q["inv"] = eye + jnp.where(q["off"][0], q["ab"], 0.0)
    for lvl in range(1, int(math.log2(ch))):
        for q in probs:
            q["tmp"] = _dot(q["inv"].astype(BF16), jnp.where(q["off"][lvl], q["ab"], 0.0).astype(BF16))
        for q in probs:
            q["inv"] = q["inv"] + _dot(q["tmp"].astype(BF16), q["inv"].astype(BF16))
    for q in probs:
        rhs = jnp.concatenate([q["a"], q["akv"].astype(BF16)], axis=1)
        q["x"] = _dot(q["inv"].astype(BF16), rhs)
    for q in probs:
        q["z"] = _dot(q["rb"], q["x"].astype(BF16))
    for i, pr in enumerate(pairs):
        q0, q1 = probs[2 * i], probs[2 * i + 1]
        x = q0["x"] + q1["x"]
        z = q0["z"] + q1["z"]
        ra = pr["r"] + z[:, :pw]
        y0 = z[:, pw:] + q0["rkv"] + q1["rkv"]
        xb = _dot_tn(x.astype(BF16), pr["b"])
        vk = _dot_tn(pr["v"], pr["k"])
        g = (eye_pair + jnp.where(same_head, xb[:pw], 0.0)) * pr["w_tot"]
        h = jnp.where(same_head, xb[pw:] + vk, 0.0) * pr["w_tot"]
        c, d, p = pr["c"], pr["d"], pr["p"]
        sl = slice(p * pw, (p + 1) * pw)
        rows = slice(c * ch, (c + 1) * ch)
        ra_ref[0, d, rows, sl] = ra.astype(BF16)
        y0_ref[0, d, rows, sl] = y0.astype(BF16)
        g_ref[0, d, c, :, sl] = g.astype(BF16)
        h_ref[0, d, c, :, sl] = h.astype(BF16)


def _pad_lora(w2):
    keep = jnp.arange(2)[:, None, None, None] == jnp.arange(2)[None, :, None, None]
    return jnp.where(keep, w2[None], 0.0).reshape(2, -1, w2.shape[-1])


def _rwkv_chunks(p_rw3, mu, w0, w2, a0, a2, g2, k_k, k_a, r_k, chunks_per_step=2):
    bsz, seq, cols = p_rw3.shape
    ch, w, pw = RW_CHUNK, RW_WIDTH, 2 * RW_HEAD
    blk = chunks_per_step * ch
    n_chunks = seq // ch
    n_blocks = seq // blk
    sub_per_blk = blk // SUBLANES
    n_sub = seq // SUBLANES
    full = lambda shape: pl.BlockSpec(shape, lambda b, c: (0,) * len(shape))
    row = lambda x: x.reshape(1, w)
    head_sum = (jnp.arange(w)[:, None] // RW_HEAD == jnp.arange(w)[None, :] // RW_HEAD).astype(BF16)
    tok = lambda dt: jax.ShapeDtypeStruct((bsz, 2, seq, w), dt)
    mat = lambda dt: jax.ShapeDtypeStruct((bsz, 2, n_chunks, pw, w), dt)
    tok_spec = pl.BlockSpec((1, 2, blk, w), lambda b, c: (b, 0, c, 0))
    mat_spec = pl.BlockSpec((1, 2, chunks_per_step, pw, w), lambda b, c: (b, 0, c, 0, 0))
    one_spec = pl.BlockSpec((1, blk, w), lambda b, c: (b, c, 0))
    return pl.pallas_call(
        functools.partial(_rwkv_chunk_kernel, n_blocks),
        grid=(bsz, n_blocks),
        in_specs=[pl.BlockSpec((1, blk, cols), lambda b, c: (b, c, 0)),
                  pl.BlockSpec((1, SUBLANES, cols), lambda b, c: (b, jnp.maximum(c * sub_per_blk - 1, 0), 0)),
                  pl.BlockSpec((1, SUBLANES, cols),
                               lambda b, c: (b, jnp.minimum((c + 1) * sub_per_blk, n_sub - 1), 0)),
                  full((2, cols)), full((2, w)), full((2, LORA_COLS, w)), full((2, w)), full((2, LORA_COLS, w)),
                  full((LORA_COLS, w)), full((1, w)), full((1, w)), full((1, w)), full((w, w))],
        out_specs=[tok_spec, tok_spec, mat_spec, mat_spec, one_spec, one_spec],
        out_shape=[tok(BF16), tok(BF16), mat(BF16), mat(BF16),
                   jax.ShapeDtypeStruct((bsz, seq, w), BF16), jax.ShapeDtypeStruct((bsz, seq, w), BF16)],
        compiler_params=_params("parallel", "parallel"),
        name="rwkv7_chunks",
    )(p_rw3, p_rw3, p_rw3, mu, w0, _pad_lora(w2), a0, _pad_lora(a2), g2, row(k_k), row(k_a), row(r_k), head_sum)


def _rwkv_scan_kernel(ra0_ref, ra1_ref, y00_ref, y01_ref, g0_ref, g1_ref, h0_ref, h1_ref, o0_ref, o1_ref, st_ref):
    pw = 2 * RW_HEAD

    @pl.when(pl.program_id(1) == 0)
    def _():
        st_ref[...] = jnp.zeros_like(st_ref)

    ch = RW_CHUNK
    n_sub = g0_ref.shape[2]
    dirs = ((ra0_ref, y00_ref, g0_ref, h0_ref, o0_ref), (ra1_ref, y01_ref, g1_ref, h1_ref, o1_ref))
    states = {(d, p): st_ref[d, p] for d in range(2) for p in range(RW_PAIRS)}
    for step in range(n_sub):
        for d, (ra_ref, y0_ref, g_ref, h_ref, o_ref) in enumerate(dirs):
            c = step if d == 0 else n_sub - 1 - step
            rows = slice(c * ch, (c + 1) * ch)
            for p in range(RW_PAIRS):
                sl = slice(p * pw, (p + 1) * pw)
                s16 = states[d, p].astype(BF16)
                o_ref[0, rows, sl] = (_dot_nt(ra_ref[0, 0, rows, sl], s16) + y0_ref[0, 0, rows, sl]).astype(BF16)
                states[d, p] = _dot(s16, g_ref[0, 0, c, :, sl]) + h_ref[0, 0, c, :, sl]
    for (d, p), s in states.items():
        st_ref[d, p] = s


def _rwkv_scan(ra, y0, g, h, n_sub=8):
    bsz, _, seq, w = ra.shape
    ch, pw = RW_CHUNK, 2 * RW_HEAD
    n_blk = seq // (ch * n_sub)
    blk = lambda d, c: c if d == 0 else n_blk - 1 - c
    tok = lambda d: pl.BlockSpec((1, 1, n_sub * ch, w), lambda b, c: (b, d, blk(d, c), 0))
    mat = lambda d: pl.BlockSpec((1, 1, n_sub, pw, w), lambda b, c: (b, d, blk(d, c), 0, 0))
    out = lambda d: pl.BlockSpec((1, n_sub * ch, w), lambda b, c: (b, blk(d, c), 0))
    return pl.pallas_call(
        _rwkv_scan_kernel,
        grid=(bsz, n_blk),
        in_specs=[tok(0), tok(1), tok(0), tok(1), mat(0), mat(1), mat(0), mat(1)],
        out_specs=[out(0), out(1)],
        out_shape=[jax.ShapeDtypeStruct((bsz, seq, w), BF16)] * 2,
        scratch_shapes=[pltpu.VMEM((2, RW_PAIRS, pw, pw), F32)],
        compiler_params=_params("parallel", "arbitrary"),
        name="rwkv7_scan",
    )(ra, ra, y0, y0, g, g, h, h)


def _outproj_kernel(att_ref, yf_ref, yb_ref, bonus_ref, gate_ref, lng_ref, lnb_ref, hs_ref, x_ref, mod_ref,
                    wa_ref, wr_ref, g_ref, x1_ref, h2_ref, h2t_ref):
    head_sum = hs_ref[...]

    def per_head_mean(v):
        return _dot(v.astype(BF16), head_sum) * (1.0 / RW_HEAD)

    y = yf_ref[...].astype(F32) + yb_ref[...].astype(F32)
    yc = y - per_head_mean(y)
    yn = yc * lax.rsqrt(per_head_mean(yc * yc) + GN_EPS) * lng_ref[...] + lnb_ref[...]
    rw = (yn + bonus_ref[...]) * gate_ref[...]
    acc = _dot(att_ref[...].astype(BF16), wa_ref[...]) + _dot(rw.astype(BF16), wr_ref[...])
    m = mod_ref[0]
    x1 = x_ref[...] + m[2:3] * acc
    x1_ref[...] = x1
    ms = jnp.mean(x1 * x1, axis=-1, keepdims=True)
    yo = x1 * lax.rsqrt(ms + NORM_EPS) * g_ref[...]
    h2 = yo * (1.0 + m[4:5]) + m[3:4]
    h2_ref[...] = h2.astype(BF16)
    h2t_ref[...] = h2.T.astype(BF16)


def _out_proj(att2, y_f, y_b, bonus, gate, ln_g, ln_b, x2, mod3, w_a, w_r, g, seq, tm=512):
    t, d = x2.shape
    ka, kr = att2.shape[1], y_f.shape[1]
    per_b = seq // tm
    rows = lambda n: pl.BlockSpec((tm, n), lambda i: (i, 0))
    full = lambda shape: pl.BlockSpec(shape, lambda i: (0,) * len(shape))
    head_sum = (jnp.arange(kr)[:, None] // RW_HEAD == jnp.arange(kr)[None, :] // RW_HEAD).astype(BF16)
    return pl.pallas_call(
        _outproj_kernel,
        grid=(t // tm,),
        in_specs=[rows(ka), rows(kr), rows(kr), rows(kr), rows(kr), full((1, kr)), full((1, kr)), full((kr, kr)),
                  rows(d), pl.BlockSpec((1, 6, d), lambda i: (i // per_b, 0, 0)),
                  full((ka, d)), full((kr, d)), full((1, d))],
        out_specs=[rows(d), rows(d), pl.BlockSpec((d, tm), lambda i: (0, i))],
        out_shape=[jax.ShapeDtypeStruct((t, d), F32), jax.ShapeDtypeStruct((t, d), BF16),
                   jax.ShapeDtypeStruct((d, t), BF16)],
        compiler_params=_params("arbitrary"),
        name="out_proj",
    )(att2, y_f, y_b, bonus, gate, ln_g.reshape(1, kr), ln_b.reshape(1, kr), head_sum, x2, mod3, w_a, w_r,
      g.reshape(1, d))


def _topk_ranks(scores, vals_refs, rank_refs):
    nk, tm = scores[0].shape
    kiota = lax.broadcasted_iota(jnp.int32, (nk, tm), 0)

    def body(j, carry):
        out = []
        for (cur, rank), vals_ref in zip(carry, vals_refs):
            m = jnp.max(cur, axis=0, keepdims=True)
            hit = kiota == jnp.min(jnp.where(cur == m, kiota, nk), axis=0, keepdims=True)
            vals_ref[pl.ds(j, 1), :] = m
            out.append((jnp.where(hit, NEG_INF, cur), jnp.where(hit, j, rank)))
        return tuple(out)

    init = tuple((s, jnp.full((nk, tm), PEER_TOPK, jnp.int32)) for s in scores)
    for (_, rank), rank_ref in zip(lax.fori_loop(0, PEER_TOPK, body, init), rank_refs):
        rank_ref[...] = rank


def _compare_exchange(rows, i, j):
    a, b = rows[i], rows[j]
    rows[i], rows[j] = jnp.maximum(a, b), jnp.minimum(a, b)


def _bitonic_merge_desc(rows):
    stride = len(rows) // 2
    while stride >= 1:
        for i in range(len(rows)):
            if not i & stride:
                _compare_exchange(rows, i, i + stride)
        stride //= 2


def _bitonic_sort_desc(rows):
    n, size = len(rows), 2
    while size <= n:
        stride = size // 2
        while stride >= 1:
            for i in range(n):
                if not i & stride:
                    lo, hi = (i, i + stride) if (size == n or not i & size) else (i + stride, i)
                    _compare_exchange(rows, lo, hi)
            stride //= 2
        size *= 2


def _top16_sorted(x):
    n = PEER_TOPK
    rows = [x[SUBLANES * r:SUBLANES * (r + 1)] for r in range(x.shape[0] // SUBLANES)]
    rows += [jnp.full_like(rows[0], NEG_INF)] * (n - len(rows))
    _bitonic_sort_desc(rows)
    shift = SUBLANES // 2
    while shift >= 1:
        other = [pltpu.roll(rows[n - 1 - r], shift, 0) for r in range(n)]
        rows = [jnp.maximum(rows[r], other[r]) for r in range(n)]
        _bitonic_merge_desc(rows)
        shift //= 2
    return rows


def _top16_values(s, vals_ref):
    n = PEER_TOPK
    rows = _top16_sorted(s)
    for r in range(n):
        vals_ref[r:r + 1, :] = rows[r][0:1]
    distinct = rows[0] > rows[1]
    for r in range(1, n - 1):
        distinct = distinct & (rows[r] > rows[r + 1])
    at_least = jnp.zeros_like(rows[0])
    for r in range(n):
        at_least = at_least + jnp.where(s[SUBLANES * r:SUBLANES * (r + 1)] >= rows[n - 1], 1.0, 0.0)
    count = jnp.sum(at_least, axis=0, keepdims=True)
    return jnp.where(distinct[0:1] & (count == n), 0.0, 1.0)


def _pair_candidates():
    pieces = []
    for lvl in range(4):
        cnt = PEER_TOPK // (lvl + 1)
        pieces.append(("row", lvl, lvl, cnt))
        if lvl + 1 < cnt:
            pieces.append(("col", lvl, lvl + 1, cnt))
    return pieces


def _select_pairs(cand, ci, valid, sel_ref):
    big = 4 * PEER_TOPK * PEER_TOPK

    def body(j, cur):
        m = jnp.max(cur, axis=0, keepdims=True)
        hit = ci == jnp.min(jnp.where(cur == m, ci, big), axis=0, keepdims=True)
        return jnp.where(hit, NEG_INF, cur)

    cur = lax.fori_loop(0, PEER_TOPK, body, cand)
    sel_ref[...] = jnp.where((cur == NEG_INF) & valid, 1.0, 0.0)


def _select_pairs_by_value(cand, sel_ref):
    worst = 0.0
    for g in range(cand.shape[1] // LANES):
        lanes = slice(g * LANES, (g + 1) * LANES)
        c = cand[:, lanes]
        kth = _top16_sorted(c)[PEER_TOPK - 1][0:1]
        sel = jnp.where(c >= kth, 1.0, 0.0)
        sel_ref[:, lanes] = sel
        worst = jnp.maximum(worst, jnp.max(jnp.abs(jnp.sum(sel, axis=0, keepdims=True) - PEER_TOPK)))
    return worst


def _route_kernel(h_ref, wq_ref, keys_ref, p1_ref, cnt_ref, p2_ref, rk2_ref, q_scr, v1_ref, v2_ref, rank_ref,
                  sel_ref, key_ref, kref_ref):
    tm = h_ref.shape[0]
    nk = PEER_NKEYS
    n_grp = tm // LANES
    q = _dot(h_ref[...], wq_ref[...]).astype(BF16)
    for hp in range(2 * PEER_HEADS):
        q_scr[hp] = q[:, hp * nk:(hp + 1) * nk]

    pieces = _pair_candidates()
    big = 4 * PEER_TOPK * PEER_TOPK
    cat = lambda parts: jnp.concatenate(parts, axis=1)

    def head(h, carry):
        s1 = _dot_nt(keys_ref[2 * h], q_scr[2 * h])
        s2 = _dot_nt(keys_ref[2 * h + 1], q_scr[2 * h + 1])
        probs = [s[:, g * LANES:(g + 1) * LANES] for s in (s1, s2) for g in range(n_grp)]
        vals = [ref.at[g] for ref in (v1_ref, v2_ref) for g in range(n_grp)]
        ranks = [rank_ref.at[i] for i in range(2 * n_grp)]
        ambiguous = jnp.max(cat([_top16_values(prob, val) for prob, val in zip(probs, vals)]))
        key_ref[0] = s1
        key_ref[1] = s2
        kref_ref[0] = cat([v1_ref[g] for g in range(n_grp)])
        kref_ref[1] = cat([v2_ref[g] for g in range(n_grp)])

        @pl.when(ambiguous > 0.0)
        def _():
            _topk_ranks(probs, vals, ranks)
            rank_row = jnp.broadcast_to(lax.broadcasted_iota(jnp.int32, (PEER_TOPK, 1), 0).astype(F32),
                                        (PEER_TOPK, tm))
            for half in range(2):
                key_ref[half] = cat([rank_ref[half * n_grp + g] for g in range(n_grp)]).astype(F32)
                kref_ref[half] = rank_row

        v1 = cat([v1_ref[g] for g in range(n_grp)])
        v2 = cat([v2_ref[g] for g in range(n_grp)])

        cands, cis, valids = [], [], []
        for kind, fixed, lo, hi in pieces:
            rows = SUBLANES * ((hi + SUBLANES - 1) // SUBLANES)
            ridx = lax.broadcasted_iota(jnp.int32, (rows, 1), 0)
            valid = (ridx >= lo) & (ridx < hi)
            if kind == "row":
                vsum = v1[fixed:fixed + 1] + v2[0:rows]
                ci = fixed * PEER_TOPK + ridx
            else:
                vsum = v1[0:rows] + v2[fixed:fixed + 1]
                ci = ridx * PEER_TOPK + fixed
            cands.append(jnp.where(valid, vsum, NEG_INF))
            cis.append(jnp.broadcast_to(jnp.where(valid, ci, big), (rows, tm)))
            valids.append(jnp.broadcast_to(valid, (rows, tm)))
        cand = jnp.concatenate(cands, axis=0)
        ci = jnp.concatenate(cis, axis=0)
        valid = jnp.concatenate(valids, axis=0)
        off = _select_pairs_by_value(cand, sel_ref)

        @pl.when(off > 0.0)
        def _():
            _select_pairs(cand, ci, valid, sel_ref)

        sel = sel_ref[...]
        zsum = jnp.sum(sel * jnp.exp(cand - (v1[0:1] + v2[0:1])), axis=0, keepdims=True)
        aidx = lax.broadcasted_iota(jnp.int32, (PEER_TOPK, 1), 0)
        per_a = jnp.zeros((PEER_TOPK, tm), F32)
        row0 = 0
        for kind, fixed, lo, hi in pieces:
            rows = SUBLANES * ((hi + SUBLANES - 1) // SUBLANES)
            part = sel[row0:row0 + rows]
            row0 += rows
            if kind == "row":
                per_a = per_a + jnp.where(aidx == fixed, jnp.sum(part, axis=0, keepdims=True), 0.0)
            elif rows == PEER_TOPK:
                per_a = per_a + part
            else:
                per_a = per_a + jnp.concatenate([part, jnp.zeros((PEER_TOPK - rows, tm), F32)], axis=0)
        key1, key2 = key_ref[0], key_ref[1]
        kref1, kref2 = kref_ref[0], kref_ref[1]
        cnt = jnp.zeros((nk, tm), F32)
        rank2 = jnp.full((nk, tm), float(PEER_TOPK), F32)
        for a in range(PEER_TOPK):
            cnt = jnp.where(key1 == kref1[a:a + 1], per_a[a:a + 1], cnt)
            rank2 = jnp.where(key2 == kref2[a:a + 1], float(a), rank2)
        p1_ref[h] = jnp.exp(s1 - v1[0:1]) / (zsum * 2.0 ** 0.5)
        cnt_ref[h] = cnt
        p2_ref[h] = jnp.exp(s2 - v2[0:1]).astype(BF16)
        rk2_ref[h] = rank2.astype(BF16)
        return carry

    lax.fori_loop(0, PEER_HEADS, head, 0)


def _route(h2, wq, keys, tm=1024):
    t, d = h2.shape
    nq = wq.shape[1]
    nk = PEER_NKEYS
    out = lambda dt: jax.ShapeDtypeStruct((PEER_HEADS, nk, t), dt)
    ospec = pl.BlockSpec((PEER_HEADS, nk, tm), lambda i: (0, 0, i))
    return pl.pallas_call(
        _route_kernel,
        grid=(t // tm,),
        in_specs=[pl.BlockSpec((tm, d), lambda i: (i, 0)),
                  pl.BlockSpec((d, nq), lambda i: (0, 0)),
                  pl.BlockSpec((2 * PEER_HEADS, nk, nk), lambda i: (0, 0, 0))],
        out_specs=[ospec] * 4,
        out_shape=[out(F32), out(F32), out(BF16), out(BF16)],
        scratch_shapes=[pltpu.VMEM((2 * PEER_HEADS, tm, nk), BF16),
                        pltpu.VMEM((tm // LANES, PEER_TOPK, LANES), F32),
                        pltpu.VMEM((tm // LANES, PEER_TOPK, LANES), F32),
                        pltpu.VMEM((2 * tm // LANES, nk, LANES), jnp.int32),
                        pltpu.VMEM((sum(SUBLANES * ((hi + SUBLANES - 1) // SUBLANES)
                                        for _, _, _, hi in _pair_candidates()), tm), F32),
                        pltpu.VMEM((2, nk, tm), F32), pltpu.VMEM((2, PEER_TOPK, tm), F32)],
        compiler_params=_params("parallel"),
        name="peer_route",
    )(h2, wq, keys)


def _expert_kernel(n_eblk, h_ref, u_ref, vt_ref, p1_ref, cnt_ref, p2_ref, rk2_ref, x1_ref, mod_ref,
                   o_ref, acc_ref, act_ref, gate_ref):
    j = pl.program_id(1)
    te = u_ref.shape[0]
    tm = h_ref.shape[1]
    nk = PEER_NKEYS
    kb_rows = MXU_WIDTH
    per_kb = kb_rows // nk
    tile = 2 * SUBLANES

    @pl.when(j == 0)
    def _():
        acc_ref[...] = jnp.zeros_like(acc_ref)

    hb = h_ref[...]
    n_kb = te // kb_rows
    second_k = 2 * MXU_WIDTH

    first_rows = MXU_WIDTH

    def gates(kb):
        for sb in range(kb * per_kb, (kb + 1) * per_kb):
            e1 = j * (te // nk) + sb
            gate = None
            for h in range(PEER_HEADS):
                c = jnp.broadcast_to(cnt_ref[h, pl.ds(e1, 1), :], (tile, tm)).astype(BF16)
                p = jnp.broadcast_to(p1_ref[h, pl.ds(e1, 1), :], (tile, tm)).astype(BF16)
                rk2 = rk2_ref[h].reshape(nk // tile, tile, tm)
                p2 = p2_ref[h].reshape(nk // tile, tile, tm)
                term = jnp.where(rk2 < c[None], p2, jnp.zeros_like(p2)) * p[None]
                gate = term if gate is None else gate + term
            gate_ref[sb * nk:(sb + 1) * nk, :] = gate.reshape(nk, tm)

    for kb in range(n_kb):
        gates(kb)
    for r0 in range(0, te, first_rows):
        act_ref[r0:r0 + first_rows, :] = _dot(u_ref[r0:r0 + first_rows, :], hb)
    acc = acc_ref[...]
    for c0 in range(0, te, second_k):
        rows = slice(c0, c0 + second_k)
        t = act_ref[rows, :]
        w = gate_ref[rows, :] * (t + t * lax.erf(t)).astype(BF16)
        acc = acc + _dot(vt_ref[:, rows], w)
    acc_ref[...] = acc

    @pl.when(j == n_eblk - 1)
    def _():
        o_ref[...] = x1_ref[...] + mod_ref[0][5:6] * acc_ref[...].T


def _experts(h2t, u, vt, p1, cnt, p2, rk2, x1, mod3, seq, tm=512, te=2048):
    d, t = h2t.shape
    ne = u.shape[0]
    nk = PEER_NKEYS
    n_eblk = ne // te
    per_b = seq // tm
    rspec = pl.BlockSpec((PEER_HEADS, nk, tm), lambda i, j: (0, 0, i))
    return pl.pallas_call(
        functools.partial(_expert_kernel, n_eblk),
        grid=(t // tm, n_eblk),
        in_specs=[pl.BlockSpec((d, tm), lambda i, j: (0, i)),
                  pl.BlockSpec((te, d), lambda i, j: (j, 0)),
                  pl.BlockSpec((d, te), lambda i, j: (0, j)),
                  rspec, rspec, rspec, rspec,
                  pl.BlockSpec((tm, d), lambda i, j: (i, 0)),
                  pl.BlockSpec((1, 6, d), lambda i, j: (i // per_b, 0, 0))],
        out_specs=pl.BlockSpec((tm, d), lambda i, j: (i, 0)),
        out_shape=jax.ShapeDtypeStruct((t, d), F32),
        scratch_shapes=[pltpu.VMEM((d, tm), F32), pltpu.VMEM((te, tm), F32), pltpu.VMEM((te, tm), BF16)],
        compiler_params=_params("parallel", "arbitrary"),
        name="peer_experts",
    )(h2t, u, vt, p1, cnt, p2, rk2, x1, mod3)


def kernel(x, c, ada_w, ada_b, norm1_g, w_in, da_qnorm_g, da_knorm_g, da_lambda, da_subln_g, rw_shift_mu, rw_w0,
           rw_w2, rw_a0, rw_a2, rw_g2, rw_k_k, rw_k_a, rw_r_k, rw_ln_g, rw_ln_b, w_out, norm2_g, peer_wq,
           peer_keys, peer_u, peer_v):
    bsz, seq, d = x.shape
    depth = ada_w.shape[0]
    t = bsz * seq
    for l in range(depth):
        mod3 = _ada_mod(c, ada_w[l], ada_b[l]).reshape(bsz, 6, d)
        x2 = x.reshape(t, d)
        w_l = w_in[l].astype(BF16)
        p_att, p_rw = _in_proj(x2, mod3, norm1_g[l], w_l[:, :ATT_COLS], w_l[:, ATT_COLS:], seq)
        att = _diff_attention(p_att.reshape(bsz, seq, ATT_COLS), da_qnorm_g[l], da_knorm_g[l], da_lambda[l],
                              da_subln_g[l])
        ra, y0, g, h, bonus, gate = _rwkv_chunks(p_rw.reshape(bsz, seq, RWKV_COLS), rw_shift_mu[l], rw_w0[l],
                                                 rw_w2[l], rw_a0[l], rw_a2[l], rw_g2[l], rw_k_k[l], rw_k_a[l],
                                                 rw_r_k[l].reshape(-1))
        y_f, y_b = _rwkv_scan(ra, y0, g, h)
        wo = w_out[l].astype(BF16)
        aw = att.shape[-1]
        flat = lambda a: a.reshape(t, RW_WIDTH)
        x1, h2, h2t = _out_proj(att.reshape(t, aw), flat(y_f), flat(y_b), flat(bonus), flat(gate), rw_ln_g[l],
                                rw_ln_b[l], x2, mod3, wo[:aw], wo[aw:], norm2_g[l], seq)
        keys = peer_keys[l].reshape(2 * PEER_HEADS, PEER_NKEYS, -1).astype(BF16)
        p1, cnt, p2, rk2 = _route(h2, peer_wq[l].astype(BF16), keys)
        u_scaled = (peer_u[l] * 2.0 ** -0.5).astype(BF16)
        x = _experts(h2t, u_scaled, peer_v[l].T.astype(BF16), p1, cnt, p2, rk2, x1, mod3, seq).reshape(bsz, seq, d)
    return x
```
